```python
import math
import jax, jax.numpy as jnp
from jax import lax
import numpy as np

D_MODEL = 1024
BATCH = 8
SEQ = 2048
DEPTH = 2
DEC_BATCH = 128
DEC_SEQ = 1
PAST_LEN = 2048
PAGE_SIZE = 128

H_A = 4
DK_A = 64
A_W = H_A * 2 * DK_A
H_B = 4
DH_B = 64
B_W = H_B * DH_B
H_IDX = 8
D_IDX = 32
INDEX_TOPK_MAX = 256
H_C = 4
DH_C = 64
C_W = H_C * DH_C
FGATE_BIAS_MEAN = 2.0
CONV_W = 256
CONV_K = 3

N_BRANCH = 4
IN_WIDTH = 3 * A_W + 3 * B_W + H_IDX * D_IDX + D_IDX + H_IDX + 3 * C_W + H_C + 3 * CONV_W + N_BRANCH * D_MODEL
D_FF = -(-8 * D_MODEL // (3 * 256)) * 256
Q_BLOCK = 128
DSA_Q_BLOCK = 64
ALPHA = (2 * DEPTH) ** 0.25
BETA = (8 * DEPTH) ** -0.25
LN_EPS = 1e-5
NORM_EPS = 1e-6

kernel_name = "hybrid_gated_diff_dsa_fox_conv_step"


def _in_splits():
    widths = (A_W, A_W, A_W, B_W, B_W, B_W, H_IDX * D_IDX, D_IDX, H_IDX,
              C_W, C_W, C_W, H_C, CONV_W, CONV_W, CONV_W, N_BRANCH * D_MODEL)
    return [int(v) for v in np.cumsum(widths)[:-1]]


def _alibi_slopes():
    n = H_A + H_B
    s = 2.0 ** (-8.0 * (np.arange(n) + 1) / n)
    return jnp.asarray(s[0::2], jnp.float32), jnp.asarray(s[1::2], jnp.float32)


def _layernorm(x, g, b):
    xf = x.astype(jnp.float32)
    mu = jnp.mean(xf, -1, keepdims=True)
    var = jnp.mean(jnp.square(xf - mu), -1, keepdims=True)
    return ((xf - mu) * lax.rsqrt(var + LN_EPS) * g.astype(jnp.float32) + b.astype(jnp.float32)).astype(x.dtype)


def _blockwise(fn, q_arrays, q_pos, block):
    T = q_pos.shape[0]
    qb = block if T % block == 0 else T
    n = T // qb
    bsz = q_arrays[0].shape[0]
    xs = tuple(jnp.moveaxis(a.reshape((bsz, n, qb) + a.shape[2:]), 1, 0) for a in q_arrays) + (q_pos.reshape(n, qb),)
    out = lax.map(lambda blk: fn(*blk), xs)
    return jnp.moveaxis(out, 0, 1).reshape((bsz, T) + out.shape[3:])


def _diff_attention(q, kv, q_pos, slopes, lam):
    L = kv.shape[1]
    k_pos = jnp.arange(L, dtype=jnp.int32)
    k1 = kv[..., :DK_A]
    k2 = kv[..., DK_A:2 * DK_A]
    v = kv[..., 2 * DK_A:]
    scale = DK_A ** -0.5

    def blk(qb_, qp):
        dist = (qp[:, None] - k_pos[None, :]).astype(jnp.float32)
        bias = jnp.where(dist[None] >= 0, -slopes[:, None, None] * dist[None], -jnp.inf)
        s1 = jnp.einsum("bqhd,bkhd->bhqk", qb_[..., 0, :], k1).astype(jnp.float32) * scale + bias
        s2 = jnp.einsum("bqhd,bkhd->bhqk", qb_[..., 1, :], k2).astype(jnp.float32) * scale + bias
        p = jax.nn.softmax(s1, axis=-1) - lam * jax.nn.softmax(s2, axis=-1)
        return jnp.einsum("bhqk,bkhd->bqhd", p.astype(v.dtype), v)

    return _blockwise(blk, (q,), q_pos, Q_BLOCK)


def _dsa_attention(q, qi, wi, kv, kidx, q_pos, slopes):
    L = kv.shape[1]
    topk = min(INDEX_TOPK_MAX, L // 4)
    k_pos = jnp.arange(L, dtype=jnp.int32)
    scale = DH_B ** -0.5

    def blk(qb_, qib, wib, qp):
        rel = jax.nn.relu(jnp.einsum("bqhd,bkd->bqhk", qib, kidx).astype(jnp.float32))
        score = jnp.einsum("bqhk,bqh->bqk", rel, wib.astype(jnp.float32))
        score = jnp.where(k_pos[None, None, :] <= qp[None, :, None], score, -jnp.inf)
        _, idx = lax.top_k(score, topk)
        sel = jax.vmap(lambda kvb, ib: kvb[ib])(kv, idx)
        ks = sel[..., :DH_B]
        vs = sel[..., DH_B:]
        dist = (qp[None, :, None] - idx).astype(jnp.float32)
        s = jnp.einsum("bqhd,bqkhd->bhqk", qb_, ks).astype(jnp.float32) * scale - slopes[None, :, None, None] * dist[:, None]
        s = jnp.where(dist[:, None] >= 0, s, -jnp.inf)
        p = jax.nn.softmax(s, axis=-1)
        return jnp.einsum("bhqk,bqkhd->bqhd", p.astype(vs.dtype), vs)

    return _blockwise(blk, (q, qi, wi), q_pos, DSA_Q_BLOCK)


def _fox_attention(q, kv, F, offset, q_pos):
    L = kv.shape[1]
    k_pos = jnp.arange(L, dtype=jnp.int32)
    k = kv[..., :DH_C]
    v = kv[..., DH_C:]
    Fq = F[:, offset:]
    Fk = jnp.transpose(F, (0, 2, 1))[:, :, None, :]
    scale = DH_C ** -0.5

    def blk(qb_, fqb, qp):
        s = (jnp.einsum("bqhd,bkhd->bhqk", qb_, k).astype(jnp.float32) * scale
             + jnp.transpose(fqb, (0, 2, 1))[..., None] - Fk)
        s = jnp.where(k_pos[None, None, None, :] <= qp[None, None, :, None], s, -jnp.inf)
        p = jax.nn.softmax(s, axis=-1)
        return jnp.einsum("bhqk,bkhd->bqhd", p.astype(v.dtype), v)

    return _blockwise(blk, (q, Fq), q_pos, Q_BLOCK)


def _short_conv(u, state, w):
    bsz, T, W = u.shape
    prev = jnp.zeros((bsz, CONV_K - 1, W), u.dtype) if state is None else state.astype(u.dtype)
    full = jnp.concatenate([prev, u], axis=1)
    y = full[:, 0:T] * w[0]
    for j in range(1, CONV_K):
        y = y + full[:, j:j + T] * w[j]
    return y, full[:, T:]


def _layer(x, p, past, lam_init, slopes_a, slopes_b):
    f32 = jnp.float32
    bsz, T, _ = x.shape
    offset = 0 if past is None else past["a_kv"].shape[1]
    q_pos = offset + jnp.arange(T, dtype=jnp.int32)
    z = jnp.einsum("btd,de->bte", x, p["w_in"])
    (a_q, a_k, a_v, b_q, b_k, b_v, b_iq, b_ik, b_iw, c_q, c_k, c_v, c_f,
     d_b, d_c, d_h, gates) = jnp.split(z, _in_splits(), axis=-1)

    def with_past(name, new):
        return new if past is None else jnp.concatenate([past[name].astype(new.dtype), new], axis=1)

    new_a = jnp.concatenate([a_k.reshape(bsz, T, H_A, 2 * DK_A), a_v.reshape(bsz, T, H_A, 2 * DK_A)], axis=-1)
    lam = (jnp.exp(jnp.sum(p["lam_q1"].astype(f32) * p["lam_k1"].astype(f32)))
           - jnp.exp(jnp.sum(p["lam_q2"].astype(f32) * p["lam_k2"].astype(f32))) + lam_init)
    o_a = _diff_attention(a_q.reshape(bsz, T, H_A, 2, DK_A), with_past("a_kv", new_a), q_pos, slopes_a, lam).astype(f32)
    o_a = o_a * lax.rsqrt(jnp.mean(jnp.square(o_a), -1, keepdims=True) + NORM_EPS) * p["g_diffnorm"].astype(f32) * (1.0 - lam_init)
    o_a = o_a.astype(x.dtype).reshape(bsz, T, A_W)

    new_bkv = jnp.concatenate([b_k.reshape(bsz, T, H_B, DH_B), b_v.reshape(bsz, T, H_B, DH_B)], axis=-1)
    wi = b_iw * (H_IDX ** -0.5 * D_IDX ** -0.5)
    o_b = _dsa_attention(b_q.reshape(bsz, T, H_B, DH_B), b_iq.reshape(bsz, T, H_IDX, D_IDX), wi,
                         with_past("b_kv", new_bkv), with_past("b_kidx", b_ik), q_pos, slopes_b).reshape(bsz, T, B_W)

    new_ckv = jnp.concatenate([c_k.reshape(bsz, T, H_C, DH_C), c_v.reshape(bsz, T, H_C, DH_C)], axis=-1)
    new_logf = jax.nn.log_sigmoid((c_f + p["b_fgate"]).astype(f32))
    F = jnp.cumsum(with_past("c_logf", new_logf), axis=1)
    o_c = _fox_attention(c_q.reshape(bsz, T, H_C, DH_C), with_past("c_kv", new_ckv), F, offset, q_pos).reshape(bsz, T, C_W)

    y_d, new_conv = _short_conv(d_c * d_h, None if past is None else past["conv"], p["conv_w"])
    o_d = d_b * y_d

    g = jax.nn.sigmoid(gates.astype(f32)).astype(x.dtype).reshape(bsz, T, N_BRANCH, D_MODEL)
    mixed = (g[:, :, 0] * (o_a @ p["w_br_a"]) + g[:, :, 1] * (o_b @ p["w_br_b"])
             + g[:, :, 2] * (o_c @ p["w_br_c"]) + g[:, :, 3] * (o_d @ p["w_br_d"]))
    x = _layernorm(ALPHA * x + mixed @ p["w_o"], p["ln1_g"], p["ln1_b"])

    hg, hu = jnp.split(x @ p["w_ffn_in"], 2, axis=-1)
    x = _layernorm(ALPHA * x + (jax.nn.silu(hg) * hu) @ p["w_ffn_out"], p["ln2_g"], p["ln2_b"])
    return x, (new_a, new_bkv, b_ik, new_ckv, new_logf.astype(x.dtype), new_conv)


def setup_inputs(seed: int = 0) -> dict:
    key = jax.random.key(seed)
    ks = jax.random.split(key, 32)
    f32 = jnp.float32
    n_pages = PAST_LEN // PAGE_SIZE
    n_pool = (5 * DEC_BATCH * n_pages + 3) // 4

    def nrm(k, shape, scale):
        return jax.random.normal(k, shape, f32) * scale

    page_table = jax.random.permutation(ks[8], n_pool)[:DEC_BATCH * n_pages].reshape(DEC_BATCH, n_pages).astype(jnp.int32)
    return {
        "x_prompt": nrm(ks[0], (BATCH, SEQ, D_MODEL), 1.0),
        "x_sample": nrm(ks[1], (DEC_BATCH, DEC_SEQ, D_MODEL), 1.0),
        "cache_a_kv": nrm(ks[2], (DEPTH, n_pool, PAGE_SIZE, H_A, 4 * DK_A), 1.0),
        "cache_b_kv": nrm(ks[3], (DEPTH, n_pool, PAGE_SIZE, H_B, 2 * DH_B), 1.0),
        "cache_b_kidx": nrm(ks[4], (DEPTH, n_pool, PAGE_SIZE, D_IDX), 1.0),
        "cache_c_kv": nrm(ks[5], (DEPTH, n_pool, PAGE_SIZE, H_C, 2 * DH_C), 1.0),
        "cache_c_logf": jax.nn.log_sigmoid(FGATE_BIAS_MEAN + nrm(ks[6], (DEPTH, n_pool, PAGE_SIZE, H_C), 1.0)),
        "state_conv": nrm(ks[7], (DEPTH, DEC_BATCH, CONV_K - 1, CONV_W), 1.0),
        "page_table": page_table,
        "w_in": nrm(ks[9], (DEPTH, D_MODEL, IN_WIDTH), D_MODEL ** -0.5),
        "b_fgate": FGATE_BIAS_MEAN + nrm(ks[10], (DEPTH, H_C), 0.1),
        "lam_q1": nrm(ks[11], (DEPTH, DK_A), 0.1),
        "lam_k1": nrm(ks[12], (DEPTH, DK_A), 0.1),
        "lam_q2": nrm(ks[13], (DEPTH, DK_A), 0.1),
        "lam_k2": nrm(ks[14], (DEPTH, DK_A), 0.1),
        "g_diffnorm": 1.0 + nrm(ks[15], (DEPTH, H_A, 2 * DK_A), 0.02),
        "conv_w": nrm(ks[16], (DEPTH, CONV_K, CONV_W), CONV_K ** -0.5),
        "w_br_a": nrm(ks[17], (DEPTH, A_W, D_MODEL), A_W ** -0.5),
        "w_br_b": nrm(ks[18], (DEPTH, B_W, D_MODEL), B_W ** -0.5),
        "w_br_c": nrm(ks[19], (DEPTH, C_W, D_MODEL), C_W ** -0.5),
        "w_br_d": nrm(ks[20], (DEPTH, CONV_W, D_MODEL), CONV_W ** -0.5),
        "w_o": nrm(ks[21], (DEPTH, D_MODEL, D_MODEL), D_MODEL ** -0.5 * BETA),
        "ln1_g": 1.0 + nrm(ks[22], (DEPTH, D_MODEL), 0.02),
        "ln1_b": nrm(ks[23], (DEPTH, D_MODEL), 0.02),
        "w_ffn_in": nrm(ks[24], (DEPTH, D_MODEL, 2 * D_FF), D_MODEL ** -0.5),
        "w_ffn_out": nrm(ks[25], (DEPTH, D_FF, D_MODEL), D_FF ** -0.5 * BETA),
        "ln2_g": 1.0 + nrm(ks[26], (DEPTH, D_MODEL), 0.02),
        "ln2_b": nrm(ks[27], (DEPTH, D_MODEL), 0.02),
    }


def reference(x_prompt, x_sample, cache_a_kv, cache_b_kv, cache_b_kidx, cache_c_kv, cache_c_logf,
              state_conv, page_table, w_in, b_fgate, lam_q1, lam_k1, lam_q2, lam_k2, g_diffnorm,
              conv_w, w_br_a, w_br_b, w_br_c, w_br_d, w_o, ln1_g, ln1_b, w_ffn_in, w_ffn_out,
              ln2_g, ln2_b):
    slopes_a, slopes_b = _alibi_slopes()

    def gather(cache, l):
        pages = cache[l, page_table]
        return pages.reshape((pages.shape[0], pages.shape[1] * pages.shape[2]) + pages.shape[3:])

    yp, ys = x_prompt, x_sample
    news_p, news_s = [], []
    for l in range(DEPTH):
        p = {"w_in": w_in[l], "b_fgate": b_fgate[l], "lam_q1": lam_q1[l], "lam_k1": lam_k1[l],
             "lam_q2": lam_q2[l], "lam_k2": lam_k2[l], "g_diffnorm": g_diffnorm[l], "conv_w": conv_w[l],
             "w_br_a": w_br_a[l], "w_br_b": w_br_b[l], "w_br_c": w_br_c[l], "w_br_d": w_br_d[l],
             "w_o": w_o[l], "ln1_g": ln1_g[l], "ln1_b": ln1_b[l], "w_ffn_in": w_ffn_in[l],
             "w_ffn_out": w_ffn_out[l], "ln2_g": ln2_g[l], "ln2_b": ln2_b[l]}
        lam_init = 0.8 - 0.6 * math.exp(-0.3 * l)
        yp, new_p = _layer(yp, p, None, lam_init, slopes_a, slopes_b)
        past = {"a_kv": gather(cache_a_kv, l), "b_kv": gather(cache_b_kv, l), "b_kidx": gather(cache_b_kidx, l),
                "c_kv": gather(cache_c_kv, l), "c_logf": gather(cache_c_logf, l), "conv": state_conv[l]}
        ys, new_s = _layer(ys, p, past, lam_init, slopes_a, slopes_b)
        news_p.append(new_p)
        news_s.append(new_s)

    def stack(lst, i):
        return jnp.stack([e[i] for e in lst])

    new_a_kv_prompt, new_a_kv_sample = stack(news_p, 0), stack(news_s, 0)
    new_b_kv_prompt, new_b_kv_sample = stack(news_p, 1), stack(news_s, 1)
    new_b_kidx_prompt, new_b_kidx_sample = stack(news_p, 2), stack(news_s, 2)
    new_c_kv_prompt, new_c_kv_sample = stack(news_p, 3), stack(news_s, 3)
    new_c_logf_prompt, new_c_logf_sample = stack(news_p, 4), stack(news_s, 4)
    new_conv_prompt, new_conv_sample = stack(news_p, 5), stack(news_s, 5)
    return (yp, ys, new_a_kv_prompt, new_a_kv_sample, new_b_kv_prompt, new_b_kv_sample,
            new_b_kidx_prompt, new_b_kidx_sample, new_c_kv_prompt, new_c_kv_sample,
            new_c_logf_prompt, new_c_logf_sample, new_conv_prompt, new_conv_sample)
```

```python
import functools
import math

import numpy as np
import jax
import jax.numpy as jnp
from jax import lax
from jax.experimental import pallas as pl
from jax.experimental.pallas import tpu as pltpu

F32 = jnp.float32
BF16 = jnp.bfloat16
NEG_INF = float("-inf")
M_INIT = -1e30
INT_MIN = -2 ** 31

D_MODEL = 1024
DEPTH = 2
PAGE_SIZE = 128
H_A, DK_A = 4, 64
A_W = H_A * 2 * DK_A
H_B, DH_B = 4, 64
B_W = H_B * DH_B
H_IDX, D_IDX = 8, 32
INDEX_TOPK_MAX = 256
H_C, DH_C = 4, 64
C_W = H_C * DH_C
CONV_W, CONV_K = 256, 3
N_BRANCH = 4
D_FF = -(-8 * D_MODEL // (3 * 256)) * 256
ALPHA = (2 * DEPTH) ** 0.25
LN_EPS = 1e-5
NORM_EPS = 1e-6

_n = H_A + H_B
_S_ALL = 2.0 ** (-8.0 * (np.arange(_n) + 1) / _n)
SLOPES_A = [float(v) for v in _S_ALL[0::2]]
SLOPES_B = [float(v) for v in _S_ALL[1::2]]

LANES = 128
SUBLANES = 8
VMEM_LIMIT_BYTES = 56 * 1024 * 1024

_WIDTHS = (A_W, A_W, A_W, B_W, B_W, B_W, H_IDX * D_IDX, D_IDX, H_IDX,
           C_W, C_W, C_W, H_C, CONV_W, CONV_W, CONV_W, N_BRANCH * D_MODEL)
_OFF = np.concatenate([[0], np.cumsum(_WIDTHS)]).astype(np.int64)
(O_AQ, O_AK, O_AV, O_BQ, O_BK, O_BV, O_BIQ, O_BIK, O_BIW, O_CQ, O_CK, O_CV, O_CF,
 O_DB, O_DC, O_DH, O_G, IN_WIDTH) = [int(v) for v in _OFF]

SEG = {}
_pos = 0
for _name, _w in (("qa", 1024), ("na", 1024), ("qb", 512), ("nb", 512), ("iq", 256), ("krep", 256),
                  ("qc", 512), ("nc", 512), ("dconv", 768), ("kidx", 128), ("misc", 128)):
    SEG[_name] = (_pos, _pos + _w)
    _pos += _w
W1_WIDTH = _pos
MISC_CF = 0
MISC_WI = 8


def _build_w1_columns():
    src = -np.ones((W1_WIDTH,), np.int64)
    scale = np.ones((W1_WIDTH,), np.float32)
    s = SEG["qa"][0]
    for h in range(H_A):
        src[s + h * 256: s + h * 256 + 128] = O_AQ + h * 128 + np.arange(128)
    scale[SEG["qa"][0]:SEG["qa"][1]] = DK_A ** -0.5
    s = SEG["na"][0]
    for h in range(H_A):
        src[s + h * 256: s + h * 256 + 128] = O_AK + h * 128 + np.arange(128)
        src[s + h * 256 + 128: s + (h + 1) * 256] = O_AV + h * 128 + np.arange(128)
    for nm, oq, ok, ov in (("b", O_BQ, O_BK, O_BV), ("c", O_CQ, O_CK, O_CV)):
        s = SEG["q" + nm][0]
        for h in range(4):
            src[s + h * 128: s + h * 128 + 64] = oq + h * 64 + np.arange(64)
        scale[SEG["q" + nm][0]:SEG["q" + nm][1]] = 64 ** -0.5
        s = SEG["n" + nm][0]
        for h in range(4):
            src[s + h * 128: s + h * 128 + 64] = ok + h * 64 + np.arange(64)
            src[s + h * 128 + 64: s + (h + 1) * 128] = ov + h * 64 + np.arange(64)
    s = SEG["iq"][0]
    src[s:s + 256] = O_BIQ + np.arange(256)
    s = SEG["krep"][0]
    for h in range(H_IDX):
        src[s + h * 32: s + (h + 1) * 32] = O_BIK + np.arange(32)
    s = SEG["dconv"][0]
    src[s:s + 768] = O_DB + np.arange(768)
    s = SEG["kidx"][0]
    src[s:s + 32] = O_BIK + np.arange(32)
    s = SEG["misc"][0]
    src[s + MISC_CF: s + MISC_CF + H_C] = O_CF + np.arange(H_C)
    src[s + MISC_WI: s + MISC_WI + H_IDX] = O_BIW + np.arange(H_IDX)
    scale[s + MISC_WI: s + MISC_WI + H_IDX] = H_IDX ** -0.5 * D_IDX ** -0.5
    return src, scale


_W1_SRC, _W1_SCALE = _build_w1_columns()

QA_AUG = 128
QBC_AUG = 64


def _build_q_aug_row():
    row = np.zeros((1, W1_WIDTH), np.float32)
    for h in range(4):
        a = SEG["qa"][0] + h * 256 + QA_AUG
        row[0, a], row[0, a + 1] = SLOPES_A[h], SLOPES_A[h] * 256.0
        b = SEG["qb"][0] + h * 128 + QBC_AUG
        row[0, b], row[0, b + 1] = SLOPES_B[h], SLOPES_B[h] * 256.0
        c = SEG["qc"][0] + h * 128 + QBC_AUG
        row[0, c:c + 3] = 1.0
    return row


def _cparams(*sem):
    return pltpu.CompilerParams(dimension_semantics=sem, vmem_limit_bytes=VMEM_LIMIT_BYTES)


def _const_spec(shape):
    nd = len(shape)
    return pl.BlockSpec(shape, lambda *_: (0,) * nd, pipeline_mode=pl.Buffered(1))


def _dot(a, b):
    return jnp.dot(a, b, preferred_element_type=F32)


def _dot_nt(a, b):
    return lax.dot_general(a, b, (((1,), (1,)), ((), ())), preferred_element_type=F32)


def _row_tile(n, pref):
    t = min(n, pref)
    while n % t:
        t //= 2
    return t


def _split3(x):
    hi = x.astype(BF16)
    r = x - hi.astype(F32)
    mid = r.astype(BF16)
    lo = (r - mid.astype(F32)).astype(BF16)
    return hi, mid, lo


def _log_sigmoid(x):
    return jnp.minimum(x, 0.0) - jnp.log1p(jnp.exp(-jnp.abs(x)))


def _layernorm(h, g, b):
    mu = jnp.mean(h, axis=-1, keepdims=True)
    d = h - mu
    var = jnp.mean(d * d, axis=-1, keepdims=True)
    return d * lax.rsqrt(var + LN_EPS) * g + b


def _proj_kernel(x_ref, w_ref, aug_ref, bfg_ref, qa_ref, na_ref, qb_ref, nb_ref, iq_ref, kr_ref, qc_ref,
                 nc_ref, dc_ref, misc_ref, kidx_ref, logf_ref):
    xb = x_ref[...].astype(BF16)

    def seg(name):
        lo, hi = SEG[name]
        return _dot(xb, w_ref[:, lo:hi])

    def qseg(name):
        lo, hi = SEG[name]
        return (seg(name) + aug_ref[:, lo:hi]).astype(BF16)

    qa_ref[...] = qseg("qa")
    na_ref[...] = seg("na")
    qb_ref[...] = qseg("qb")
    nb_ref[...] = seg("nb")
    iq_ref[...] = seg("iq").astype(BF16)
    kr_ref[...] = seg("krep").astype(BF16)
    qc_ref[...] = qseg("qc")
    nc_ref[...] = seg("nc")
    dc_ref[...] = seg("dconv")
    kidx_ref[...] = seg("kidx")[:, :D_IDX]
    misc = seg("misc")
    misc_ref[...] = misc
    logf_ref[...] = _log_sigmoid(misc + bfg_ref[...])[:, MISC_CF:MISC_CF + H_C]


def _proj(x2d, w1, aug_row, bfg_row, tm):
    n = x2d.shape[0]
    widths = dict((k, v[1] - v[0]) for k, v in SEG.items())
    outs = [("qa", BF16, widths["qa"]), ("na", F32, widths["na"]), ("qb", BF16, widths["qb"]),
            ("nb", F32, widths["nb"]), ("iq", BF16, widths["iq"]), ("krep", BF16, widths["krep"]),
            ("qc", BF16, widths["qc"]), ("nc", F32, widths["nc"]), ("dconv", F32, widths["dconv"]),
            ("misc", F32, widths["misc"]), ("kidx", F32, D_IDX), ("logf", F32, H_C)]
    res = pl.pallas_call(
        _proj_kernel,
        grid=(n // tm,),
        in_specs=[pl.BlockSpec((tm, D_MODEL), lambda i: (i, 0)),
                  _const_spec((D_MODEL, W1_WIDTH)),
                  _const_spec((1, W1_WIDTH)),
                  _const_spec((1, LANES))],
        out_specs=[pl.BlockSpec((tm, w), lambda i: (i, 0)) for _, _, w in outs],
        out_shape=[jax.ShapeDtypeStruct((n, w), dt) for _, dt, w in outs],
        compiler_params=_cparams("parallel"),
        name="proj",
    )(x2d, w1, aug_row, bfg_row)
    return dict(zip([o[0] for o in outs], res))


def _topk_select(score, k):
    rows, width = score.shape
    bits = lax.bitcast_convert_type(score, jnp.int32)
    key = jnp.where(bits < 0, bits ^ jnp.int32(0x7FFFFFFF), bits)
    kf = jnp.float32(k)

    def count(mask):
        return jnp.sum(jnp.where(mask, 1.0, 0.0), axis=1, keepdims=True)

    t0 = jnp.where(count(key >= 0) >= kf, jnp.int32(0), jnp.int32(INT_MIN))

    def vbody(i, t):
        cand = t + jnp.left_shift(jnp.int32(1), jnp.int32(30) - i)
        return jnp.where(count(key >= cand) >= kf, cand, t)

    t = lax.fori_loop(0, 31, vbody, t0)
    gt = key > t
    eq = key == t
    need = kf - count(gt)
    idx = lax.broadcasted_iota(jnp.int32, (rows, width), 1)
    nbits = max(1, int(math.ceil(math.log2(width))))

    def ibody(i, j):
        cand = j + jnp.left_shift(jnp.int32(1), jnp.int32(nbits - 1) - i)
        return jnp.where(count(eq & (idx < cand)) < need, cand, j)

    j = lax.fori_loop(0, nbits, ibody, jnp.zeros((rows, 1), jnp.int32))
    return gt | (eq & (idx <= j))


def _attn_a_kernel(qa_ref, na_ref, lam_ref, g_ref, o_ref, k1_s, k2_s, v_s, *, tq, seq, lam_init):
    qi = pl.program_id(1)

    @pl.when(qi == 0)
    def _():
        lane = lax.broadcasted_iota(jnp.int32, (seq, LANES), 1)
        kp = lax.broadcasted_iota(jnp.int32, (seq, LANES), 0)
        kaug = jnp.where(lane == 0, (kp & 255).astype(F32),
                         jnp.where(lane == 1, (kp >> 8).astype(F32), 0.0)).astype(BF16)
        for h in range(H_A):
            kk = na_ref[0, :, h * 256:h * 256 + 128]
            k1_s[h, :, 0:LANES] = jnp.where(lane < 64, kk, 0.0).astype(BF16)
            k1_s[h, :, LANES:2 * LANES] = kaug
            k2_s[h, :, 0:LANES] = jnp.where(lane >= 64, kk, 0.0).astype(BF16)
            k2_s[h, :, LANES:2 * LANES] = kaug
            v_s[h] = na_ref[0, :, h * 256 + 128:(h + 1) * 256].astype(BF16)

    lam_p = lam_ref[...]
    lam = (jnp.exp(jnp.sum(lam_p[0:1] * lam_p[1:2], axis=1, keepdims=True))
           - jnp.exp(jnp.sum(lam_p[2:3] * lam_p[3:4], axis=1, keepdims=True)) + lam_init)
    rows = qi * tq + lax.broadcasted_iota(jnp.int32, (tq, seq), 0)
    cols = lax.broadcasted_iota(jnp.int32, (tq, seq), 1)
    causal = cols <= rows
    for h in range(H_A):
        qh = qa_ref[0, :, h * 256:(h + 1) * 256]
        s1 = jnp.where(causal, _dot_nt(qh, k1_s[h]), NEG_INF)
        s2 = jnp.where(causal, _dot_nt(qh, k2_s[h]), NEG_INF)
        e1 = jnp.exp(s1 - jnp.max(s1, axis=1, keepdims=True))
        e2 = jnp.exp(s2 - jnp.max(s2, axis=1, keepdims=True))
        r1 = 1.0 / jnp.sum(e1, axis=1, keepdims=True)
        r2 = lam / jnp.sum(e2, axis=1, keepdims=True)
        p = (e1 * r1 - e2 * r2).astype(BF16)
        o = _dot(p, v_s[h])
        o = o * lax.rsqrt(jnp.mean(o * o, axis=1, keepdims=True) + NORM_EPS)
        o = o * g_ref[h:h + 1, :] * (1.0 - lam_init)
        o_ref[0, :, h * 128:(h + 1) * 128] = o.astype(BF16)


def _attn_a(qa, na, lam_p, g, lam_init, tq):
    bsz, seq, _ = qa.shape
    return pl.pallas_call(
        functools.partial(_attn_a_kernel, tq=tq, seq=seq, lam_init=lam_init),
        grid=(bsz, seq // tq),
        in_specs=[pl.BlockSpec((1, tq, 1024), lambda b, i: (b, i, 0)),
                  pl.BlockSpec((1, seq, 1024), lambda b, i: (b, 0, 0)),
                  _const_spec((4, DK_A)),
                  _const_spec((H_A, 2 * DK_A))],
        out_specs=pl.BlockSpec((1, tq, A_W), lambda b, i: (b, i, 0)),
        out_shape=jax.ShapeDtypeStruct((bsz, seq, A_W), BF16),
        scratch_shapes=[pltpu.VMEM((H_A, seq, 2 * LANES), BF16), pltpu.VMEM((H_A, seq, 2 * LANES), BF16),
                        pltpu.VMEM((H_A, seq, LANES), BF16)],
        compiler_params=_cparams("parallel", "arbitrary"),
        name="attn_a",
    )(qa, na, lam_p, g)


def _attn_c_kernel(qc_ref, nc_ref, misc_ref, bfg_ref, o_ref, k_s, v_s, *, tq, seq):
    qi = pl.program_id(1)

    @pl.when(qi == 0)
    def _():
        lane = lax.broadcasted_iota(jnp.int32, (seq, LANES), 1)
        ch = min(256, seq)
        tri = jnp.where(lax.broadcasted_iota(jnp.int32, (ch, ch), 0)
                        >= lax.broadcasted_iota(jnp.int32, (ch, ch), 1), 1.0, 0.0).astype(BF16)
        carry = jnp.zeros((1, LANES), F32)
        chunks = []
        for c in range(seq // ch):
            lf = _log_sigmoid(misc_ref[0, c * ch:(c + 1) * ch, :] + bfg_ref[...])
            hi, mid, lo = _split3(lf)
            fc = (_dot(tri, hi) + _dot(tri, mid)) + _dot(tri, lo) + carry
            carry = fc[ch - 1:ch, :]
            chunks.append(fc)
        fcum = jnp.concatenate(chunks, axis=0) if len(chunks) > 1 else chunks[0]
        for h in range(H_C):
            fh = jnp.broadcast_to(fcum[:, MISC_CF + h:MISC_CF + h + 1], (seq, LANES))
            hi, mid, lo = (v.astype(F32) for v in _split3(-fh))
            kv = nc_ref[0, :, h * 128:(h + 1) * 128]
            aug = jnp.where(lane == QBC_AUG, hi,
                            jnp.where(lane == QBC_AUG + 1, mid, jnp.where(lane == QBC_AUG + 2, lo, 0.0)))
            k_s[h] = jnp.where(lane < 64, kv, aug).astype(BF16)
            v_s[h] = kv.astype(BF16)

    rows = qi * tq + lax.broadcasted_iota(jnp.int32, (tq, seq), 0)
    cols = lax.broadcasted_iota(jnp.int32, (tq, seq), 1)
    causal = cols <= rows
    for h in range(H_C):
        qh = qc_ref[0, :, h * 128:(h + 1) * 128]
        s = jnp.where(causal, _dot_nt(qh, k_s[h]), NEG_INF)
        e = jnp.exp(s - jnp.max(s, axis=1, keepdims=True))
        r = 1.0 / jnp.sum(e, axis=1, keepdims=True)
        o = _dot((e * r).astype(BF16), v_s[h])
        o_ref[0, :, h * 64:(h + 1) * 64] = o[:, 64:128].astype(BF16)


def _attn_c(qc, nc, misc, bfg_row, tq):
    bsz, seq, _ = qc.shape
    return pl.pallas_call(
        functools.partial(_attn_c_kernel, tq=tq, seq=seq),
        grid=(bsz, seq // tq),
        in_specs=[pl.BlockSpec((1, tq, 512), lambda b, i: (b, i, 0)),
                  pl.BlockSpec((1, seq, 512), lambda b, i: (b, 0, 0)),
                  pl.BlockSpec((1, seq, LANES), lambda b, i: (b, 0, 0)),
                  _const_spec((1, LANES))],
        out_specs=pl.BlockSpec((1, tq, C_W), lambda b, i: (b, i, 0)),
        out_shape=jax.ShapeDtypeStruct((bsz, seq, C_W), BF16),
        scratch_shapes=[pltpu.VMEM((H_C, seq, LANES), BF16)] * 2,
        compiler_params=_cparams("parallel", "arbitrary"),
        name="attn_c",
    )(qc, nc, misc, bfg_row)


def _attn_b_kernel(qb_ref, nb_ref, iq_ref, kr_ref, misc_ref, o_ref, k_s, v_s, kr_s, *, tq, seq, topk):
    qi = pl.program_id(1)

    @pl.when(qi == 0)
    def _():
        lane = lax.broadcasted_iota(jnp.int32, (seq, LANES), 1)
        kp = lax.broadcasted_iota(jnp.int32, (seq, LANES), 0)
        aug = jnp.where(lane == QBC_AUG, (kp & 255).astype(F32),
                        jnp.where(lane == QBC_AUG + 1, (kp >> 8).astype(F32), 0.0))
        for h in range(H_B):
            kv = nb_ref[0, :, h * 128:(h + 1) * 128]
            k_s[h] = jnp.where(lane < 64, kv, aug).astype(BF16)
            v_s[h] = kv.astype(BF16)
        kr = kr_ref[0]
        klane = lax.broadcasted_iota(jnp.int32, (seq, 256), 1)
        for h in range(H_IDX):
            kr_s[h] = kr * jnp.where((klane >> 5) == h, 1.0, 0.0).astype(BF16)

    rows = qi * tq + lax.broadcasted_iota(jnp.int32, (tq, seq), 0)
    cols = lax.broadcasted_iota(jnp.int32, (tq, seq), 1)
    causal = cols <= rows
    iq = iq_ref[0]
    score = jnp.zeros((tq, seq), F32)
    for h in range(H_IDX):
        rel = jnp.maximum(_dot_nt(iq, kr_s[h]), 0.0)
        score = score + rel * misc_ref[0, :, MISC_WI + h:MISC_WI + h + 1]
    score = jnp.where(causal, score, NEG_INF)
    sel = _topk_select(score, topk)
    bias = jnp.where(sel & causal, 0.0, NEG_INF)
    for h in range(H_B):
        qh = qb_ref[0, :, h * 128:(h + 1) * 128]
        s = _dot_nt(qh, k_s[h]) + bias
        e = jnp.exp(s - jnp.max(s, axis=1, keepdims=True))
        r = 1.0 / jnp.sum(e, axis=1, keepdims=True)
        o = _dot((e * r).astype(BF16), v_s[h])
        o_ref[0, :, h * 64:(h + 1) * 64] = o[:, 64:128].astype(BF16)


def _attn_b(qb, nb, iq, krep, misc, tq):
    bsz, seq, _ = qb.shape
    topk = min(INDEX_TOPK_MAX, seq // 4)
    return pl.pallas_call(
        functools.partial(_attn_b_kernel, tq=tq, seq=seq, topk=topk),
        grid=(bsz, seq // tq),
        in_specs=[pl.BlockSpec((1, tq, 512), lambda b, i: (b, i, 0)),
                  pl.BlockSpec((1, seq, 512), lambda b, i: (b, 0, 0)),
                  pl.BlockSpec((1, tq, 256), lambda b, i: (b, i, 0)),
                  pl.BlockSpec((1, seq, 256), lambda b, i: (b, 0, 0)),
                  pl.BlockSpec((1, tq, LANES), lambda b, i: (b, i, 0))],
        out_specs=pl.BlockSpec((1, tq, B_W), lambda b, i: (b, i, 0)),
        out_shape=jax.ShapeDtypeStruct((bsz, seq, B_W), BF16),
        scratch_shapes=[pltpu.VMEM((H_B, seq, LANES), BF16), pltpu.VMEM((H_B, seq, LANES), BF16),
                        pltpu.VMEM((H_IDX, seq, H_IDX * D_IDX), BF16)],
        compiler_params=_cparams("parallel", "arbitrary"),
        name="attn_b",
    )(qb, nb, iq, krep, misc)


def _merge_kernel(*refs, tm, sample, tiles_per_seq):
    if sample:
        (x_ref, oa_ref, ob_ref, oc_ref, dc_ref, st_ref, wg_ref, pa_ref, pb_ref, pc_ref, pd_ref,
         wo_ref, g_ref, b_ref, cw_ref, y_ref, conv_ref) = refs
    else:
        (x_ref, oa_ref, ob_ref, oc_ref, dc_ref, halo_ref, wg_ref, pa_ref, pb_ref, pc_ref, pd_ref,
         wo_ref, g_ref, b_ref, cw_ref, y_ref, conv_ref, uu_ref) = refs
    w0 = cw_ref[0:1, :]
    w1 = cw_ref[1:2, :]
    w2 = cw_ref[2:3, :]
    u = dc_ref[:, 256:512] * dc_ref[:, 512:768]
    if sample:
        s0 = st_ref[:, 0:256]
        s1 = st_ref[:, 256:512]
        y = s0 * w0 + s1 * w1 + u * w2
        conv_ref[:, 0:256] = s1
        conv_ref[:, 256:512] = u
    else:
        first = (pl.program_id(0) % tiles_per_seq) == 0
        uh = halo_ref[:, 256:512] * halo_ref[:, 512:768]
        uu_ref[0:SUBLANES, :] = jnp.where(first, 0.0, uh)
        uu_ref[SUBLANES:SUBLANES + tm, :] = u
        y = uu_ref[SUBLANES - 2:SUBLANES - 2 + tm, :] * w0 + uu_ref[SUBLANES - 1:SUBLANES - 1 + tm, :] * w1 + u * w2
        conv_ref[0] = uu_ref[SUBLANES + tm - 2:SUBLANES + tm, :]
    o_d = (dc_ref[:, 0:256] * y).astype(BF16)
    xv = x_ref[...]
    xb = xv.astype(BF16)
    mixed = None
    for i, (o, p_ref) in enumerate(((oa_ref[...], pa_ref), (ob_ref[...], pb_ref),
                                    (oc_ref[...], pc_ref), (o_d, pd_ref))):
        gate = jax.nn.sigmoid(_dot(xb, wg_ref[:, i * D_MODEL:(i + 1) * D_MODEL]))
        term = gate * _dot(o, p_ref[...])
        mixed = term if mixed is None else mixed + term
    h = ALPHA * xv + _dot(mixed.astype(BF16), wo_ref[...])
    y_ref[...] = _layernorm(h, g_ref[...], b_ref[...])


def _merge(x2d, oa, ob, oc, dconv, state2d, wts, tm, seq):
    n = x2d.shape[0]
    sample = state2d is not None
    row = lambda w: pl.BlockSpec((tm, w), lambda i: (i, 0))
    in_specs = [row(D_MODEL), row(A_W), row(B_W), row(C_W), row(768)]
    args = [x2d, oa, ob, oc, dconv]
    if sample:
        in_specs.append(row(512))
        args.append(state2d)
        out_specs = [row(D_MODEL), row(512)]
        out_shape = [jax.ShapeDtypeStruct((n, D_MODEL), F32), jax.ShapeDtypeStruct((n, 512), F32)]
        scratch = []
        tiles_per_seq = 1
    else:
        tiles_per_seq = seq // tm
        r8 = tm // SUBLANES
        in_specs.append(pl.BlockSpec((SUBLANES, 768), lambda i: (jnp.maximum(i * r8 - 1, 0), 0)))
        args.append(dconv)
        out_specs = [row(D_MODEL), pl.BlockSpec((1, CONV_K - 1, CONV_W), lambda i: (i // tiles_per_seq, 0, 0))]
        out_shape = [jax.ShapeDtypeStruct((n, D_MODEL), F32),
                     jax.ShapeDtypeStruct((n // seq, CONV_K - 1, CONV_W), F32)]
        scratch = [pltpu.VMEM((tm + SUBLANES, CONV_W), F32)]
    for name in ("wg", "pa", "pb", "pc", "pd", "wo", "ln1_g", "ln1_b", "conv_w"):
        in_specs.append(_const_spec(wts[name].shape))
        args.append(wts[name])
    return pl.pallas_call(
        functools.partial(_merge_kernel, tm=tm, sample=sample, tiles_per_seq=tiles_per_seq),
        grid=(n // tm,),
        in_specs=in_specs, out_specs=out_specs, out_shape=out_shape, scratch_shapes=scratch,
        compiler_params=_cparams("arbitrary"),
        name="merge_s" if sample else "merge",
    )(*args)


def _ffn_kernel(x_ref, wi_ref, wo_ref, g_ref, b_ref, y_ref):
    xv = x_ref[...]
    xb = xv.astype(BF16)
    hg = _dot(xb, wi_ref[:, 0:D_FF])
    hu = _dot(xb, wi_ref[:, D_FF:2 * D_FF])
    act = (hg * jax.nn.sigmoid(hg) * hu).astype(BF16)
    h = ALPHA * xv + _dot(act, wo_ref[...])
    y_ref[...] = _layernorm(h, g_ref[...], b_ref[...])


def _ffn(x2d, wts, tm):
    n = x2d.shape[0]
    return pl.pallas_call(
        _ffn_kernel,
        grid=(n // tm,),
        in_specs=[pl.BlockSpec((tm, D_MODEL), lambda i: (i, 0)),
                  _const_spec((D_MODEL, 2 * D_FF)), _const_spec((D_FF, D_MODEL)),
                  _const_spec((1, D_MODEL)), _const_spec((1, D_MODEL))],
        out_specs=pl.BlockSpec((tm, D_MODEL), lambda i: (i, 0)),
        out_shape=jax.ShapeDtypeStruct((n, D_MODEL), F32),
        compiler_params=_cparams("parallel"),
        name="ffn",
    )(x2d, wts["wf_in"], wts["wf_out"], wts["ln2_g"], wts["ln2_b"])


def _build_prefix_matrix():
    u = np.zeros((PAGE_SIZE * H_C, SUBLANES * PAGE_SIZE), np.float32)
    j = np.arange(PAGE_SIZE)
    for h in range(H_C):
        for k in range(PAGE_SIZE):
            u[j[j <= k] * H_C + h, h * PAGE_SIZE + k] = 1.0
    return u


def _pfx_kernel(x_ref, u_ref, o_ref):
    hi, mid, lo = _split3(x_ref[...])
    u = u_ref[...]
    o_ref[...] = (_dot(hi, u) + _dot(mid, u)) + _dot(lo, u)


def _page_prefix(logf_pages, umat):
    n_pool = logf_pages.shape[0]
    tp = _row_tile(n_pool, 512) if n_pool % SUBLANES == 0 else n_pool
    out = pl.pallas_call(
        _pfx_kernel,
        grid=(n_pool // tp,),
        in_specs=[pl.BlockSpec((tp, PAGE_SIZE * H_C), lambda i: (i, 0)),
                  _const_spec(umat.shape)],
        out_specs=pl.BlockSpec((tp, SUBLANES * PAGE_SIZE), lambda i: (i, 0)),
        out_shape=jax.ShapeDtypeStruct((n_pool, SUBLANES * PAGE_SIZE), F32),
        compiler_params=_cparams("parallel"),
        name="page_prefix",
    )(logf_pages, umat)
    return out.reshape(n_pool, SUBLANES, PAGE_SIZE)


def _idx_kernel(pt_ref, *refs, n_pages):
    pages = refs[:n_pages]
    iq_ref, wi_ref, knew_ref, o_ref = refs[n_pages:]
    iq = iq_ref[0]
    wi = wi_ref[0]

    def page_score(kpage):
        rel = jnp.maximum(_dot_nt(iq, kpage), 0.0)
        return jnp.sum(rel * wi, axis=0, keepdims=True)

    for j in range(n_pages):
        o_ref[0, :, j * PAGE_SIZE:(j + 1) * PAGE_SIZE] = page_score(pages[j][0].astype(BF16))
    knew = jnp.broadcast_to(knew_ref[0], (PAGE_SIZE, D_IDX)).astype(BF16)
    lane = lax.broadcasted_iota(jnp.int32, (1, PAGE_SIZE), 1)
    o_ref[0, :, n_pages * PAGE_SIZE:(n_pages + 1) * PAGE_SIZE] = jnp.where(lane == 0, page_score(knew), NEG_INF)


def _idx_scores(page_table, kidx_cache, page_base, iq8, wi8, knew):
    nsamp, n_pages = page_table.shape
    width = (n_pages + 1) * PAGE_SIZE

    def page_spec(j):
        return pl.BlockSpec((1, PAGE_SIZE, D_IDX), lambda s, pt: (page_base + pt[s, j], 0, 0))

    grid_spec = pltpu.PrefetchScalarGridSpec(
        num_scalar_prefetch=1,
        grid=(nsamp,),
        in_specs=[page_spec(j) for j in range(n_pages)] + [
            pl.BlockSpec((1, H_IDX, D_IDX), lambda s, pt: (s, 0, 0)),
            pl.BlockSpec((1, H_IDX, 1), lambda s, pt: (s, 0, 0)),
            pl.BlockSpec((1, 1, D_IDX), lambda s, pt: (s, 0, 0))],
        out_specs=pl.BlockSpec((1, 1, width), lambda s, pt: (s, 0, 0)),
    )
    out = pl.pallas_call(
        functools.partial(_idx_kernel, n_pages=n_pages),
        grid_spec=grid_spec,
        out_shape=jax.ShapeDtypeStruct((nsamp, 1, width), F32),
        compiler_params=_cparams("arbitrary"),
        name="idx_scores",
    )(page_table, *([kidx_cache] * n_pages), iq8, wi8, knew)
    return out.reshape(nsamp, width)


def _sel_kernel(s_ref, o_ref, *, topk):
    sel = _topk_select(s_ref[...], topk)
    o_ref[...] = jnp.where(sel, 0.0, NEG_INF)


def _select_bias(scores, topk):
    return pl.pallas_call(
        functools.partial(_sel_kernel, topk=topk),
        out_shape=jax.ShapeDtypeStruct(scores.shape, F32),
        compiler_params=pltpu.CompilerParams(vmem_limit_bytes=VMEM_LIMIT_BYTES),
        name="select_bias",
    )(scores)


def _attn_s_kernel(pt_ref, *refs, group, n_pages, lam_init):
    g = group
    ca = refs[0:g]
    cb = refs[g:2 * g]
    cc = refs[2 * g:3 * g]
    pf = refs[3 * g:4 * g]
    (selp_ref, seln_ref, qa_ref, qb_ref, qc_ref, na_ref, nb_ref, nc_ref, lfn_ref, lam_ref, g_ref,
     oa_ref, ob_ref, oc_ref,
     ma_s, la_s, acca_s, mb_s, lb_s, accb_s, mc_s, lc_s, accc_s, carry_s) = refs[4 * g:]
    p = pl.program_id(1)
    n_steps = n_pages // g

    @pl.when(p == 0)
    def _():
        for m_s, l_s, acc_s in ((ma_s, la_s, acca_s), (mb_s, lb_s, accb_s), (mc_s, lc_s, accc_s)):
            m_s[...] = jnp.full(m_s.shape, M_INIT, F32)
            l_s[...] = jnp.zeros(l_s.shape, F32)
            acc_s[...] = jnp.zeros(acc_s.shape, F32)
        carry_s[...] = jnp.zeros(carry_s.shape, F32)

    sub_a = lax.broadcasted_iota(jnp.int32, (SUBLANES, 1024), 0)
    lane_a = lax.broadcasted_iota(jnp.int32, (SUBLANES, 1024), 1)
    qa_rows = jnp.broadcast_to(qa_ref[0].astype(F32), (SUBLANES, 1024))
    qa_blk = jnp.where(((lane_a >> 8) == (sub_a >> 1)) & (((lane_a >> 6) & 3) == (sub_a & 1)),
                       qa_rows, 0.0).astype(BF16)
    sub_b = lax.broadcasted_iota(jnp.int32, (SUBLANES, 512), 0)
    lane_b = lax.broadcasted_iota(jnp.int32, (SUBLANES, 512), 1)
    head_b = ((lane_b >> 7) == sub_b) & ((lane_b & 127) < 64)
    qb_blk = jnp.where(head_b, jnp.broadcast_to(qb_ref[0].astype(F32), (SUBLANES, 512)), 0.0).astype(BF16)
    qc_blk = jnp.where(head_b, jnp.broadcast_to(qc_ref[0].astype(F32), (SUBLANES, 512)), 0.0).astype(BF16)

    sub1 = lax.broadcasted_iota(jnp.int32, (SUBLANES, 1), 0)
    slope_a = jnp.zeros((SUBLANES, 1), F32)
    slope_b = jnp.zeros((SUBLANES, 1), F32)
    for h in range(4):
        slope_a = jnp.where((sub1 >> 1) == h, SLOPES_A[h], slope_a)
        slope_b = jnp.where(sub1 == h, SLOPES_B[h], slope_b)

    def update(qblk, pages, bias, m_s, l_s, acc_s):
        n = pages[0].shape[0]
        s = jnp.concatenate([_dot_nt(qblk, pg) for pg in pages], axis=1) + bias
        m_old = m_s[...]
        m_new = jnp.maximum(m_old, jnp.max(s, axis=1, keepdims=True))
        alpha = jnp.exp(m_old - m_new)
        e = jnp.exp(s - m_new)
        l_s[...] = alpha * l_s[...] + jnp.sum(e, axis=1, keepdims=True)
        eb = e.astype(BF16)
        pv = None
        for j, pg in enumerate(pages):
            t = _dot(eb[:, j * n:(j + 1) * n], pg)
            pv = t if pv is None else pv + t
        acc_s[...] = alpha * acc_s[...] + pv
        m_s[...] = m_new

    lane_g = lax.broadcasted_iota(jnp.int32, (1, g * PAGE_SIZE), 1)
    kpos = (p * (g * PAGE_SIZE) + lane_g).astype(F32)
    pages_a = [r[0].astype(BF16) for r in ca]
    update(qa_blk, pages_a, slope_a * kpos, ma_s, la_s, acca_s)
    selb = jnp.concatenate([selp_ref[0, j] for j in range(g)], axis=1)
    pages_b = [r[0].astype(BF16) for r in cb]
    update(qb_blk, pages_b, slope_b * kpos + selb, mb_s, lb_s, accb_s)
    fparts = []
    carry = carry_s[...]
    for j in range(g):
        fj = pf[j][0] + carry
        carry = fj[:, PAGE_SIZE - 1:PAGE_SIZE]
        fparts.append(fj)
    carry_s[...] = carry
    pages_c = [r[0].astype(BF16) for r in cc]
    update(qc_blk, pages_c, -jnp.concatenate(fparts, axis=1), mc_s, lc_s, accc_s)

    @pl.when(p == n_steps - 1)
    def _():
        lane1 = lax.broadcasted_iota(jnp.int32, (1, PAGE_SIZE), 1)
        only0 = jnp.where(lane1 == 0, 0.0, NEG_INF)
        new_pos = float(n_pages * PAGE_SIZE)

        def new_page(ref, width):
            subp = lax.broadcasted_iota(jnp.int32, (PAGE_SIZE, width), 0)
            rows_ = jnp.broadcast_to(ref[0], (PAGE_SIZE, width))
            return jnp.where(subp == 0, rows_, 0.0).astype(BF16)

        update(qa_blk, [new_page(na_ref, 1024)], slope_a * new_pos + only0, ma_s, la_s, acca_s)
        update(qb_blk, [new_page(nb_ref, 512)], slope_b * new_pos + seln_ref[0, 0] + only0,
               mb_s, lb_s, accb_s)
        update(qc_blk, [new_page(nc_ref, 512)], -(carry_s[...] + lfn_ref[0]) + only0,
               mc_s, lc_s, accc_s)

        lam_p = lam_ref[...]
        lam = (jnp.exp(jnp.sum(lam_p[0:1] * lam_p[1:2], axis=1, keepdims=True))
               - jnp.exp(jnp.sum(lam_p[2:3] * lam_p[3:4], axis=1, keepdims=True)) + lam_init)
        inv_a = 1.0 / la_s[...]
        inv_b = 1.0 / lb_s[...]
        inv_c = 1.0 / lc_s[...]
        for h in range(4):
            v1 = acca_s[2 * h:2 * h + 1, h * 256 + 128:(h + 1) * 256] * inv_a[2 * h:2 * h + 1]
            v2 = acca_s[2 * h + 1:2 * h + 2, h * 256 + 128:(h + 1) * 256] * inv_a[2 * h + 1:2 * h + 2]
            o = v1 - lam * v2
            o = o * lax.rsqrt(jnp.mean(o * o, axis=1, keepdims=True) + NORM_EPS)
            o = o * g_ref[h:h + 1, :] * (1.0 - lam_init)
            oa_ref[0, :, h * 128:(h + 1) * 128] = o.astype(BF16)
            ob_ref[0, :, h * 64:(h + 1) * 64] = (
                accb_s[h:h + 1, h * 128 + 64:(h + 1) * 128] * inv_b[h:h + 1]).astype(BF16)
            oc_ref[0, :, h * 64:(h + 1) * 64] = (
                accc_s[h:h + 1, h * 128 + 64:(h + 1) * 128] * inv_c[h:h + 1]).astype(BF16)


def _attn_sample(page_table, page_base, ca, cb, cc, pfx, selb4, qa, qb, qc, na, nb, nc, lfn8, lam_p, g,
                 lam_init, group):
    nsamp, n_pages = page_table.shape
    n_steps = n_pages // group

    def page_spec(width, j):
        return pl.BlockSpec((1, PAGE_SIZE, width),
                            lambda s, p, pt: (page_base + pt[s, p * group + j], 0, 0))

    def pf_spec(j):
        return pl.BlockSpec((1, SUBLANES, PAGE_SIZE),
                            lambda s, p, pt: (page_base + pt[s, p * group + j], 0, 0))

    def samp(shape):
        return pl.BlockSpec((1,) + shape, lambda s, p, pt: (s,) + (0,) * len(shape))

    in_specs = ([page_spec(1024, j) for j in range(group)] + [page_spec(512, j) for j in range(group)]
                + [page_spec(512, j) for j in range(group)] + [pf_spec(j) for j in range(group)]
                + [pl.BlockSpec((1, group, 1, PAGE_SIZE), lambda s, p, pt: (s, p, 0, 0)),
                   pl.BlockSpec((1, 1, 1, PAGE_SIZE), lambda s, p, pt: (s, n_pages, 0, 0)),
                   samp((1, 1024)), samp((1, 512)), samp((1, 512)),
                   samp((1, 1024)), samp((1, 512)), samp((1, 512)), samp((SUBLANES, 1)),
                   pl.BlockSpec((4, DK_A), lambda s, p, pt: (0, 0)),
                   pl.BlockSpec((H_A, 2 * DK_A), lambda s, p, pt: (0, 0))])
    out_specs = [samp((1, A_W)), samp((1, B_W)), samp((1, C_W))]
    grid_spec = pltpu.PrefetchScalarGridSpec(
        num_scalar_prefetch=1, grid=(nsamp, n_steps), in_specs=in_specs, out_specs=out_specs,
        scratch_shapes=[pltpu.VMEM((SUBLANES, 1), F32), pltpu.VMEM((SUBLANES, 1), F32),
                        pltpu.VMEM((SUBLANES, 1024), F32),
                        pltpu.VMEM((SUBLANES, 1), F32), pltpu.VMEM((SUBLANES, 1), F32),
                        pltpu.VMEM((SUBLANES, 512), F32),
                        pltpu.VMEM((SUBLANES, 1), F32), pltpu.VMEM((SUBLANES, 1), F32),
                        pltpu.VMEM((SUBLANES, 512), F32),
                        pltpu.VMEM((SUBLANES, 1), F32)])
    return pl.pallas_call(
        functools.partial(_attn_s_kernel, group=group, n_pages=n_pages, lam_init=lam_init),
        grid_spec=grid_spec,
        out_shape=[jax.ShapeDtypeStruct((nsamp, 1, A_W), BF16),
                   jax.ShapeDtypeStruct((nsamp, 1, B_W), BF16),
                   jax.ShapeDtypeStruct((nsamp, 1, C_W), BF16)],
        compiler_params=_cparams("arbitrary", "arbitrary"),
        name="attn_sample",
    )(page_table, *([ca] * group), *([cb] * group), *([cc] * group), *([pfx] * group),
      selb4, selb4, qa, qb, qc, na, nb, nc, lfn8, lam_p, g)


def _layer_weights(l, w1, wg, b_fgate, lam_q1, lam_k1, lam_q2, lam_k2, g_diffnorm, conv_w, pa, pb, pc, pd,
                   wo, ln1_g, ln1_b, wf_in, wf_out, ln2_g, ln2_b):
    bfg_row = jnp.zeros((1, LANES), F32).at[0, MISC_CF:MISC_CF + H_C].set(b_fgate[l])
    return dict(
        w1=w1[l], wg=wg[l], bfg_row=bfg_row, aug_row=jnp.asarray(_build_q_aug_row()),
        lam_p=jnp.stack([lam_q1[l], lam_k1[l], lam_q2[l], lam_k2[l]]),
        g=g_diffnorm[l], conv_w=conv_w[l], pa=pa[l], pb=pb[l], pc=pc[l], pd=pd[l], wo=wo[l],
        ln1_g=ln1_g[l][None, :], ln1_b=ln1_b[l][None, :], wf_in=wf_in[l], wf_out=wf_out[l],
        ln2_g=ln2_g[l][None, :], ln2_b=ln2_b[l][None, :])


def _prompt_layer(x2d, wts, bsz, seq, lam_init):
    n = bsz * seq
    pr = _proj(x2d, wts["w1"], wts["aug_row"], wts["bfg_row"], _row_tile(n, 512))
    r3 = lambda a: a.reshape(bsz, seq, a.shape[-1])
    tq = _row_tile(seq, 256)
    oa = _attn_a(r3(pr["qa"]), r3(pr["na"]), wts["lam_p"], wts["g"], lam_init, tq)
    ob = _attn_b(r3(pr["qb"]), r3(pr["nb"]), r3(pr["iq"]), r3(pr["krep"]), r3(pr["misc"]), tq)
    oc = _attn_c(r3(pr["qc"]), r3(pr["nc"]), r3(pr["misc"]), wts["bfg_row"], tq)
    x1, new_conv = _merge(x2d, oa.reshape(n, A_W), ob.reshape(n, B_W), oc.reshape(n, C_W), pr["dconv"],
                          None, wts, _row_tile(seq, 256), seq)
    x2 = _ffn(x1, wts, _row_tile(n, 256))
    news = (pr["na"].reshape(bsz, seq, H_A, 4 * DK_A), pr["nb"].reshape(bsz, seq, H_B, 2 * DH_B),
            pr["kidx"].reshape(bsz, seq, D_IDX), pr["nc"].reshape(bsz, seq, H_C, 2 * DH_C),
            pr["logf"].reshape(bsz, seq, H_C), new_conv)
    return x2, news


def _sample_layer(x2d, wts, caches, page_base, state, page_table, lam_init):
    nsamp = x2d.shape[0]
    n_pages = page_table.shape[1]
    ca, cb, ck, cc, pfx = caches
    tm = _row_tile(nsamp, 128)
    pr = _proj(x2d, wts["w1"], wts["aug_row"], wts["bfg_row"], tm)
    iq8 = pr["iq"].reshape(nsamp, H_IDX, D_IDX)
    wi8 = pr["misc"][:, MISC_WI:MISC_WI + H_IDX].reshape(nsamp, H_IDX, 1)
    scores = _idx_scores(page_table, ck, page_base, iq8, wi8, pr["kidx"].reshape(nsamp, 1, D_IDX))
    topk = min(INDEX_TOPK_MAX, (n_pages * PAGE_SIZE + 1) // 4)
    selb4 = _select_bias(scores, topk).reshape(nsamp, n_pages + 1, 1, PAGE_SIZE)
    lfn8 = jnp.pad(pr["logf"], ((0, 0), (0, SUBLANES - H_C))).reshape(nsamp, SUBLANES, 1)
    r3 = lambda a: a.reshape(nsamp, 1, a.shape[-1])
    group = 4 if n_pages % 4 == 0 else 1
    oa, ob, oc = _attn_sample(page_table, page_base, ca, cb, cc, pfx, selb4,
                              r3(pr["qa"]), r3(pr["qb"]), r3(pr["qc"]), r3(pr["na"]), r3(pr["nb"]), r3(pr["nc"]),
                              lfn8, wts["lam_p"], wts["g"], lam_init, group)
    x1, conv2 = _merge(x2d, oa.reshape(nsamp, A_W), ob.reshape(nsamp, B_W), oc.reshape(nsamp, C_W),
                       pr["dconv"], state.reshape(nsamp, (CONV_K - 1) * CONV_W), wts, tm, 1)
    x2 = _ffn(x1, wts, tm)
    news = (pr["na"].reshape(nsamp, 1, H_A, 4 * DK_A), pr["nb"].reshape(nsamp, 1, H_B, 2 * DH_B),
            pr["kidx"].reshape(nsamp, 1, D_IDX), pr["nc"].reshape(nsamp, 1, H_C, 2 * DH_C),
            pr["logf"].reshape(nsamp, 1, H_C), conv2.reshape(nsamp, CONV_K - 1, CONV_W))
    return x2, news


def kernel(x_prompt, x_sample, cache_a_kv, cache_b_kv, cache_b_kidx, cache_c_kv, cache_c_logf, state_conv, page_table, w_in, b_fgate, lam_q1, lam_k1, lam_q2, lam_k2, g_diffnorm, conv_w, w_br_a, w_br_b, w_br_c, w_br_d, w_o, ln1_g, ln1_b, w_ffn_in, w_ffn_out, ln2_g, ln2_b):
    bsz, seq, _ = x_prompt.shape
    nsamp = x_sample.shape[0]
    depth = w_in.shape[0]
    src = jnp.asarray(np.maximum(_W1_SRC, 0).astype(np.int32))
    colscale = jnp.asarray(np.where(_W1_SRC >= 0, _W1_SCALE, 0.0).astype(np.float32))
    w1 = (jnp.take(w_in, src, axis=2) * colscale).astype(BF16)
    wg = w_in[:, :, O_G:].astype(BF16)
    bf = lambda a: a.astype(BF16)
    umat = jnp.asarray(_build_prefix_matrix(), dtype=BF16)
    n_pool = cache_a_kv.shape[1]
    all_pages = depth * n_pool
    pfx = _page_prefix(cache_c_logf.reshape(all_pages, PAGE_SIZE * H_C), umat)
    caches = (cache_a_kv.reshape(all_pages, PAGE_SIZE, 1024), cache_b_kv.reshape(all_pages, PAGE_SIZE, 512),
              cache_b_kidx.reshape(all_pages, PAGE_SIZE, D_IDX), cache_c_kv.reshape(all_pages, PAGE_SIZE, 512),
              pfx)

    yp = x_prompt.reshape(bsz * seq, D_MODEL)
    ys = x_sample.reshape(nsamp, D_MODEL)
    news_p, news_s = [], []
    pa, pb, pc, pd, wo, wf_in, wf_out = (bf(w_br_a), bf(w_br_b), bf(w_br_c), bf(w_br_d), bf(w_o),
                                         bf(w_ffn_in), bf(w_ffn_out))
    for l in range(depth):
        wts = _layer_weights(l, w1, wg, b_fgate, lam_q1, lam_k1, lam_q2, lam_k2, g_diffnorm, conv_w,
                             pa, pb, pc, pd, wo, ln1_g, ln1_b, wf_in, wf_out, ln2_g, ln2_b)
        lam_init = 0.8 - 0.6 * math.exp(-0.3 * l)
        yp, new_p = _prompt_layer(yp, wts, bsz, seq, lam_init)
        ys, new_s = _sample_layer(ys, wts, caches, l * n_pool, state_conv[l], page_table, lam_init)
        news_p.append(new_p)
        news_s.append(new_s)

    def stack(lst, i):
        return jnp.stack([e[i] for e in lst])

    outs = [yp.reshape(bsz, seq, D_MODEL), ys.reshape(nsamp, 1, D_MODEL)]
    for i in range(6):
        outs.append(stack(news_p, i))
        outs.append(stack(news_s, i))
    return tuple(outs)
```

```python
import functools
import math

import numpy as np
import jax
import jax.numpy as jnp
from jax import lax
from jax.experimental import pallas as pl
from jax.experimental.pallas import tpu as pltpu

F32 = jnp.float32
BF16 = jnp.bfloat16
NEG_INF = float("-inf")
M_INIT = -1e30
INT_MIN = -2 ** 31
KEY_OF_NEG_INF = -2139095041

D_MODEL = 1024
DEPTH = 2
PAGE_SIZE = 128
H_A, DK_A = 4, 64
A_W = H_A * 2 * DK_A
H_B, DH_B = 4, 64
B_W = H_B * DH_B
H_IDX, D_IDX = 8, 32
INDEX_TOPK_MAX = 256
H_C, DH_C = 4, 64
C_W = H_C * DH_C
CONV_W, CONV_K = 256, 3
N_BRANCH = 4
D_FF = -(-8 * D_MODEL // (3 * 256)) * 256
ALPHA = (2 * DEPTH) ** 0.25
LN_EPS = 1e-5
NORM_EPS = 1e-6
N_HEADS = 4

_n = H_A + H_B
_S_ALL = 2.0 ** (-8.0 * (np.arange(_n) + 1) / _n)
SLOPES_A = [float(v) for v in _S_ALL[0::2]]
SLOPES_B = [float(v) for v in _S_ALL[1::2]]

LANES = 128
SUBLANES = 8
VMEM_LIMIT_BYTES = 56 * 1024 * 1024
PKEYS = PAGE_SIZE * N_HEADS

_WIDTHS = (A_W, A_W, A_W, B_W, B_W, B_W, H_IDX * D_IDX, D_IDX, H_IDX,
           C_W, C_W, C_W, H_C, CONV_W, CONV_W, CONV_W, N_BRANCH * D_MODEL)
_OFF = np.concatenate([[0], np.cumsum(_WIDTHS)]).astype(np.int64)
(O_AQ, O_AK, O_AV, O_BQ, O_BK, O_BV, O_BIQ, O_BIK, O_BIW, O_CQ, O_CK, O_CV, O_CF,
 O_DB, O_DC, O_DH, O_G, IN_WIDTH) = [int(v) for v in _OFF]

SEG = {}
_pos = 0
for _name, _w in (("qa", 1024), ("na", 1024), ("qb", 512), ("nb", 512), ("iq", 256), ("krep", 256),
                  ("qc", 512), ("nc", 512), ("dconv", 768), ("kidx", 128), ("misc", 128)):
    SEG[_name] = (_pos, _pos + _w)
    _pos += _w
W1_WIDTH = _pos
MISC_CF = 0
MISC_WI = 8
QA_AUG = 128
QBC_AUG = 64


def _build_w1_columns():
    src = -np.ones((W1_WIDTH,), np.int64)
    scale = np.ones((W1_WIDTH,), np.float32)
    s = SEG["qa"][0]
    for h in range(H_A):
        src[s + h * 256: s + h * 256 + 128] = O_AQ + h * 128 + np.arange(128)
    scale[SEG["qa"][0]:SEG["qa"][1]] = DK_A ** -0.5
    s = SEG["na"][0]
    for h in range(H_A):
        src[s + h * 256: s + h * 256 + 128] = O_AK + h * 128 + np.arange(128)
        src[s + h * 256 + 128: s + (h + 1) * 256] = O_AV + h * 128 + np.arange(128)
    for nm, oq, ok, ov in (("b", O_BQ, O_BK, O_BV), ("c", O_CQ, O_CK, O_CV)):
        s = SEG["q" + nm][0]
        for h in range(4):
            src[s + h * 128: s + h * 128 + 64] = oq + h * 64 + np.arange(64)
        scale[SEG["q" + nm][0]:SEG["q" + nm][1]] = 64 ** -0.5
        s = SEG["n" + nm][0]
        for h in range(4):
            src[s + h * 128: s + h * 128 + 64] = ok + h * 64 + np.arange(64)
            src[s + h * 128 + 64: s + (h + 1) * 128] = ov + h * 64 + np.arange(64)
    s = SEG["iq"][0]
    src[s:s + 256] = O_BIQ + np.arange(256)
    s = SEG["krep"][0]
    for h in range(H_IDX):
        src[s + h * 32: s + (h + 1) * 32] = O_BIK + np.arange(32)
    s = SEG["dconv"][0]
    src[s:s + 768] = O_DB + np.arange(768)
    s = SEG["kidx"][0]
    src[s:s + 32] = O_BIK + np.arange(32)
    s = SEG["misc"][0]
    src[s + MISC_CF: s + MISC_CF + H_C] = O_CF + np.arange(H_C)
    src[s + MISC_WI: s + MISC_WI + H_IDX] = O_BIW + np.arange(H_IDX)
    scale[s + MISC_WI: s + MISC_WI + H_IDX] = H_IDX ** -0.5 * D_IDX ** -0.5
    return src, scale


_W1_SRC, _W1_SCALE = _build_w1_columns()


def _w1_runs():
    runs = []
    i = 0
    while i < W1_WIDTH:
        j = i + 1
        while (j < W1_WIDTH and _W1_SCALE[j] == _W1_SCALE[i]
               and ((_W1_SRC[i] < 0 and _W1_SRC[j] < 0)
                    or (_W1_SRC[i] >= 0 and _W1_SRC[j] == _W1_SRC[i] + (j - i)))):
            j += 1
        runs.append((int(_W1_SRC[i]), j - i, float(_W1_SCALE[i])))
        i = j
    return runs


def _rearranged_weights(w_in_bf16):
    parts = []
    for start, width, scale in _w1_runs():
        if start < 0:
            parts.append(jnp.zeros(w_in_bf16.shape[:2] + (width,), BF16))
        else:
            piece = w_in_bf16[:, :, start:start + width]
            parts.append(piece if scale == 1.0 else piece * jnp.asarray(scale, BF16))
    return jnp.concatenate(parts, axis=2)


def _build_q_aug_row():
    row = np.zeros((1, W1_WIDTH), np.float32)
    for h in range(4):
        a = SEG["qa"][0] + h * 256 + QA_AUG
        row[0, a], row[0, a + 1] = SLOPES_A[h], SLOPES_A[h] * 256.0
        b = SEG["qb"][0] + h * 128 + QBC_AUG
        row[0, b], row[0, b + 1] = SLOPES_B[h], SLOPES_B[h] * 256.0
        c = SEG["qc"][0] + h * 128 + QBC_AUG
        row[0, c:c + 3] = 1.0
    return row


def _cparams(*sem):
    return pltpu.CompilerParams(dimension_semantics=sem, vmem_limit_bytes=VMEM_LIMIT_BYTES)


def _const_spec(shape):
    nd = len(shape)
    return pl.BlockSpec(shape, lambda *_: (0,) * nd, pipeline_mode=pl.Buffered(1))


def _layer_spec(shape, layer):
    nd = len(shape)
    return pl.BlockSpec((None,) + tuple(shape), lambda *_: (layer,) + (0,) * nd,
                        pipeline_mode=pl.Buffered(1))


def _dot(a, b):
    return jnp.dot(a, b, preferred_element_type=F32)


def _dot_nt(a, b):
    return lax.dot_general(a, b, (((1,), (1,)), ((), ())), preferred_element_type=F32)


def _row_tile(n, pref):
    t = min(n, pref)
    while n % t:
        t //= 2
    return t


def _split3(x):
    hi = x.astype(BF16)
    r = x - hi.astype(F32)
    mid = r.astype(BF16)
    lo = (r - mid.astype(F32)).astype(BF16)
    return hi, mid, lo


def _log_sigmoid(x):
    return jnp.minimum(x, 0.0) - jnp.log1p(jnp.exp(-jnp.abs(x)))


def _layernorm(h, g, b):
    mu = jnp.mean(h, axis=-1, keepdims=True)
    d = h - mu
    var = jnp.mean(d * d, axis=-1, keepdims=True)
    return d * lax.rsqrt(var + LN_EPS) * g + b


def _lambda(lam_ref, lam_init):
    lam_p = lam_ref[...]
    return (jnp.exp(jnp.sum(lam_p[0:1] * lam_p[1:2], axis=1, keepdims=True))
            - jnp.exp(jnp.sum(lam_p[2:3] * lam_p[3:4], axis=1, keepdims=True)) + lam_init)


def _causal_classes(seq, tq):
    n_cls = min(4, seq // tq)
    chunk = seq // n_cls
    return n_cls, chunk


def _for_causal_class(qi, tq, seq, body):
    n_cls, chunk = _causal_classes(seq, tq)
    cls = ((qi + 1) * tq - 1) // chunk
    for c in range(n_cls):
        pl.when(cls == c)(functools.partial(body, (c + 1) * chunk))


def _proj_kernel(x_ref, w_ref, aug_ref, bfg_ref, qa_ref, na_ref, qb_ref, nb_ref, iq_ref, kr_ref, qc_ref,
                 nc_ref, dc_ref, misc_ref, kidx_ref, logf_ref):
    xb = x_ref[...].astype(BF16)

    def seg(name):
        lo, hi = SEG[name]
        return _dot(xb, w_ref[:, lo:hi])

    def qseg(name):
        lo, hi = SEG[name]
        return (seg(name) + aug_ref[:, lo:hi]).astype(BF16)

    qa_ref[...] = qseg("qa")
    na_ref[...] = seg("na")
    qb_ref[...] = qseg("qb")
    nb_ref[...] = seg("nb")
    iq_ref[...] = seg("iq").astype(BF16)
    kr_ref[...] = seg("krep").astype(BF16)
    qc_ref[...] = qseg("qc")
    nc_ref[...] = seg("nc")
    dc_ref[...] = seg("dconv")
    kidx_ref[...] = seg("kidx")[:, :D_IDX]
    misc = seg("misc")
    misc_ref[...] = misc
    logf_ref[...] = _log_sigmoid(misc + bfg_ref[...])[:, MISC_CF:MISC_CF + H_C]


def _proj(x2d, wts, layer, tm):
    n = x2d.shape[0]
    widths = dict((k, v[1] - v[0]) for k, v in SEG.items())
    outs = [("qa", BF16, widths["qa"]), ("na", F32, widths["na"]), ("qb", BF16, widths["qb"]),
            ("nb", F32, widths["nb"]), ("iq", BF16, widths["iq"]), ("krep", BF16, widths["krep"]),
            ("qc", BF16, widths["qc"]), ("nc", F32, widths["nc"]), ("dconv", F32, widths["dconv"]),
            ("misc", F32, widths["misc"]), ("kidx", F32, D_IDX), ("logf", F32, H_C)]
    res = pl.pallas_call(
        _proj_kernel,
        grid=(n // tm,),
        in_specs=[pl.BlockSpec((tm, D_MODEL), lambda i: (i, 0)),
                  _layer_spec((D_MODEL, W1_WIDTH), layer),
                  _const_spec((1, W1_WIDTH)),
                  _layer_spec((1, LANES), layer)],
        out_specs=[pl.BlockSpec((tm, w), lambda i: (i, 0)) for _, _, w in outs],
        out_shape=[jax.ShapeDtypeStruct((n, w), dt) for _, dt, w in outs],
        compiler_params=_cparams("parallel"),
        name="proj",
    )(x2d, wts["w1"], wts["aug_row"], wts["bfg_row"])
    return dict(zip([o[0] for o in outs], res))


def _topk_select(score, k):
    rows, width = score.shape
    bits = lax.bitcast_convert_type(score, jnp.int32)
    key = jnp.where(bits < 0, bits ^ jnp.int32(0x7FFFFFFF), bits)
    kf = jnp.float32(k)

    def count(mask):
        return jnp.sum(jnp.where(mask, 1.0, 0.0), axis=1, keepdims=True)

    t0 = jnp.where(count(key >= 0) >= kf, jnp.int32(0), jnp.int32(INT_MIN))

    def vbody(i, t):
        cand = t + jnp.left_shift(jnp.int32(1), jnp.int32(30) - i)
        return jnp.where(count(key >= cand) >= kf, cand, t)

    t = lax.fori_loop(0, 31, vbody, t0)
    gt = key > t
    eq = key == t
    n_gt = count(gt)
    need = kf - n_gt
    idx = lax.broadcasted_iota(jnp.int32, (rows, width), 1)
    nbits = max(1, int(math.ceil(math.log2(width))))

    surplus = jnp.where(t > KEY_OF_NEG_INF, n_gt + count(eq) - kf, 0.0)
    tied = jnp.max(surplus) > 0.0

    def tie_cut():
        def ibody(i, j):
            cand = j + jnp.left_shift(jnp.int32(1), jnp.int32(nbits - 1) - i)
            return jnp.where(count(eq & (idx < cand)) < need, cand, j)

        return lax.fori_loop(0, nbits, ibody, jnp.zeros((rows, 1), jnp.int32))

    j = lax.cond(tied, tie_cut, lambda: jnp.full((rows, 1), width, jnp.int32))
    return gt | (eq & (idx <= j))


def _attn_a_kernel(qa_ref, na_ref, lam_ref, g_ref, o_ref, k1_s, k2_s, v_s, *, tq, seq, lam_init):
    qi = pl.program_id(1)

    @pl.when(qi == 0)
    def _():
        lane = lax.broadcasted_iota(jnp.int32, (seq, LANES), 1)
        kp = lax.broadcasted_iota(jnp.int32, (seq, LANES), 0)
        kaug = jnp.where(lane == 0, (kp & 255).astype(F32),
                         jnp.where(lane == 1, (kp >> 8).astype(F32), 0.0)).astype(BF16)
        for h in range(H_A):
            kk = na_ref[0, :, h * 256:h * 256 + 128]
            k1_s[h, :, 0:LANES] = jnp.where(lane < 64, kk, 0.0).astype(BF16)
            k1_s[h, :, LANES:2 * LANES] = kaug
            k2_s[h, :, 0:LANES] = jnp.where(lane >= 64, kk, 0.0).astype(BF16)
            k2_s[h, :, LANES:2 * LANES] = kaug
            v_s[h] = na_ref[0, :, h * 256 + 128:(h + 1) * 256].astype(BF16)

    lam = _lambda(lam_ref, lam_init)

    def body(width):
        rows = qi * tq + lax.broadcasted_iota(jnp.int32, (tq, width), 0)
        cols = lax.broadcasted_iota(jnp.int32, (tq, width), 1)
        causal = cols <= rows
        for h in range(H_A):
            qh = qa_ref[0, :, h * 256:(h + 1) * 256]
            s1 = jnp.where(causal, _dot_nt(qh, k1_s[h, 0:width, :]), NEG_INF)
            s2 = jnp.where(causal, _dot_nt(qh, k2_s[h, 0:width, :]), NEG_INF)
            e1 = jnp.exp(s1 - jnp.max(s1, axis=1, keepdims=True))
            e2 = jnp.exp(s2 - jnp.max(s2, axis=1, keepdims=True))
            r1 = 1.0 / jnp.sum(e1, axis=1, keepdims=True)
            r2 = lam / jnp.sum(e2, axis=1, keepdims=True)
            p = (e1 * r1 - e2 * r2).astype(BF16)
            o = _dot(p, v_s[h, 0:width, :])
            o = o * lax.rsqrt(jnp.mean(o * o, axis=1, keepdims=True) + NORM_EPS)
            o = o * g_ref[h:h + 1, :] * (1.0 - lam_init)
            o_ref[0, :, h * 128:(h + 1) * 128] = o.astype(BF16)

    _for_causal_class(qi, tq, seq, body)


def _attn_a(qa, na, wts, layer, lam_init, tq):
    bsz, seq, _ = qa.shape
    return pl.pallas_call(
        functools.partial(_attn_a_kernel, tq=tq, seq=seq, lam_init=lam_init),
        grid=(bsz, seq // tq),
        in_specs=[pl.BlockSpec((1, tq, 1024), lambda b, i: (b, i, 0)),
                  pl.BlockSpec((1, seq, 1024), lambda b, i: (b, 0, 0)),
                  _layer_spec((4, DK_A), layer),
                  _layer_spec((H_A, 2 * DK_A), layer)],
        out_specs=pl.BlockSpec((1, tq, A_W), lambda b, i: (b, i, 0)),
        out_shape=jax.ShapeDtypeStruct((bsz, seq, A_W), BF16),
        scratch_shapes=[pltpu.VMEM((H_A, seq, 2 * LANES), BF16), pltpu.VMEM((H_A, seq, 2 * LANES), BF16),
                        pltpu.VMEM((H_A, seq, LANES), BF16)],
        compiler_params=_cparams("parallel", "arbitrary"),
        name="attn_a",
    )(qa, na, wts["lam_p"], wts["g"])


def _attn_c_kernel(qc_ref, nc_ref, misc_ref, bfg_ref, o_ref, k_s, v_s, *, tq, seq):
    qi = pl.program_id(1)

    @pl.when(qi == 0)
    def _():
        lane = lax.broadcasted_iota(jnp.int32, (seq, LANES), 1)
        ch = min(256, seq)
        tri = jnp.where(lax.broadcasted_iota(jnp.int32, (ch, ch), 0)
                        >= lax.broadcasted_iota(jnp.int32, (ch, ch), 1), 1.0, 0.0).astype(BF16)
        carry = jnp.zeros((1, LANES), F32)
        chunks = []
        for c in range(seq // ch):
            lf = _log_sigmoid(misc_ref[0, c * ch:(c + 1) * ch, :] + bfg_ref[...])
            hi, mid, lo = _split3(lf)
            fc = (_dot(tri, hi) + _dot(tri, mid)) + _dot(tri, lo) + carry
            carry = fc[ch - 1:ch, :]
            chunks.append(fc)
        fcum = jnp.concatenate(chunks, axis=0) if len(chunks) > 1 else chunks[0]
        for h in range(H_C):
            fh = jnp.broadcast_to(fcum[:, MISC_CF + h:MISC_CF + h + 1], (seq, LANES))
            hi, mid, lo = (v.astype(F32) for v in _split3(-fh))
            kv = nc_ref[0, :, h * 128:(h + 1) * 128]
            aug = jnp.where(lane == QBC_AUG, hi,
                            jnp.where(lane == QBC_AUG + 1, mid, jnp.where(lane == QBC_AUG + 2, lo, 0.0)))
            k_s[h] = jnp.where(lane < 64, kv, aug).astype(BF16)
            v_s[h] = kv.astype(BF16)

    def body(width):
        rows = qi * tq + lax.broadcasted_iota(jnp.int32, (tq, width), 0)
        cols = lax.broadcasted_iota(jnp.int32, (tq, width), 1)
        causal = cols <= rows
        for h in range(H_C):
            qh = qc_ref[0, :, h * 128:(h + 1) * 128]
            s = jnp.where(causal, _dot_nt(qh, k_s[h, 0:width, :]), NEG_INF)
            e = jnp.exp(s - jnp.max(s, axis=1, keepdims=True))
            r = 1.0 / jnp.sum(e, axis=1, keepdims=True)
            o = _dot((e * r).astype(BF16), v_s[h, 0:width, :])
            o_ref[0, :, h * 64:(h + 1) * 64] = o[:, 64:128].astype(BF16)

    _for_causal_class(qi, tq, seq, body)


def _attn_c(qc, nc, misc, wts, layer, tq):
    bsz, seq, _ = qc.shape
    return pl.pallas_call(
        functools.partial(_attn_c_kernel, tq=tq, seq=seq),
        grid=(bsz, seq // tq),
        in_specs=[pl.BlockSpec((1, tq, 512), lambda b, i: (b, i, 0)),
                  pl.BlockSpec((1, seq, 512), lambda b, i: (b, 0, 0)),
                  pl.BlockSpec((1, seq, LANES), lambda b, i: (b, 0, 0)),
                  _layer_spec((1, LANES), layer)],
        out_specs=pl.BlockSpec((1, tq, C_W), lambda b, i: (b, i, 0)),
        out_shape=jax.ShapeDtypeStruct((bsz, seq, C_W), BF16),
        scratch_shapes=[pltpu.VMEM((H_C, seq, LANES), BF16)] * 2,
        compiler_params=_cparams("parallel", "arbitrary"),
        name="attn_c",
    )(qc, nc, misc, wts["bfg_row"])


def _attn_b_kernel(qb_ref, nb_ref, iq_ref, kr_ref, misc_ref, o_ref, k_s, v_s, kr_s, *, tq, seq, topk):
    qi = pl.program_id(1)

    @pl.when(qi == 0)
    def _():
        lane = lax.broadcasted_iota(jnp.int32, (seq, LANES), 1)
        kp = lax.broadcasted_iota(jnp.int32, (seq, LANES), 0)
        aug = jnp.where(lane == QBC_AUG, (kp & 255).astype(F32),
                        jnp.where(lane == QBC_AUG + 1, (kp >> 8).astype(F32), 0.0))
        for h in range(H_B):
            kv = nb_ref[0, :, h * 128:(h + 1) * 128]
            k_s[h] = jnp.where(lane < 64, kv, aug).astype(BF16)
            v_s[h] = kv.astype(BF16)
        kr = kr_ref[0]
        klane = lax.broadcasted_iota(jnp.int32, (seq, 256), 1)
        for h in range(H_IDX):
            kr_s[h] = kr * jnp.where((klane >> 5) == h, 1.0, 0.0).astype(BF16)

    def body(width):
        rows = qi * tq + lax.broadcasted_iota(jnp.int32, (tq, width), 0)
        cols = lax.broadcasted_iota(jnp.int32, (tq, width), 1)
        causal = cols <= rows
        iq = iq_ref[0]
        score = jnp.zeros((tq, width), F32)
        for h in range(H_IDX):
            rel = jnp.maximum(_dot_nt(iq, kr_s[h, 0:width, :]), 0.0)
            score = score + rel * misc_ref[0, :, MISC_WI + h:MISC_WI + h + 1]
        score = jnp.where(causal, score, NEG_INF)
        sel = _topk_select(score, topk)
        bias = jnp.where(sel & causal, 0.0, NEG_INF)
        for h in range(H_B):
            qh = qb_ref[0, :, h * 128:(h + 1) * 128]
            s = _dot_nt(qh, k_s[h, 0:width, :]) + bias
            e = jnp.exp(s - jnp.max(s, axis=1, keepdims=True))
            r = 1.0 / jnp.sum(e, axis=1, keepdims=True)
            o = _dot((e * r).astype(BF16), v_s[h, 0:width, :])
            o_ref[0, :, h * 64:(h + 1) * 64] = o[:, 64:128].astype(BF16)

    _for_causal_class(qi, tq, seq, body)


def _attn_b(qb, nb, iq, krep, misc, tq):
    bsz, seq, _ = qb.shape
    topk = min(INDEX_TOPK_MAX, seq // 4)
    return pl.pallas_call(
        functools.partial(_attn_b_kernel, tq=tq, seq=seq, topk=topk),
        grid=(bsz, seq // tq),
        in_specs=[pl.BlockSpec((1, tq, 512), lambda b, i: (b, i, 0)),
                  pl.BlockSpec((1, seq, 512), lambda b, i: (b, 0, 0)),
                  pl.BlockSpec((1, tq, 256), lambda b, i: (b, i, 0)),
                  pl.BlockSpec((1, seq, 256), lambda b, i: (b, 0, 0)),
                  pl.BlockSpec((1, tq, LANES), lambda b, i: (b, i, 0))],
        out_specs=pl.BlockSpec((1, tq, B_W), lambda b, i: (b, i, 0)),
        out_shape=jax.ShapeDtypeStruct((bsz, seq, B_W), BF16),
        scratch_shapes=[pltpu.VMEM((H_B, seq, LANES), BF16), pltpu.VMEM((H_B, seq, LANES), BF16),
                        pltpu.VMEM((H_IDX, seq, H_IDX * D_IDX), BF16)],
        compiler_params=_cparams("parallel", "arbitrary"),
        name="attn_b",
    )(qb, nb, iq, krep, misc)


def _merge_kernel(*refs, tm, sample, tiles_per_seq):
    if sample:
        (x_ref, oa_ref, ob_ref, oc_ref, dc_ref, st_ref, wg_ref, pa_ref, pb_ref, pc_ref, pd_ref,
         wo_ref, g_ref, b_ref, cw_ref, y_ref, conv_ref) = refs
    else:
        (x_ref, oa_ref, ob_ref, oc_ref, dc_ref, halo_ref, wg_ref, pa_ref, pb_ref, pc_ref, pd_ref,
         wo_ref, g_ref, b_ref, cw_ref, y_ref, conv_ref, uu_ref) = refs
    w0 = cw_ref[0:1, :]
    w1 = cw_ref[1:2, :]
    w2 = cw_ref[2:3, :]
    u = dc_ref[:, 256:512] * dc_ref[:, 512:768]
    if sample:
        s0 = st_ref[:, 0:256]
        s1 = st_ref[:, 256:512]
        y = s0 * w0 + s1 * w1 + u * w2
        conv_ref[:, 0:256] = s1
        conv_ref[:, 256:512] = u
    else:
        first = (pl.program_id(0) % tiles_per_seq) == 0
        uh = halo_ref[:, 256:512] * halo_ref[:, 512:768]
        uu_ref[0:SUBLANES, :] = jnp.where(first, 0.0, uh)
        uu_ref[SUBLANES:SUBLANES + tm, :] = u
        y = uu_ref[SUBLANES - 2:SUBLANES - 2 + tm, :] * w0 + uu_ref[SUBLANES - 1:SUBLANES - 1 + tm, :] * w1 + u * w2
        conv_ref[0] = uu_ref[SUBLANES + tm - 2:SUBLANES + tm, :]
    o_d = (dc_ref[:, 0:256] * y).astype(BF16)
    xv = x_ref[...]
    xb = xv.astype(BF16)
    mixed = None
    for i, (o, p_ref) in enumerate(((oa_ref[...], pa_ref), (ob_ref[...], pb_ref),
                                    (oc_ref[...], pc_ref), (o_d, pd_ref))):
        gate = jax.nn.sigmoid(_dot(xb, wg_ref[:, i * D_MODEL:(i + 1) * D_MODEL]))
        term = gate * _dot(o, p_ref[...])
        mixed = term if mixed is None else mixed + term
    h = ALPHA * xv + _dot(mixed.astype(BF16), wo_ref[...])
    y_ref[...] = _layernorm(h, g_ref[...], b_ref[...])


def _merge(x2d, oa, ob, oc, dconv, state2d, wts, layer, tm, seq):
    n = x2d.shape[0]
    sample = state2d is not None
    row = lambda w: pl.BlockSpec((tm, w), lambda i: (i, 0))
    in_specs = [row(D_MODEL), row(A_W), row(B_W), row(C_W), row(768)]
    args = [x2d, oa, ob, oc, dconv]
    if sample:
        in_specs.append(row(512))
        args.append(state2d)
        out_specs = [row(D_MODEL), row(512)]
        out_shape = [jax.ShapeDtypeStruct((n, D_MODEL), F32), jax.ShapeDtypeStruct((n, 512), F32)]
        scratch = []
        tiles_per_seq = 1
    else:
        tiles_per_seq = seq // tm
        r8 = tm // SUBLANES
        in_specs.append(pl.BlockSpec((SUBLANES, 768), lambda i: (jnp.maximum(i * r8 - 1, 0), 0)))
        args.append(dconv)
        out_specs = [row(D_MODEL), pl.BlockSpec((1, CONV_K - 1, CONV_W), lambda i: (i // tiles_per_seq, 0, 0))]
        out_shape = [jax.ShapeDtypeStruct((n, D_MODEL), F32),
                     jax.ShapeDtypeStruct((n // seq, CONV_K - 1, CONV_W), F32)]
        scratch = [pltpu.VMEM((tm + SUBLANES, CONV_W), F32)]
    for name in ("wg", "pa", "pb", "pc", "pd", "wo", "ln1_g", "ln1_b", "conv_w"):
        in_specs.append(_layer_spec(wts[name].shape[1:], layer))
        args.append(wts[name])
    return pl.pallas_call(
        functools.partial(_merge_kernel, tm=tm, sample=sample, tiles_per_seq=tiles_per_seq),
        grid=(n // tm,),
        in_specs=in_specs, out_specs=out_specs, out_shape=out_shape, scratch_shapes=scratch,
        compiler_params=_cparams("arbitrary"),
        name="merge_s" if sample else "merge",
    )(*args)


def _ffn_kernel(x_ref, wi_ref, wo_ref, g_ref, b_ref, y_ref):
    xv = x_ref[...]
    xb = xv.astype(BF16)
    hg = _dot(xb, wi_ref[:, 0:D_FF])
    hu = _dot(xb, wi_ref[:, D_FF:2 * D_FF])
    act = (hg * jax.nn.sigmoid(hg) * hu).astype(BF16)
    h = ALPHA * xv + _dot(act, wo_ref[...])
    y_ref[...] = _layernorm(h, g_ref[...], b_ref[...])


def _ffn(x2d, wts, layer, tm):
    n = x2d.shape[0]
    return pl.pallas_call(
        _ffn_kernel,
        grid=(n // tm,),
        in_specs=[pl.BlockSpec((tm, D_MODEL), lambda i: (i, 0)),
                  _layer_spec((D_MODEL, 2 * D_FF), layer), _layer_spec((D_FF, D_MODEL), layer),
                  _layer_spec((1, D_MODEL), layer), _layer_spec((1, D_MODEL), layer)],
        out_specs=pl.BlockSpec((tm, D_MODEL), lambda i: (i, 0)),
        out_shape=jax.ShapeDtypeStruct((n, D_MODEL), F32),
        compiler_params=_cparams("parallel"),
        name="ffn",
    )(x2d, wts["wf_in"], wts["wf_out"], wts["ln2_g"], wts["ln2_b"])


def _build_prefix_matrix():
    c = np.arange(PKEYS)
    same_head = (c[:, None] % N_HEADS) == (c[None, :] % N_HEADS)
    upto = (c[:, None] // N_HEADS) <= (c[None, :] // N_HEADS)
    return np.concatenate([same_head & upto, same_head], axis=1).astype(np.float32)


def _build_key_expansion():
    return (np.arange(PAGE_SIZE)[:, None] == (np.arange(PKEYS)[None, :] // N_HEADS)).astype(np.float32)


def _pfx_kernel(x_ref, u_ref, o_ref):
    hi, mid, lo = _split3(x_ref[...])
    u = u_ref[...]
    o_ref[...] = (_dot(hi, u) + _dot(mid, u)) + _dot(lo, u)


def _page_prefix(logf_pages, umat):
    n_pool = logf_pages.shape[0]
    tp = _row_tile(n_pool, 512) if n_pool % SUBLANES == 0 else n_pool
    return pl.pallas_call(
        _pfx_kernel,
        grid=(n_pool // tp,),
        in_specs=[pl.BlockSpec((tp, PKEYS), lambda i: (i, 0)),
                  _const_spec(umat.shape)],
        out_specs=pl.BlockSpec((tp, 2 * PKEYS), lambda i: (i, 0)),
        out_shape=jax.ShapeDtypeStruct((n_pool, 2 * PKEYS), F32),
        compiler_params=_cparams("parallel"),
        name="page_prefix",
    )(logf_pages, umat)


def _idx_kernel(pt_ref, *refs, n_pages):
    pages = refs[:n_pages]
    iq_ref, wi_ref, knew_ref, o_ref = refs[n_pages:]
    iq = iq_ref[0]
    wi = wi_ref[0]

    def weighted(rel):
        return jnp.sum(jnp.maximum(rel, 0.0) * wi, axis=0, keepdims=True)

    for j in range(n_pages):
        o_ref[0, :, j * PAGE_SIZE:(j + 1) * PAGE_SIZE] = weighted(_dot(iq, pages[j][0].astype(BF16)))
    knew = jnp.broadcast_to(knew_ref[0], (PAGE_SIZE, D_IDX)).astype(BF16)
    lane = lax.broadcasted_iota(jnp.int32, (1, PAGE_SIZE), 1)
    o_ref[0, :, n_pages * PAGE_SIZE:(n_pages + 1) * PAGE_SIZE] = jnp.where(
        lane == 0, weighted(_dot_nt(iq, knew)), NEG_INF)


def _idx_scores(page_table, kidx_t, page_base, iq8, wi8, knew):
    nsamp, n_pages = page_table.shape
    width = (n_pages + 1) * PAGE_SIZE

    def page_spec(j):
        return pl.BlockSpec((1, D_IDX, PAGE_SIZE), lambda s, pt: (page_base + pt[s, j], 0, 0))

    grid_spec = pltpu.PrefetchScalarGridSpec(
        num_scalar_prefetch=1,
        grid=(nsamp,),
        in_specs=[page_spec(j) for j in range(n_pages)] + [
            pl.BlockSpec((1, H_IDX, D_IDX), lambda s, pt: (s, 0, 0)),
            pl.BlockSpec((1, H_IDX, 1), lambda s, pt: (s, 0, 0)),
            pl.BlockSpec((1, 1, D_IDX), lambda s, pt: (s, 0, 0))],
        out_specs=pl.BlockSpec((1, 1, width), lambda s, pt: (s, 0, 0)),
    )
    out = pl.pallas_call(
        functools.partial(_idx_kernel, n_pages=n_pages),
        grid_spec=grid_spec,
        out_shape=jax.ShapeDtypeStruct((nsamp, 1, width), F32),
        compiler_params=_cparams("arbitrary"),
        name="idx_scores",
    )(page_table, *([kidx_t] * n_pages), iq8, wi8, knew)
    return out.reshape(nsamp, width)


def _sel_kernel(s_ref, e_ref, o_ref, *, topk, n_blocks):
    sel = jnp.where(_topk_select(s_ref[...], topk), 1.0, 0.0).astype(BF16)
    for j in range(n_blocks):
        rep = _dot(sel[:, j * PAGE_SIZE:(j + 1) * PAGE_SIZE], e_ref[...])
        o_ref[:, j * PKEYS:(j + 1) * PKEYS] = jnp.where(rep > 0.5, 0.0, NEG_INF)


def _select_bias(scores, expand, topk):
    nsamp, width = scores.shape
    n_blocks = width // PAGE_SIZE
    return pl.pallas_call(
        functools.partial(_sel_kernel, topk=topk, n_blocks=n_blocks),
        out_shape=jax.ShapeDtypeStruct((nsamp, n_blocks * PKEYS), F32),
        compiler_params=pltpu.CompilerParams(vmem_limit_bytes=VMEM_LIMIT_BYTES),
        name="select_bias",
    )(scores, expand)


def _attn_s_kernel(pt_ref, *refs, group, n_pages, lam_init):
    g = group
    ca = refs[0:g]
    cb = refs[g:2 * g]
    cc = refs[2 * g:3 * g]
    pf = refs[3 * g:4 * g]
    (selp_ref, seln_ref, qa_ref, qb_ref, qc_ref, na_ref, nb_ref, nc_ref, lfn_ref, lam_ref, g_ref,
     oa_ref, ob_ref, oc_ref,
     ma_s, la_s, acca_s, mb_s, lb_s, accb_s, mc_s, lc_s, accc_s, carry_s, xa_s, xb_s, xc_s) = refs[4 * g:]
    p = pl.program_id(1)
    n_steps = n_pages // g

    @pl.when(p == 0)
    def _():
        for m_s, l_s, acc_s in ((ma_s, la_s, acca_s), (mb_s, lb_s, accb_s), (mc_s, lc_s, accc_s)):
            m_s[...] = jnp.full(m_s.shape, M_INIT, F32)
            l_s[...] = jnp.zeros(l_s.shape, F32)
            acc_s[...] = jnp.zeros(acc_s.shape, F32)
        carry_s[...] = jnp.zeros(carry_s.shape, F32)

    sub_a = lax.broadcasted_iota(jnp.int32, (SUBLANES, 256), 0)
    lane_a = lax.broadcasted_iota(jnp.int32, (SUBLANES, 256), 1)
    qa_row = qa_ref[0].astype(F32)
    qa_blk = jnp.zeros((SUBLANES, 256), F32)
    for h in range(N_HEADS):
        qa_blk = jnp.where((sub_a >> 1) == h, jnp.broadcast_to(qa_row[:, h * 256:(h + 1) * 256], (SUBLANES, 256)),
                           qa_blk)
    qa_blk = jnp.where((lane_a < 128) & ((lane_a >> 6) == (sub_a & 1)), qa_blk, 0.0).astype(BF16)
    sub_b = lax.broadcasted_iota(jnp.int32, (SUBLANES, LANES), 0)
    lane_b = lax.broadcasted_iota(jnp.int32, (SUBLANES, LANES), 1)

    def head_block(row):
        blk = jnp.zeros((SUBLANES, LANES), F32)
        for h in range(N_HEADS):
            blk = jnp.where(sub_b == h, jnp.broadcast_to(row[:, h * 128:(h + 1) * 128], (SUBLANES, LANES)), blk)
        return jnp.where(lane_b < 64, blk, 0.0).astype(BF16)

    qb_blk = head_block(qb_ref[0].astype(F32))
    qc_blk = head_block(qc_ref[0].astype(F32))

    sub1 = lax.broadcasted_iota(jnp.int32, (SUBLANES, 1), 0)
    slope_a = jnp.zeros((SUBLANES, 1), F32)
    slope_b = jnp.zeros((SUBLANES, 1), F32)
    for h in range(N_HEADS):
        slope_a = jnp.where((sub1 >> 1) == h, SLOPES_A[h], slope_a)
        slope_b = jnp.where(sub1 == h, SLOPES_B[h], slope_b)

    def head_masks(width):
        sub = lax.broadcasted_iota(jnp.int32, (SUBLANES, width), 0)
        lane = lax.broadcasted_iota(jnp.int32, (SUBLANES, width), 1)
        own_a = jnp.where((lane & 3) == (sub >> 1), 0.0, NEG_INF)
        own_b = jnp.where((lane & 3) == sub, 0.0, NEG_INF)
        return own_a, own_b

    def update(qblk, pages, bias, m_s, l_s, acc_s):
        n = pages[0].shape[0]
        s = jnp.concatenate([_dot_nt(qblk, pg) for pg in pages], axis=1) + bias
        m_old = m_s[...]
        m_new = jnp.maximum(m_old, jnp.max(s, axis=1, keepdims=True))
        alpha = jnp.exp(m_old - m_new)
        e = jnp.exp(s - m_new)
        l_s[...] = alpha * l_s[...] + jnp.sum(e, axis=1, keepdims=True)
        eb = e.astype(BF16)
        pv = None
        for j, pg in enumerate(pages):
            t = _dot(eb[:, j * n:(j + 1) * n], pg)
            pv = t if pv is None else pv + t
        acc_s[...] = alpha * acc_s[...] + pv
        m_s[...] = m_new

    own_a, own_b = head_masks(g * PKEYS)
    lane_g = lax.broadcasted_iota(jnp.int32, (1, g * PKEYS), 1)
    kpos = (p * (g * PAGE_SIZE) + (lane_g >> 2)).astype(F32)
    pages_a = [r[0].reshape(PKEYS, 256).astype(BF16) for r in ca]
    update(qa_blk, pages_a, slope_a * kpos + own_a, ma_s, la_s, acca_s)
    selb = jnp.concatenate([selp_ref[0, j] for j in range(g)], axis=1)
    pages_b = [r[0].reshape(PKEYS, 128).astype(BF16) for r in cb]
    update(qb_blk, pages_b, slope_b * kpos + selb + own_b, mb_s, lb_s, accb_s)
    fparts = []
    carry = carry_s[...]
    for j in range(g):
        fparts.append(pf[j][0, :, 0:PKEYS] + carry)
        carry = carry + pf[j][0, :, PKEYS:2 * PKEYS]
    carry_s[...] = carry
    pages_c = [r[0].reshape(PKEYS, 128).astype(BF16) for r in cc]
    update(qc_blk, pages_c, own_b - jnp.concatenate(fparts, axis=1), mc_s, lc_s, accc_s)

    @pl.when(p == n_steps - 1)
    def _():
        new_pos = float(n_pages * PAGE_SIZE)
        own_a1, own_b1 = head_masks(PAGE_SIZE)
        lane1 = lax.broadcasted_iota(jnp.int32, (1, PAGE_SIZE), 1)
        valid = jnp.where(lane1 < N_HEADS, 0.0, NEG_INF)

        def new_rows(x_s, ref):
            x_s[...] = jnp.zeros(x_s.shape, F32)
            x_s[0:N_HEADS, :] = ref[0]
            return x_s[...].astype(BF16)

        update(qa_blk, [new_rows(xa_s, na_ref)], slope_a * new_pos + own_a1 + valid, ma_s, la_s, acca_s)
        update(qb_blk, [new_rows(xb_s, nb_ref)],
               slope_b * new_pos + seln_ref[0, 0][:, 0:PAGE_SIZE] + own_b1 + valid, mb_s, lb_s, accb_s)
        f_new = carry_s[...][:, 0:PAGE_SIZE] + lfn_ref[0]
        update(qc_blk, [new_rows(xc_s, nc_ref)], own_b1 + valid - f_new, mc_s, lc_s, accc_s)

        lam = _lambda(lam_ref, lam_init)
        inv_a = 1.0 / la_s[...]
        inv_b = 1.0 / lb_s[...]
        inv_c = 1.0 / lc_s[...]
        for h in range(N_HEADS):
            v1 = acca_s[2 * h:2 * h + 1, 128:256] * inv_a[2 * h:2 * h + 1]
            v2 = acca_s[2 * h + 1:2 * h + 2, 128:256] * inv_a[2 * h + 1:2 * h + 2]
            o = v1 - lam * v2
            o = o * lax.rsqrt(jnp.mean(o * o, axis=1, keepdims=True) + NORM_EPS)
            o = o * g_ref[h:h + 1, :] * (1.0 - lam_init)
            oa_ref[0, :, h * 128:(h + 1) * 128] = o.astype(BF16)
            ob_ref[0, :, h * 64:(h + 1) * 64] = (accb_s[h:h + 1, 64:128] * inv_b[h:h + 1]).astype(BF16)
            oc_ref[0, :, h * 64:(h + 1) * 64] = (accc_s[h:h + 1, 64:128] * inv_c[h:h + 1]).astype(BF16)


def _attn_sample(page_table, page_base, ca, cb, cc, pfx, selb4, qa, qb, qc, na, nb, nc, lfn, wts, layer,
                 lam_init, group):
    nsamp, n_pages = page_table.shape
    n_steps = n_pages // group

    def page_spec(width, j):
        return pl.BlockSpec((1, PAGE_SIZE, N_HEADS, width),
                            lambda s, p, pt: (page_base + pt[s, p * group + j], 0, 0, 0))

    def pf_spec(j):
        return pl.BlockSpec((1, 1, 2 * PKEYS), lambda s, p, pt: (page_base + pt[s, p * group + j], 0, 0))

    def samp(shape):
        return pl.BlockSpec((1,) + shape, lambda s, p, pt: (s,) + (0,) * len(shape))

    in_specs = ([page_spec(256, j) for j in range(group)] + [page_spec(128, j) for j in range(group)]
                + [page_spec(128, j) for j in range(group)] + [pf_spec(j) for j in range(group)]
                + [pl.BlockSpec((1, group, 1, PKEYS), lambda s, p, pt: (s, p, 0, 0)),
                   pl.BlockSpec((1, 1, 1, PKEYS), lambda s, p, pt: (s, n_pages, 0, 0)),
                   samp((1, 1024)), samp((1, 512)), samp((1, 512)),
                   samp((N_HEADS, 256)), samp((N_HEADS, 128)), samp((N_HEADS, 128)), samp((1, PAGE_SIZE)),
                   pl.BlockSpec((None, 4, DK_A), lambda s, p, pt: (layer, 0, 0)),
                   pl.BlockSpec((None, H_A, 2 * DK_A), lambda s, p, pt: (layer, 0, 0))])
    out_specs = [samp((1, A_W)), samp((1, B_W)), samp((1, C_W))]
    small = lambda w: [pltpu.VMEM((SUBLANES, 1), F32), pltpu.VMEM((SUBLANES, 1), F32), pltpu.VMEM((SUBLANES, w), F32)]
    grid_spec = pltpu.PrefetchScalarGridSpec(
        num_scalar_prefetch=1, grid=(nsamp, n_steps), in_specs=in_specs, out_specs=out_specs,
        scratch_shapes=small(256) + small(128) + small(128) + [
            pltpu.VMEM((1, PKEYS), F32),
            pltpu.VMEM((PAGE_SIZE, 256), F32), pltpu.VMEM((PAGE_SIZE, 128), F32), pltpu.VMEM((PAGE_SIZE, 128), F32)])
    return pl.pallas_call(
        functools.partial(_attn_s_kernel, group=group, n_pages=n_pages, lam_init=lam_init),
        grid_spec=grid_spec,
        out_shape=[jax.ShapeDtypeStruct((nsamp, 1, A_W), BF16),
                   jax.ShapeDtypeStruct((nsamp, 1, B_W), BF16),
                   jax.ShapeDtypeStruct((nsamp, 1, C_W), BF16)],
        compiler_params=_cparams("arbitrary", "arbitrary"),
        name="attn_sample",
    )(page_table, *([ca] * group), *([cb] * group), *([cc] * group), *([pfx] * group),
      selb4, selb4, qa, qb, qc, na, nb, nc, lfn, wts["lam_p"], wts["g"])


def _prompt_layer(x2d, wts, layer, bsz, seq, lam_init):
    n = bsz * seq
    pr = _proj(x2d, wts, layer, _row_tile(n, 512))
    r3 = lambda a: a.reshape(bsz, seq, a.shape[-1])
    tq = _row_tile(seq, 256)
    oa = _attn_a(r3(pr["qa"]), r3(pr["na"]), wts, layer, lam_init, tq)
    ob = _attn_b(r3(pr["qb"]), r3(pr["nb"]), r3(pr["iq"]), r3(pr["krep"]), r3(pr["misc"]), tq)
    oc = _attn_c(r3(pr["qc"]), r3(pr["nc"]), r3(pr["misc"]), wts, layer, tq)
    x1, new_conv = _merge(x2d, oa.reshape(n, A_W), ob.reshape(n, B_W), oc.reshape(n, C_W), pr["dconv"],
                          None, wts, layer, _row_tile(seq, 256), seq)
    x2 = _ffn(x1, wts, layer, _row_tile(n, 256))
    news = (pr["na"].reshape(bsz, seq, H_A, 4 * DK_A), pr["nb"].reshape(bsz, seq, H_B, 2 * DH_B),
            pr["kidx"].reshape(bsz, seq, D_IDX), pr["nc"].reshape(bsz, seq, H_C, 2 * DH_C),
            pr["logf"].reshape(bsz, seq, H_C), new_conv)
    return x2, news


def _sample_layer(x2d, wts, layer, caches, page_base, state, page_table, lam_init):
    nsamp = x2d.shape[0]
    n_pages = page_table.shape[1]
    ca, cb, ckt, cc, pfx = caches
    tm = _row_tile(nsamp, 128)
    pr = _proj(x2d, wts, layer, tm)
    iq8 = pr["iq"].reshape(nsamp, H_IDX, D_IDX)
    wi8 = pr["misc"][:, MISC_WI:MISC_WI + H_IDX].reshape(nsamp, H_IDX, 1)
    scores = _idx_scores(page_table, ckt, page_base, iq8, wi8, pr["kidx"].reshape(nsamp, 1, D_IDX))
    topk = min(INDEX_TOPK_MAX, (n_pages * PAGE_SIZE + 1) // 4)
    selb4 = _select_bias(scores, wts["expand"], topk).reshape(nsamp, n_pages + 1, 1, PKEYS)
    lfn = jnp.pad(pr["logf"], ((0, 0), (0, PAGE_SIZE - H_C))).reshape(nsamp, 1, PAGE_SIZE)
    r3 = lambda a: a.reshape(nsamp, 1, a.shape[-1])
    r4 = lambda a: a.reshape(nsamp, N_HEADS, a.shape[-1] // N_HEADS)
    group = 4 if n_pages % 4 == 0 else 1
    oa, ob, oc = _attn_sample(page_table, page_base, ca, cb, cc, pfx, selb4,
                              r3(pr["qa"]), r3(pr["qb"]), r3(pr["qc"]), r4(pr["na"]), r4(pr["nb"]), r4(pr["nc"]),
                              lfn, wts, layer, lam_init, group)
    x1, conv2 = _merge(x2d, oa.reshape(nsamp, A_W), ob.reshape(nsamp, B_W), oc.reshape(nsamp, C_W),
                       pr["dconv"], state.reshape(nsamp, (CONV_K - 1) * CONV_W), wts, layer, tm, 1)
    x2 = _ffn(x1, wts, layer, tm)
    news = (pr["na"].reshape(nsamp, 1, H_A, 4 * DK_A), pr["nb"].reshape(nsamp, 1, H_B, 2 * DH_B),
            pr["kidx"].reshape(nsamp, 1, D_IDX), pr["nc"].reshape(nsamp, 1, H_C, 2 * DH_C),
            pr["logf"].reshape(nsamp, 1, H_C), conv2.reshape(nsamp, CONV_K - 1, CONV_W))
    return x2, news


def kernel(x_prompt, x_sample, cache_a_kv, cache_b_kv, cache_b_kidx, cache_c_kv, cache_c_logf, state_conv, page_table, w_in, b_fgate, lam_q1, lam_k1, lam_q2, lam_k2, g_diffnorm, conv_w, w_br_a, w_br_b, w_br_c, w_br_d, w_o, ln1_g, ln1_b, w_ffn_in, w_ffn_out, ln2_g, ln2_b):
    bsz, seq, _ = x_prompt.shape
    nsamp = x_sample.shape[0]
    depth = w_in.shape[0]
    bf = lambda a: a.astype(BF16)
    w_in_b = bf(w_in)
    wts = dict(
        w1=_rearranged_weights(w_in_b), wg=w_in_b[:, :, O_G:],
        aug_row=jnp.asarray(_build_q_aug_row()),
        bfg_row=jnp.zeros((depth, 1, LANES), F32).at[:, 0, MISC_CF:MISC_CF + H_C].set(b_fgate),
        lam_p=jnp.stack([lam_q1, lam_k1, lam_q2, lam_k2], axis=1),
        g=g_diffnorm, conv_w=conv_w, pa=bf(w_br_a), pb=bf(w_br_b), pc=bf(w_br_c), pd=bf(w_br_d), wo=bf(w_o),
        ln1_g=ln1_g[:, None, :], ln1_b=ln1_b[:, None, :], wf_in=bf(w_ffn_in), wf_out=bf(w_ffn_out),
        ln2_g=ln2_g[:, None, :], ln2_b=ln2_b[:, None, :],
        expand=jnp.asarray(_build_key_expansion(), dtype=BF16))
    n_pool = cache_a_kv.shape[1]
    all_pages = depth * n_pool
    pfx = _page_prefix(cache_c_logf.reshape(all_pages, PKEYS), jnp.asarray(_build_prefix_matrix(), dtype=BF16))
    caches = (cache_a_kv.reshape(all_pages, PAGE_SIZE, N_HEADS, 4 * DK_A),
              cache_b_kv.reshape(all_pages, PAGE_SIZE, N_HEADS, 2 * DH_B),
              jnp.swapaxes(cache_b_kidx, 2, 3).reshape(all_pages, D_IDX, PAGE_SIZE),
              cache_c_kv.reshape(all_pages, PAGE_SIZE, N_HEADS, 2 * DH_C),
              pfx.reshape(all_pages, 1, 2 * PKEYS))

    yp = x_prompt.reshape(bsz * seq, D_MODEL)
    ys = x_sample.reshape(nsamp, D_MODEL)
    news_p, news_s = [], []
    for l in range(depth):
        lam_init = 0.8 - 0.6 * math.exp(-0.3 * l)
        yp, new_p = _prompt_layer(yp, wts, l, bsz, seq, lam_init)
        ys, new_s = _sample_layer(ys, wts, l, caches, l * n_pool, state_conv[l], page_table, lam_init)
        news_p.append(new_p)
        news_s.append(new_s)

    def stack(lst, i):
        return jnp.stack([e[i] for e in lst])

    outs = [yp.reshape(bsz, seq, D_MODEL), ys.reshape(nsamp, 1, D_MODEL)]
    for i in range(6):
        outs.append(stack(news_p, i))
        outs.append(stack(news_s, i))
    return tuple(outs)
```

```python
import functools
import math

import numpy as np
import jax
import jax.numpy as jnp
from jax import lax
from jax.experimental import pallas as pl
from jax.experimental.pallas import tpu as pltpu

F32 = jnp.float32
BF16 = jnp.bfloat16
NEG_INF = float("-inf")
M_INIT = -1e30
INT_MIN = -2 ** 31
KEY_OF_NEG_INF = -2139095041

D_MODEL = 1024
DEPTH = 2
PAGE_SIZE = 128
H_A, DK_A = 4, 64
A_W = H_A * 2 * DK_A
H_B, DH_B = 4, 64
B_W = H_B * DH_B
H_IDX, D_IDX = 8, 32
INDEX_TOPK_MAX = 256
H_C, DH_C = 4, 64
C_W = H_C * DH_C
CONV_W, CONV_K = 256, 3
N_BRANCH = 4
D_FF = -(-8 * D_MODEL // (3 * 256)) * 256
ALPHA = (2 * DEPTH) ** 0.25
LN_EPS = 1e-5
NORM_EPS = 1e-6
N_HEADS = 4

_n = H_A + H_B
_S_ALL = 2.0 ** (-8.0 * (np.arange(_n) + 1) / _n)
SLOPES_A = [float(v) for v in _S_ALL[0::2]]
SLOPES_B = [float(v) for v in _S_ALL[1::2]]

LANES = 128
SUBLANES = 8
VMEM_LIMIT_BYTES = 56 * 1024 * 1024
PKEYS = PAGE_SIZE * N_HEADS

_WIDTHS = (A_W, A_W, A_W, B_W, B_W, B_W, H_IDX * D_IDX, D_IDX, H_IDX,
           C_W, C_W, C_W, H_C, CONV_W, CONV_W, CONV_W, N_BRANCH * D_MODEL)
_OFF = np.concatenate([[0], np.cumsum(_WIDTHS)]).astype(np.int64)
(O_AQ, O_AK, O_AV, O_BQ, O_BK, O_BV, O_BIQ, O_BIK, O_BIW, O_CQ, O_CK, O_CV, O_CF,
 O_DB, O_DC, O_DH, O_G, IN_WIDTH) = [int(v) for v in _OFF]

SEG = {}
_pos = 0
for _name, _w in (("qa", 1024), ("na", 1024), ("qb", 512), ("nb", 512), ("iq", 256), ("krep", 256),
                  ("qc", 512), ("nc", 512), ("dconv", 768), ("kidx", 128), ("misc", 128)):
    SEG[_name] = (_pos, _pos + _w)
    _pos += _w
W1_WIDTH = _pos
MISC_CF = 0
MISC_WI = 8
QA_AUG = 128
QBC_AUG = 64


def _build_w1_columns():
    src = -np.ones((W1_WIDTH,), np.int64)
    scale = np.ones((W1_WIDTH,), np.float32)
    s = SEG["qa"][0]
    for h in range(H_A):
        src[s + h * 256: s + h * 256 + 128] = O_AQ + h * 128 + np.arange(128)
    scale[SEG["qa"][0]:SEG["qa"][1]] = DK_A ** -0.5
    s = SEG["na"][0]
    for h in range(H_A):
        src[s + h * 256: s + h * 256 + 128] = O_AK + h * 128 + np.arange(128)
        src[s + h * 256 + 128: s + (h + 1) * 256] = O_AV + h * 128 + np.arange(128)
    for nm, oq, ok, ov in (("b", O_BQ, O_BK, O_BV), ("c", O_CQ, O_CK, O_CV)):
        s = SEG["q" + nm][0]
        for h in range(4):
            src[s + h * 128: s + h * 128 + 64] = oq + h * 64 + np.arange(64)
        scale[SEG["q" + nm][0]:SEG["q" + nm][1]] = 64 ** -0.5
        s = SEG["n" + nm][0]
        for h in range(4):
            src[s + h * 128: s + h * 128 + 64] = ok + h * 64 + np.arange(64)
            src[s + h * 128 + 64: s + (h + 1) * 128] = ov + h * 64 + np.arange(64)
    s = SEG["iq"][0]
    src[s:s + 256] = O_BIQ + np.arange(256)
    s = SEG["krep"][0]
    for h in range(H_IDX):
        src[s + h * 32: s + (h + 1) * 32] = O_BIK + np.arange(32)
    s = SEG["dconv"][0]
    src[s:s + 768] = O_DB + np.arange(768)
    s = SEG["kidx"][0]
    src[s:s + 32] = O_BIK + np.arange(32)
    s = SEG["misc"][0]
    src[s + MISC_CF: s + MISC_CF + H_C] = O_CF + np.arange(H_C)
    src[s + MISC_WI: s + MISC_WI + H_IDX] = O_BIW + np.arange(H_IDX)
    scale[s + MISC_WI: s + MISC_WI + H_IDX] = H_IDX ** -0.5 * D_IDX ** -0.5
    return src, scale


_W1_SRC, _W1_SCALE = _build_w1_columns()


def _w1_runs():
    runs = []
    i = 0
    while i < W1_WIDTH:
        j = i + 1
        while (j < W1_WIDTH and _W1_SCALE[j] == _W1_SCALE[i]
               and ((_W1_SRC[i] < 0 and _W1_SRC[j] < 0)
                    or (_W1_SRC[i] >= 0 and _W1_SRC[j] == _W1_SRC[i] + (j - i)))):
            j += 1
        runs.append((int(_W1_SRC[i]), j - i, float(_W1_SCALE[i])))
        i = j
    return runs


def _rearranged_weights(w_in_bf16):
    parts = []
    for start, width, scale in _w1_runs():
        if start < 0:
            parts.append(jnp.zeros(w_in_bf16.shape[:2] + (width,), BF16))
        else:
            piece = w_in_bf16[:, :, start:start + width]
            parts.append(piece if scale == 1.0 else piece * jnp.asarray(scale, BF16))
    return jnp.concatenate(parts, axis=2)


def _build_q_aug_row():
    row = np.zeros((1, W1_WIDTH), np.float32)
    for h in range(4):
        a = SEG["qa"][0] + h * 256 + QA_AUG
        row[0, a], row[0, a + 1] = SLOPES_A[h], SLOPES_A[h] * 256.0
        b = SEG["qb"][0] + h * 128 + QBC_AUG
        row[0, b], row[0, b + 1] = SLOPES_B[h], SLOPES_B[h] * 256.0
        c = SEG["qc"][0] + h * 128 + QBC_AUG
        row[0, c:c + 3] = 1.0
    return row


def _cparams(*sem):
    return pltpu.CompilerParams(dimension_semantics=sem, vmem_limit_bytes=VMEM_LIMIT_BYTES)


def _const_spec(shape):
    nd = len(shape)
    return pl.BlockSpec(shape, lambda *_: (0,) * nd, pipeline_mode=pl.Buffered(1))


def _layer_spec(shape, layer):
    nd = len(shape)
    return pl.BlockSpec((None,) + tuple(shape), lambda *_: (layer,) + (0,) * nd,
                        pipeline_mode=pl.Buffered(1))


def _dot(a, b):
    return jnp.dot(a, b, preferred_element_type=F32)


def _dot_nt(a, b):
    return lax.dot_general(a, b, (((1,), (1,)), ((), ())), preferred_element_type=F32)


def _row_tile(n, pref):
    t = min(n, pref)
    while n % t:
        t //= 2
    return t


def _split3(x):
    hi = x.astype(BF16)
    r = x - hi.astype(F32)
    mid = r.astype(BF16)
    lo = (r - mid.astype(F32)).astype(BF16)
    return hi, mid, lo


def _log_sigmoid(x):
    return jnp.minimum(x, 0.0) - jnp.log1p(jnp.exp(-jnp.abs(x)))


def _layernorm(h, g, b):
    mu = jnp.mean(h, axis=-1, keepdims=True)
    d = h - mu
    var = jnp.mean(d * d, axis=-1, keepdims=True)
    return d * lax.rsqrt(var + LN_EPS) * g + b


def _lambda(lam_ref, lam_init):
    lam_p = lam_ref[...]
    return (jnp.exp(jnp.sum(lam_p[0:1] * lam_p[1:2], axis=1, keepdims=True))
            - jnp.exp(jnp.sum(lam_p[2:3] * lam_p[3:4], axis=1, keepdims=True)) + lam_init)


def _for_causal_extent(qi, tq, seq, body):
    for c in range(seq // tq):
        pl.when(qi == c)(functools.partial(body, (c + 1) * tq))


def _diag_mask(tq):
    return (lax.broadcasted_iota(jnp.int32, (tq, tq), 1) <= lax.broadcasted_iota(jnp.int32, (tq, tq), 0))


def _mask_diag(s, diag):
    tq = diag.shape[0]
    width = s.shape[1]
    tail = jnp.where(diag, s[:, width - tq:], NEG_INF)
    return tail if width == tq else jnp.concatenate([s[:, :width - tq], tail], axis=1)


def _proj_kernel(*refs, n_alias, leaves):
    x_ref, w_ref, aug_ref, bfg_ref = refs[:4]
    (qa_ref, na_ref, qb_ref, nb_ref, iq_ref, kr_ref, qc_ref, nc_ref, dc_ref, misc_ref, kidx_ref,
     logf_ref) = refs[4 + n_alias:16 + n_alias]
    leaf_refs = refs[16 + n_alias:]
    xb = x_ref[...].astype(BF16)

    def seg(name):
        lo, hi = SEG[name]
        return _dot(xb, w_ref[:, lo:hi])

    def qseg(name):
        lo, hi = SEG[name]
        return (seg(name) + aug_ref[:, lo:hi]).astype(BF16)

    news = {"na": seg("na"), "nb": seg("nb"), "nc": seg("nc")}
    qa_ref[...] = qseg("qa")
    na_ref[...] = news["na"]
    qb_ref[...] = qseg("qb")
    nb_ref[...] = news["nb"]
    iq_ref[...] = seg("iq").astype(BF16)
    kr_ref[...] = seg("krep").astype(BF16)
    qc_ref[...] = qseg("qc")
    nc_ref[...] = news["nc"]
    for name, ref in zip(leaves, leaf_refs):
        v = news[name]
        ref[...] = v.reshape(v.shape[0], N_HEADS, v.shape[1] // N_HEADS)
    dc_ref[...] = seg("dconv")
    kidx_ref[...] = seg("kidx")[:, :D_IDX]
    misc = seg("misc")
    misc_ref[...] = misc
    logf_ref[...] = _log_sigmoid(misc + bfg_ref[...])[:, MISC_CF:MISC_CF + H_C]


LEAF_NAMES = ("na", "nb", "nc")


def _proj(x2d, wts, layer, tm, leaf_bufs=None, depth=None):
    n = x2d.shape[0]
    widths = dict((k, v[1] - v[0]) for k, v in SEG.items())
    outs = [("qa", BF16, widths["qa"]), ("na", F32, widths["na"]), ("qb", BF16, widths["qb"]),
            ("nb", F32, widths["nb"]), ("iq", BF16, widths["iq"]), ("krep", BF16, widths["krep"]),
            ("qc", BF16, widths["qc"]), ("nc", F32, widths["nc"]), ("dconv", F32, widths["dconv"]),
            ("misc", F32, widths["misc"]), ("kidx", F32, D_IDX), ("logf", F32, H_C)]
    in_specs = [pl.BlockSpec((tm, D_MODEL), lambda i: (i, 0)),
                _layer_spec((D_MODEL, W1_WIDTH), layer),
                _const_spec((1, W1_WIDTH)),
                _layer_spec((1, LANES), layer)]
    args = [x2d, wts["w1"], wts["aug_row"], wts["bfg_row"]]
    out_specs = [pl.BlockSpec((tm, w), lambda i: (i, 0)) for _, _, w in outs]
    out_shape = [jax.ShapeDtypeStruct((n, w), dt) for _, dt, w in outs]
    leaves, aliases, n_alias = (), {}, 0
    if leaf_bufs is not None:
        leaves = LEAF_NAMES
        for k, name in enumerate(leaves):
            hw = widths[name] // N_HEADS
            out_specs.append(pl.BlockSpec((None, tm, N_HEADS, hw), lambda i: (layer, i, 0, 0)))
            out_shape.append(jax.ShapeDtypeStruct((depth, n, N_HEADS, hw), F32))
            if leaf_bufs != "new":
                in_specs.append(pl.BlockSpec(memory_space=pl.ANY))
                args.append(leaf_bufs[k])
                aliases[len(args) - 1] = len(outs) + k
        n_alias = len(aliases)
    res = pl.pallas_call(
        functools.partial(_proj_kernel, n_alias=n_alias, leaves=leaves),
        grid=(n // tm,),
        in_specs=in_specs, out_specs=out_specs, out_shape=out_shape,
        input_output_aliases=aliases,
        compiler_params=_cparams("parallel"),
        name="proj",
    )(*args)
    out = dict(zip([o[0] for o in outs], res[:len(outs)]))
    out["leaf_bufs"] = tuple(res[len(outs):])
    return out


def _topk_select(score, k):
    rows, width = score.shape
    bits = lax.bitcast_convert_type(score, jnp.int32)
    key = jnp.where(bits < 0, bits ^ jnp.int32(0x7FFFFFFF), bits)
    kf = jnp.float32(k)

    def count(mask):
        return jnp.sum(jnp.where(mask, 1.0, 0.0), axis=1, keepdims=True)

    t0 = jnp.where(count(key >= 0) >= kf, jnp.int32(0), jnp.int32(INT_MIN))

    def vbody(i, t):
        cand = t + jnp.left_shift(jnp.int32(1), jnp.int32(30) - i)
        return jnp.where(count(key >= cand) >= kf, cand, t)

    t = lax.fori_loop(0, 31, vbody, t0)
    gt = key > t
    eq = key == t
    n_gt = count(gt)
    need = kf - n_gt
    idx = lax.broadcasted_iota(jnp.int32, (rows, width), 1)
    nbits = max(1, int(math.ceil(math.log2(width))))

    surplus = jnp.where(t > KEY_OF_NEG_INF, n_gt + count(eq) - kf, 0.0)
    tied = jnp.max(surplus) > 0.0

    def tie_cut():
        def ibody(i, j):
            cand = j + jnp.left_shift(jnp.int32(1), jnp.int32(nbits - 1) - i)
            return jnp.where(count(eq & (idx < cand)) < need, cand, j)

        return lax.fori_loop(0, nbits, ibody, jnp.zeros((rows, 1), jnp.int32))

    j = lax.cond(tied, tie_cut, lambda: jnp.full((rows, 1), width, jnp.int32))
    return gt | (eq & (idx <= j))


def _attn_a_kernel(qa_ref, na_ref, lam_ref, g_ref, o_ref, k1_s, k2_s, v_s, *, tq, seq, lam_init):
    qi = pl.program_id(1)

    @pl.when(qi == 0)
    def _():
        lane = lax.broadcasted_iota(jnp.int32, (seq, LANES), 1)
        kp = lax.broadcasted_iota(jnp.int32, (seq, LANES), 0)
        kaug = jnp.where(lane == 0, (kp & 255).astype(F32),
                         jnp.where(lane == 1, (kp >> 8).astype(F32), 0.0)).astype(BF16)
        for h in range(H_A):
            kk = na_ref[0, :, h * 256:h * 256 + 128]
            k1_s[h, :, 0:LANES] = jnp.where(lane < 64, kk, 0.0).astype(BF16)
            k1_s[h, :, LANES:2 * LANES] = kaug
            k2_s[h, :, 0:LANES] = jnp.where(lane >= 64, kk, 0.0).astype(BF16)
            k2_s[h, :, LANES:2 * LANES] = kaug
            v_s[h] = na_ref[0, :, h * 256 + 128:(h + 1) * 256].astype(BF16)

    lam = _lambda(lam_ref, lam_init)

    def body(width):
        diag = _diag_mask(tq)
        for h in range(H_A):
            qh = qa_ref[0, :, h * 256:(h + 1) * 256]
            s1 = _mask_diag(_dot_nt(qh, k1_s[h, 0:width, :]), diag)
            s2 = _mask_diag(_dot_nt(qh, k2_s[h, 0:width, :]), diag)
            e1 = jnp.exp(s1 - jnp.max(s1, axis=1, keepdims=True))
            e2 = jnp.exp(s2 - jnp.max(s2, axis=1, keepdims=True))
            r1 = 1.0 / jnp.sum(e1, axis=1, keepdims=True)
            r2 = lam / jnp.sum(e2, axis=1, keepdims=True)
            p = (e1 * r1 - e2 * r2).astype(BF16)
            o = _dot(p, v_s[h, 0:width, :])
            o = o * lax.rsqrt(jnp.mean(o * o, axis=1, keepdims=True) + NORM_EPS)
            o = o * g_ref[h:h + 1, :] * (1.0 - lam_init)
            o_ref[0, :, h * 128:(h + 1) * 128] = o.astype(BF16)

    _for_causal_extent(qi, tq, seq, body)


def _attn_a(qa, na, wts, layer, lam_init, tq):
    bsz, seq, _ = qa.shape
    return pl.pallas_call(
        functools.partial(_attn_a_kernel, tq=tq, seq=seq, lam_init=lam_init),
        grid=(bsz, seq // tq),
        in_specs=[pl.BlockSpec((1, tq, 1024), lambda b, i: (b, i, 0)),
                  pl.BlockSpec((1, seq, 1024), lambda b, i: (b, 0, 0)),
                  _layer_spec((4, DK_A), layer),
                  _layer_spec((H_A, 2 * DK_A), layer)],
        out_specs=pl.BlockSpec((1, tq, A_W), lambda b, i: (b, i, 0)),
        out_shape=jax.ShapeDtypeStruct((bsz, seq, A_W), BF16),
        scratch_shapes=[pltpu.VMEM((H_A, seq, 2 * LANES), BF16), pltpu.VMEM((H_A, seq, 2 * LANES), BF16),
                        pltpu.VMEM((H_A, seq, LANES), BF16)],
        compiler_params=_cparams("parallel", "arbitrary"),
        name="attn_a",
    )(qa, na, wts["lam_p"], wts["g"])


def _attn_c_kernel(qc_ref, nc_ref, misc_ref, bfg_ref, o_ref, k_s, v_s, *, tq, seq):
    qi = pl.program_id(1)

    @pl.when(qi == 0)
    def _():
        lane = lax.broadcasted_iota(jnp.int32, (seq, LANES), 1)
        ch = min(256, seq)
        tri = jnp.where(lax.broadcasted_iota(jnp.int32, (ch, ch), 0)
                        >= lax.broadcasted_iota(jnp.int32, (ch, ch), 1), 1.0, 0.0).astype(BF16)
        carry = jnp.zeros((1, LANES), F32)
        chunks = []
        for c in range(seq // ch):
            lf = _log_sigmoid(misc_ref[0, c * ch:(c + 1) * ch, :] + bfg_ref[...])
            hi, mid, lo = _split3(lf)
            fc = (_dot(tri, hi) + _dot(tri, mid)) + _dot(tri, lo) + carry
            carry = fc[ch - 1:ch, :]
            chunks.append(fc)
        fcum = jnp.concatenate(chunks, axis=0) if len(chunks) > 1 else chunks[0]
        for h in range(H_C):
            fh = jnp.broadcast_to(fcum[:, MISC_CF + h:MISC_CF + h + 1], (seq, LANES))
            hi, mid, lo = (v.astype(F32) for v in _split3(-fh))
            kv = nc_ref[0, :, h * 128:(h + 1) * 128]
            aug = jnp.where(lane == QBC_AUG, hi,
                            jnp.where(lane == QBC_AUG + 1, mid, jnp.where(lane == QBC_AUG + 2, lo, 0.0)))
            k_s[h] = jnp.where(lane < 64, kv, aug).astype(BF16)
            v_s[h] = kv.astype(BF16)

    def body(width):
        diag = _diag_mask(tq)
        for h in range(H_C):
            qh = qc_ref[0, :, h * 128:(h + 1) * 128]
            s = _mask_diag(_dot_nt(qh, k_s[h, 0:width, :]), diag)
            e = jnp.exp(s - jnp.max(s, axis=1, keepdims=True))
            r = 1.0 / jnp.sum(e, axis=1, keepdims=True)
            o = _dot(e.astype(BF16), v_s[h, 0:width, :]) * r
            o_ref[0, :, h * 64:(h + 1) * 64] = o[:, 64:128].astype(BF16)

    _for_causal_extent(qi, tq, seq, body)


def _attn_c(qc, nc, misc, wts, layer, tq):
    bsz, seq, _ = qc.shape
    return pl.pallas_call(
        functools.partial(_attn_c_kernel, tq=tq, seq=seq),
        grid=(bsz, seq // tq),
        in_specs=[pl.BlockSpec((1, tq, 512), lambda b, i: (b, i, 0)),
                  pl.BlockSpec((1, seq, 512), lambda b, i: (b, 0, 0)),
                  pl.BlockSpec((1, seq, LANES), lambda b, i: (b, 0, 0)),
                  _layer_spec((1, LANES), layer)],
        out_specs=pl.BlockSpec((1, tq, C_W), lambda b, i: (b, i, 0)),
        out_shape=jax.ShapeDtypeStruct((bsz, seq, C_W), BF16),
        scratch_shapes=[pltpu.VMEM((H_C, seq, LANES), BF16)] * 2,
        compiler_params=_cparams("parallel", "arbitrary"),
        name="attn_c",
    )(qc, nc, misc, wts["bfg_row"])


def _attn_b_kernel(qb_ref, nb_ref, iq_ref, kr_ref, misc_ref, o_ref, k_s, v_s, kr_s, *, tq, seq, topk):
    qi = pl.program_id(1)

    @pl.when(qi == 0)
    def _():
        lane = lax.broadcasted_iota(jnp.int32, (seq, LANES), 1)
        kp = lax.broadcasted_iota(jnp.int32, (seq, LANES), 0)
        aug = jnp.where(lane == QBC_AUG, (kp & 255).astype(F32),
                        jnp.where(lane == QBC_AUG + 1, (kp >> 8).astype(F32), 0.0))
        for h in range(H_B):
            kv = nb_ref[0, :, h * 128:(h + 1) * 128]
            k_s[h] = jnp.where(lane < 64, kv, aug).astype(BF16)
            v_s[h] = kv.astype(BF16)
        kr = kr_ref[0]
        klane = lax.broadcasted_iota(jnp.int32, (seq, 256), 1)
        for h in range(H_IDX):
            kr_s[h] = kr * jnp.where((klane >> 5) == h, 1.0, 0.0).astype(BF16)

    def body(width):
        diag = _diag_mask(tq)
        iq = iq_ref[0]
        score = None
        for h in range(H_IDX):
            rel = jnp.maximum(_dot_nt(iq, kr_s[h, 0:width, :]), 0.0) * misc_ref[0, :, MISC_WI + h:MISC_WI + h + 1]
            score = rel if score is None else score + rel
        sel = _topk_select(_mask_diag(score, diag), topk)
        bias = _mask_diag(jnp.where(sel, 0.0, NEG_INF), diag)
        for h in range(H_B):
            qh = qb_ref[0, :, h * 128:(h + 1) * 128]
            s = _dot_nt(qh, k_s[h, 0:width, :]) + bias
            e = jnp.exp(s - jnp.max(s, axis=1, keepdims=True))
            r = 1.0 / jnp.sum(e, axis=1, keepdims=True)
            o = _dot(e.astype(BF16), v_s[h, 0:width, :]) * r
            o_ref[0, :, h * 64:(h + 1) * 64] = o[:, 64:128].astype(BF16)

    _for_causal_extent(qi, tq, seq, body)


def _attn_b(qb, nb, iq, krep, misc, tq):
    bsz, seq, _ = qb.shape
    topk = min(INDEX_TOPK_MAX, seq // 4)
    return pl.pallas_call(
        functools.partial(_attn_b_kernel, tq=tq, seq=seq, topk=topk),
        grid=(bsz, seq // tq),
        in_specs=[pl.BlockSpec((1, tq, 512), lambda b, i: (b, i, 0)),
                  pl.BlockSpec((1, seq, 512), lambda b, i: (b, 0, 0)),
                  pl.BlockSpec((1, tq, 256), lambda b, i: (b, i, 0)),
                  pl.BlockSpec((1, seq, 256), lambda b, i: (b, 0, 0)),
                  pl.BlockSpec((1, tq, LANES), lambda b, i: (b, i, 0))],
        out_specs=pl.BlockSpec((1, tq, B_W), lambda b, i: (b, i, 0)),
        out_shape=jax.ShapeDtypeStruct((bsz, seq, B_W), BF16),
        scratch_shapes=[pltpu.VMEM((H_B, seq, LANES), BF16), pltpu.VMEM((H_B, seq, LANES), BF16),
                        pltpu.VMEM((H_IDX, seq, H_IDX * D_IDX), BF16)],
        compiler_params=_cparams("parallel", "arbitrary"),
        name="attn_b",
    )(qb, nb, iq, krep, misc)


def _merge_kernel(*refs, tm, sample, tiles_per_seq):
    if sample:
        (x_ref, oa_ref, ob_ref, oc_ref, dc_ref, st_ref, wg_ref, pa_ref, pb_ref, pc_ref, pd_ref,
         wo_ref, g_ref, b_ref, cw_ref, y_ref, conv_ref) = refs
    else:
        (x_ref, oa_ref, ob_ref, oc_ref, dc_ref, halo_ref, wg_ref, pa_ref, pb_ref, pc_ref, pd_ref,
         wo_ref, g_ref, b_ref, cw_ref, y_ref, conv_ref, uu_ref) = refs
    w0 = cw_ref[0:1, :]
    w1 = cw_ref[1:2, :]
    w2 = cw_ref[2:3, :]
    u = dc_ref[:, 256:512] * dc_ref[:, 512:768]
    if sample:
        s0 = st_ref[:, 0:256]
        s1 = st_ref[:, 256:512]
        y = s0 * w0 + s1 * w1 + u * w2
        conv_ref[:, 0:256] = s1
        conv_ref[:, 256:512] = u
    else:
        first = (pl.program_id(0) % tiles_per_seq) == 0
        uh = halo_ref[:, 256:512] * halo_ref[:, 512:768]
        uu_ref[0:SUBLANES, :] = jnp.where(first, 0.0, uh)
        uu_ref[SUBLANES:SUBLANES + tm, :] = u
        y = uu_ref[SUBLANES - 2:SUBLANES - 2 + tm, :] * w0 + uu_ref[SUBLANES - 1:SUBLANES - 1 + tm, :] * w1 + u * w2
        conv_ref[0] = uu_ref[SUBLANES + tm - 2:SUBLANES + tm, :]
    o_d = (dc_ref[:, 0:256] * y).astype(BF16)
    xv = x_ref[...]
    xb = xv.astype(BF16)
    mixed = None
    for i, (o, p_ref) in enumerate(((oa_ref[...], pa_ref), (ob_ref[...], pb_ref),
                                    (oc_ref[...], pc_ref), (o_d, pd_ref))):
        gate = jax.nn.sigmoid(_dot(xb, wg_ref[:, i * D_MODEL:(i + 1) * D_MODEL]))
        term = gate * _dot(o, p_ref[...])
        mixed = term if mixed is None else mixed + term
    h = ALPHA * xv + _dot(mixed.astype(BF16), wo_ref[...])
    y_ref[...] = _layernorm(h, g_ref[...], b_ref[...])


def _merge(x2d, oa, ob, oc, dconv, state2d, wts, layer, tm, seq):
    n = x2d.shape[0]
    sample = state2d is not None
    row = lambda w: pl.BlockSpec((tm, w), lambda i: (i, 0))
    in_specs = [row(D_MODEL), row(A_W), row(B_W), row(C_W), row(768)]
    args = [x2d, oa, ob, oc, dconv]
    if sample:
        in_specs.append(row(512))
        args.append(state2d)
        out_specs = [row(D_MODEL), row(512)]
        out_shape = [jax.ShapeDtypeStruct((n, D_MODEL), F32), jax.ShapeDtypeStruct((n, 512), F32)]
        scratch = []
        tiles_per_seq = 1
    else:
        tiles_per_seq = seq // tm
        r8 = tm // SUBLANES
        in_specs.append(pl.BlockSpec((SUBLANES, 768), lambda i: (jnp.maximum(i * r8 - 1, 0), 0)))
        args.append(dconv)
        out_specs = [row(D_MODEL), pl.BlockSpec((1, CONV_K - 1, CONV_W), lambda i: (i // tiles_per_seq, 0, 0))]
        out_shape = [jax.ShapeDtypeStruct((n, D_MODEL), F32),
                     jax.ShapeDtypeStruct((n // seq, CONV_K - 1, CONV_W), F32)]
        scratch = [pltpu.VMEM((tm + SUBLANES, CONV_W), F32)]
    for name in ("wg", "pa", "pb", "pc", "pd", "wo", "ln1_g", "ln1_b", "conv_w"):
        in_specs.append(_layer_spec(wts[name].shape[1:], layer))
        args.append(wts[name])
    return pl.pallas_call(
        functools.partial(_merge_kernel, tm=tm, sample=sample, tiles_per_seq=tiles_per_seq),
        grid=(n // tm,),
        in_specs=in_specs, out_specs=out_specs, out_shape=out_shape, scratch_shapes=scratch,
        compiler_params=_cparams("arbitrary"),
        name="merge_s" if sample else "merge",
    )(*args)


def _ffn_kernel(x_ref, wi_ref, wo_ref, g_ref, b_ref, y_ref):
    xv = x_ref[...]
    xb = xv.astype(BF16)
    hg = _dot(xb, wi_ref[:, 0:D_FF])
    hu = _dot(xb, wi_ref[:, D_FF:2 * D_FF])
    act = (hg * jax.nn.sigmoid(hg) * hu).astype(BF16)
    h = ALPHA * xv + _dot(act, wo_ref[...])
    y_ref[...] = _layernorm(h, g_ref[...], b_ref[...])


def _ffn(x2d, wts, layer, tm):
    n = x2d.shape[0]
    return pl.pallas_call(
        _ffn_kernel,
        grid=(n // tm,),
        in_specs=[pl.BlockSpec((tm, D_MODEL), lambda i: (i, 0)),
                  _layer_spec((D_MODEL, 2 * D_FF), layer), _layer_spec((D_FF, D_MODEL), layer),
                  _layer_spec((1, D_MODEL), layer), _layer_spec((1, D_MODEL), layer)],
        out_specs=pl.BlockSpec((tm, D_MODEL), lambda i: (i, 0)),
        out_shape=jax.ShapeDtypeStruct((n, D_MODEL), F32),
        compiler_params=_cparams("parallel"),
        name="ffn",
    )(x2d, wts["wf_in"], wts["wf_out"], wts["ln2_g"], wts["ln2_b"])


def _build_prefix_matrix():
    c = np.arange(PKEYS)
    same_head = (c[:, None] % N_HEADS) == (c[None, :] % N_HEADS)
    upto = (c[:, None] // N_HEADS) <= (c[None, :] // N_HEADS)
    return np.concatenate([same_head & upto, same_head], axis=1).astype(np.float32)


def _build_key_expansion():
    return (np.arange(PAGE_SIZE)[:, None] == (np.arange(PKEYS)[None, :] // N_HEADS)).astype(np.float32)


def _pfx_kernel(x_ref, u_ref, o_ref):
    hi, mid, lo = _split3(x_ref[...])
    u = u_ref[...]
    o_ref[...] = (_dot(hi, u) + _dot(mid, u)) + _dot(lo, u)


def _page_prefix(logf_pages, umat):
    n_pool = logf_pages.shape[0]
    tp = _row_tile(n_pool, 512) if n_pool % SUBLANES == 0 else n_pool
    return pl.pallas_call(
        _pfx_kernel,
        grid=(n_pool // tp,),
        in_specs=[pl.BlockSpec((tp, PKEYS), lambda i: (i, 0)),
                  _const_spec(umat.shape)],
        out_specs=pl.BlockSpec((tp, 2 * PKEYS), lambda i: (i, 0)),
        out_shape=jax.ShapeDtypeStruct((n_pool, 2 * PKEYS), F32),
        compiler_params=_cparams("parallel"),
        name="page_prefix",
    )(logf_pages, umat)


def _idx_kernel(pt_ref, *refs, n_pages):
    pages = refs[:n_pages]
    iq_ref, wi_ref, knew_ref, o_ref = refs[n_pages:]
    iq = iq_ref[0]
    wi = wi_ref[0]

    def weighted(rel):
        return jnp.sum(jnp.maximum(rel, 0.0) * wi, axis=0, keepdims=True)

    for j in range(n_pages):
        o_ref[0, :, j * PAGE_SIZE:(j + 1) * PAGE_SIZE] = weighted(_dot(iq, pages[j][0].astype(BF16)))
    knew = jnp.broadcast_to(knew_ref[0], (PAGE_SIZE, D_IDX)).astype(BF16)
    lane = lax.broadcasted_iota(jnp.int32, (1, PAGE_SIZE), 1)
    o_ref[0, :, n_pages * PAGE_SIZE:(n_pages + 1) * PAGE_SIZE] = jnp.where(
        lane == 0, weighted(_dot_nt(iq, knew)), NEG_INF)


def _idx_scores(page_table, kidx_t, page_base, iq8, wi8, knew):
    nsamp, n_pages = page_table.shape
    width = (n_pages + 1) * PAGE_SIZE

    def page_spec(j):
        return pl.BlockSpec((1, D_IDX, PAGE_SIZE), lambda s, pt: (page_base + pt[s, j], 0, 0))

    grid_spec = pltpu.PrefetchScalarGridSpec(
        num_scalar_prefetch=1,
        grid=(nsamp,),
        in_specs=[page_spec(j) for j in range(n_pages)] + [
            pl.BlockSpec((1, H_IDX, D_IDX), lambda s, pt: (s, 0, 0)),
            pl.BlockSpec((1, H_IDX, 1), lambda s, pt: (s, 0, 0)),
            pl.BlockSpec((1, 1, D_IDX), lambda s, pt: (s, 0, 0))],
        out_specs=pl.BlockSpec((1, 1, width), lambda s, pt: (s, 0, 0)),
    )
    out = pl.pallas_call(
        functools.partial(_idx_kernel, n_pages=n_pages),
        grid_spec=grid_spec,
        out_shape=jax.ShapeDtypeStruct((nsamp, 1, width), F32),
        compiler_params=_cparams("arbitrary"),
        name="idx_scores",
    )(page_table, *([kidx_t] * n_pages), iq8, wi8, knew)
    return out.reshape(nsamp, width)


def _sel_kernel(s_ref, e_ref, o_ref, *, topk, n_blocks):
    sel = jnp.where(_topk_select(s_ref[...], topk), 1.0, 0.0).astype(BF16)
    for j in range(n_blocks):
        rep = _dot(sel[:, j * PAGE_SIZE:(j + 1) * PAGE_SIZE], e_ref[...])
        o_ref[:, j * PKEYS:(j + 1) * PKEYS] = jnp.where(rep > 0.5, 0.0, NEG_INF)


def _select_bias(scores, expand, topk):
    nsamp, width = scores.shape
    n_blocks = width // PAGE_SIZE
    return pl.pallas_call(
        functools.partial(_sel_kernel, topk=topk, n_blocks=n_blocks),
        out_shape=jax.ShapeDtypeStruct((nsamp, n_blocks * PKEYS), F32),
        compiler_params=pltpu.CompilerParams(vmem_limit_bytes=VMEM_LIMIT_BYTES),
        name="select_bias",
    )(scores, expand)


def _attn_s_kernel(pt_ref, *refs, group, n_pages, lam_init):
    g = group
    ca = refs[0:g]
    cb = refs[g:2 * g]
    cc = refs[2 * g:3 * g]
    pf = refs[3 * g:4 * g]
    (selp_ref, seln_ref, qa_ref, qb_ref, qc_ref, na_ref, nb_ref, nc_ref, lfn_ref, lam_ref, g_ref,
     oa_ref, ob_ref, oc_ref,
     ma_s, la_s, acca_s, mb_s, lb_s, accb_s, mc_s, lc_s, accc_s, carry_s, xa_s, xb_s, xc_s) = refs[4 * g:]
    p = pl.program_id(1)
    n_steps = n_pages // g

    @pl.when(p == 0)
    def _():
        for m_s, l_s, acc_s in ((ma_s, la_s, acca_s), (mb_s, lb_s, accb_s), (mc_s, lc_s, accc_s)):
            m_s[...] = jnp.full(m_s.shape, M_INIT, F32)
            l_s[...] = jnp.zeros(l_s.shape, F32)
            acc_s[...] = jnp.zeros(acc_s.shape, F32)
        carry_s[...] = jnp.zeros(carry_s.shape, F32)

    sub_a = lax.broadcasted_iota(jnp.int32, (SUBLANES, 256), 0)
    lane_a = lax.broadcasted_iota(jnp.int32, (SUBLANES, 256), 1)
    qa_row = qa_ref[0].astype(F32)
    qa_blk = jnp.zeros((SUBLANES, 256), F32)
    for h in range(N_HEADS):
        qa_blk = jnp.where((sub_a >> 1) == h, jnp.broadcast_to(qa_row[:, h * 256:(h + 1) * 256], (SUBLANES, 256)),
                           qa_blk)
    qa_blk = jnp.where((lane_a < 128) & ((lane_a >> 6) == (sub_a & 1)), qa_blk, 0.0).astype(BF16)
    sub_b = lax.broadcasted_iota(jnp.int32, (SUBLANES, LANES), 0)
    lane_b = lax.broadcasted_iota(jnp.int32, (SUBLANES, LANES), 1)

    def head_block(row):
        blk = jnp.zeros((SUBLANES, LANES), F32)
        for h in range(N_HEADS):
            blk = jnp.where(sub_b == h, jnp.broadcast_to(row[:, h * 128:(h + 1) * 128], (SUBLANES, LANES)), blk)
        return jnp.where(lane_b < 64, blk, 0.0).astype(BF16)

    qb_blk = head_block(qb_ref[0].astype(F32))
    qc_blk = head_block(qc_ref[0].astype(F32))

    sub1 = lax.broadcasted_iota(jnp.int32, (SUBLANES, 1), 0)
    slope_a = jnp.zeros((SUBLANES, 1), F32)
    slope_b = jnp.zeros((SUBLANES, 1), F32)
    for h in range(N_HEADS):
        slope_a = jnp.where((sub1 >> 1) == h, SLOPES_A[h], slope_a)
        slope_b = jnp.where(sub1 == h, SLOPES_B[h], slope_b)

    def head_masks(width):
        sub = lax.broadcasted_iota(jnp.int32, (SUBLANES, width), 0)
        lane = lax.broadcasted_iota(jnp.int32, (SUBLANES, width), 1)
        own_a = jnp.where((lane & 3) == (sub >> 1), 0.0, NEG_INF)
        own_b = jnp.where((lane & 3) == sub, 0.0, NEG_INF)
        return own_a, own_b

    def update(qblk, pages, bias, m_s, l_s, acc_s):
        n = pages[0].shape[0]
        s = jnp.concatenate([_dot_nt(qblk, pg) for pg in pages], axis=1) + bias
        m_old = m_s[...]
        m_new = jnp.maximum(m_old, jnp.max(s, axis=1, keepdims=True))
        alpha = jnp.exp(m_old - m_new)
        e = jnp.exp(s - m_new)
        l_s[...] = alpha * l_s[...] + jnp.sum(e, axis=1, keepdims=True)
        eb = e.astype(BF16)
        pv = None
        for j, pg in enumerate(pages):
            t = _dot(eb[:, j * n:(j + 1) * n], pg)
            pv = t if pv is None else pv + t
        acc_s[...] = alpha * acc_s[...] + pv
        m_s[...] = m_new

    own_a, own_b = head_masks(g * PKEYS)
    lane_g = lax.broadcasted_iota(jnp.int32, (1, g * PKEYS), 1)
    kpos = (p * (g * PAGE_SIZE) + (lane_g >> 2)).astype(F32)
    pages_a = [r[0].reshape(PKEYS, 256).astype(BF16) for r in ca]
    update(qa_blk, pages_a, slope_a * kpos + own_a, ma_s, la_s, acca_s)
    selb = jnp.concatenate([selp_ref[0, j] for j in range(g)], axis=1)
    pages_b = [r[0].reshape(PKEYS, 128).astype(BF16) for r in cb]
    update(qb_blk, pages_b, slope_b * kpos + selb + own_b, mb_s, lb_s, accb_s)
    fparts = []
    carry = carry_s[...]
    for j in range(g):
        fparts.append(pf[j][0, :, 0:PKEYS] + carry)
        carry = carry + pf[j][0, :, PKEYS:2 * PKEYS]
    carry_s[...] = carry
    pages_c = [r[0].reshape(PKEYS, 128).astype(BF16) for r in cc]
    update(qc_blk, pages_c, own_b - jnp.concatenate(fparts, axis=1), mc_s, lc_s, accc_s)

    @pl.when(p == n_steps - 1)
    def _():
        new_pos = float(n_pages * PAGE_SIZE)
        own_a1, own_b1 = head_masks(PAGE_SIZE)
        lane1 = lax.broadcasted_iota(jnp.int32, (1, PAGE_SIZE), 1)
        valid = jnp.where(lane1 < N_HEADS, 0.0, NEG_INF)

        def new_rows(x_s, ref):
            x_s[...] = jnp.zeros(x_s.shape, F32)
            x_s[0:N_HEADS, :] = ref[0]
            return x_s[...].astype(BF16)

        update(qa_blk, [new_rows(xa_s, na_ref)], slope_a * new_pos + own_a1 + valid, ma_s, la_s, acca_s)
        update(qb_blk, [new_rows(xb_s, nb_ref)],
               slope_b * new_pos + seln_ref[0, 0][:, 0:PAGE_SIZE] + own_b1 + valid, mb_s, lb_s, accb_s)
        f_new = carry_s[...][:, 0:PAGE_SIZE] + lfn_ref[0]
        update(qc_blk, [new_rows(xc_s, nc_ref)], own_b1 + valid - f_new, mc_s, lc_s, accc_s)

        lam = _lambda(lam_ref, lam_init)
        inv_a = 1.0 / la_s[...]
        inv_b = 1.0 / lb_s[...]
        inv_c = 1.0 / lc_s[...]
        for h in range(N_HEADS):
            v1 = acca_s[2 * h:2 * h + 1, 128:256] * inv_a[2 * h:2 * h + 1]
            v2 = acca_s[2 * h + 1:2 * h + 2, 128:256] * inv_a[2 * h + 1:2 * h + 2]
            o = v1 - lam * v2
            o = o * lax.rsqrt(jnp.mean(o * o, axis=1, keepdims=True) + NORM_EPS)
            o = o * g_ref[h:h + 1, :] * (1.0 - lam_init)
            oa_ref[0, :, h * 128:(h + 1) * 128] = o.astype(BF16)
            ob_ref[0, :, h * 64:(h + 1) * 64] = (accb_s[h:h + 1, 64:128] * inv_b[h:h + 1]).astype(BF16)
            oc_ref[0, :, h * 64:(h + 1) * 64] = (accc_s[h:h + 1, 64:128] * inv_c[h:h + 1]).astype(BF16)


def _attn_sample(page_table, page_base, ca, cb, cc, pfx, selb4, qa, qb, qc, na, nb, nc, lfn, wts, layer,
                 lam_init, group):
    nsamp, n_pages = page_table.shape
    n_steps = n_pages // group

    def page_spec(width, j):
        return pl.BlockSpec((1, PAGE_SIZE, N_HEADS, width),
                            lambda s, p, pt: (page_base + pt[s, p * group + j], 0, 0, 0))

    def pf_spec(j):
        return pl.BlockSpec((1, 1, 2 * PKEYS), lambda s, p, pt: (page_base + pt[s, p * group + j], 0, 0))

    def samp(shape):
        return pl.BlockSpec((1,) + shape, lambda s, p, pt: (s,) + (0,) * len(shape))

    in_specs = ([page_spec(256, j) for j in range(group)] + [page_spec(128, j) for j in range(group)]
                + [page_spec(128, j) for j in range(group)] + [pf_spec(j) for j in range(group)]
                + [pl.BlockSpec((1, group, 1, PKEYS), lambda s, p, pt: (s, p, 0, 0)),
                   pl.BlockSpec((1, 1, 1, PKEYS), lambda s, p, pt: (s, n_pages, 0, 0)),
                   samp((1, 1024)), samp((1, 512)), samp((1, 512)),
                   samp((N_HEADS, 256)), samp((N_HEADS, 128)), samp((N_HEADS, 128)), samp((1, PAGE_SIZE)),
                   pl.BlockSpec((None, 4, DK_A), lambda s, p, pt: (layer, 0, 0)),
                   pl.BlockSpec((None, H_A, 2 * DK_A), lambda s, p, pt: (layer, 0, 0))])
    out_specs = [samp((1, A_W)), samp((1, B_W)), samp((1, C_W))]
    small = lambda w: [pltpu.VMEM((SUBLANES, 1), F32), pltpu.VMEM((SUBLANES, 1), F32), pltpu.VMEM((SUBLANES, w), F32)]
    grid_spec = pltpu.PrefetchScalarGridSpec(
        num_scalar_prefetch=1, grid=(nsamp, n_steps), in_specs=in_specs, out_specs=out_specs,
        scratch_shapes=small(256) + small(128) + small(128) + [
            pltpu.VMEM((1, PKEYS), F32),
            pltpu.VMEM((PAGE_SIZE, 256), F32), pltpu.VMEM((PAGE_SIZE, 128), F32), pltpu.VMEM((PAGE_SIZE, 128), F32)])
    return pl.pallas_call(
        functools.partial(_attn_s_kernel, group=group, n_pages=n_pages, lam_init=lam_init),
        grid_spec=grid_spec,
        out_shape=[jax.ShapeDtypeStruct((nsamp, 1, A_W), BF16),
                   jax.ShapeDtypeStruct((nsamp, 1, B_W), BF16),
                   jax.ShapeDtypeStruct((nsamp, 1, C_W), BF16)],
        compiler_params=_cparams("arbitrary", "arbitrary"),
        name="attn_sample",
    )(page_table, *([ca] * group), *([cb] * group), *([cc] * group), *([pfx] * group),
      selb4, selb4, qa, qb, qc, na, nb, nc, lfn, wts["lam_p"], wts["g"])


def _prompt_layer(x2d, wts, layer, depth, bsz, seq, lam_init, leaf_bufs):
    n = bsz * seq
    pr = _proj(x2d, wts, layer, _row_tile(n, 512), leaf_bufs=leaf_bufs, depth=depth)
    r3 = lambda a: a.reshape(bsz, seq, a.shape[-1])
    tq = _row_tile(seq, 256)
    oa = _attn_a(r3(pr["qa"]), r3(pr["na"]), wts, layer, lam_init, tq)
    ob = _attn_b(r3(pr["qb"]), r3(pr["nb"]), r3(pr["iq"]), r3(pr["krep"]), r3(pr["misc"]), tq)
    oc = _attn_c(r3(pr["qc"]), r3(pr["nc"]), r3(pr["misc"]), wts, layer, tq)
    x1, new_conv = _merge(x2d, oa.reshape(n, A_W), ob.reshape(n, B_W), oc.reshape(n, C_W), pr["dconv"],
                          None, wts, layer, _row_tile(seq, 256), seq)
    x2 = _ffn(x1, wts, layer, _row_tile(n, 256))
    news = (pr["kidx"].reshape(bsz, seq, D_IDX), pr["logf"].reshape(bsz, seq, H_C), new_conv)
    return x2, news, pr["leaf_bufs"]


def _sample_layer(x2d, wts, layer, caches, page_base, state, page_table, lam_init):
    nsamp = x2d.shape[0]
    n_pages = page_table.shape[1]
    ca, cb, ckt, cc, pfx = caches
    tm = _row_tile(nsamp, 128)
    pr = _proj(x2d, wts, layer, tm)
    iq8 = pr["iq"].reshape(nsamp, H_IDX, D_IDX)
    wi8 = pr["misc"][:, MISC_WI:MISC_WI + H_IDX].reshape(nsamp, H_IDX, 1)
    scores = _idx_scores(page_table, ckt, page_base, iq8, wi8, pr["kidx"].reshape(nsamp, 1, D_IDX))
    topk = min(INDEX_TOPK_MAX, (n_pages * PAGE_SIZE + 1) // 4)
    selb4 = _select_bias(scores, wts["expand"], topk).reshape(nsamp, n_pages + 1, 1, PKEYS)
    lfn = jnp.pad(pr["logf"], ((0, 0), (0, PAGE_SIZE - H_C))).reshape(nsamp, 1, PAGE_SIZE)
    r3 = lambda a: a.reshape(nsamp, 1, a.shape[-1])
    r4 = lambda a: a.reshape(nsamp, N_HEADS, a.shape[-1] // N_HEADS)
    group = n_pages
    oa, ob, oc = _attn_sample(page_table, page_base, ca, cb, cc, pfx, selb4,
                              r3(pr["qa"]), r3(pr["qb"]), r3(pr["qc"]), r4(pr["na"]), r4(pr["nb"]), r4(pr["nc"]),
                              lfn, wts, layer, lam_init, group)
    x1, conv2 = _merge(x2d, oa.reshape(nsamp, A_W), ob.reshape(nsamp, B_W), oc.reshape(nsamp, C_W),
                       pr["dconv"], state.reshape(nsamp, (CONV_K - 1) * CONV_W), wts, layer, tm, 1)
    x2 = _ffn(x1, wts, layer, tm)
    news = (pr["na"].reshape(nsamp, 1, H_A, 4 * DK_A), pr["nb"].reshape(nsamp, 1, H_B, 2 * DH_B),
            pr["kidx"].reshape(nsamp, 1, D_IDX), pr["nc"].reshape(nsamp, 1, H_C, 2 * DH_C),
            pr["logf"].reshape(nsamp, 1, H_C), conv2.reshape(nsamp, CONV_K - 1, CONV_W))
    return x2, news


def kernel(x_prompt, x_sample, cache_a_kv, cache_b_kv, cache_b_kidx, cache_c_kv, cache_c_logf, state_conv, page_table, w_in, b_fgate, lam_q1, lam_k1, lam_q2, lam_k2, g_diffnorm, conv_w, w_br_a, w_br_b, w_br_c, w_br_d, w_o, ln1_g, ln1_b, w_ffn_in, w_ffn_out, ln2_g, ln2_b):
    bsz, seq, _ = x_prompt.shape
    nsamp = x_sample.shape[0]
    depth = w_in.shape[0]
    bf = lambda a: a.astype(BF16)
    w_in_b = bf(w_in)
    wts = dict(
        w1=_rearranged_weights(w_in_b), wg=w_in_b[:, :, O_G:],
        aug_row=jnp.asarray(_build_q_aug_row()),
        bfg_row=jnp.zeros((depth, 1, LANES), F32).at[:, 0, MISC_CF:MISC_CF + H_C].set(b_fgate),
        lam_p=jnp.stack([lam_q1, lam_k1, lam_q2, lam_k2], axis=1),
        g=g_diffnorm, conv_w=conv_w, pa=bf(w_br_a), pb=bf(w_br_b), pc=bf(w_br_c), pd=bf(w_br_d), wo=bf(w_o),
        ln1_g=ln1_g[:, None, :], ln1_b=ln1_b[:, None, :], wf_in=bf(w_ffn_in), wf_out=bf(w_ffn_out),
        ln2_g=ln2_g[:, None, :], ln2_b=ln2_b[:, None, :],
        expand=jnp.asarray(_build_key_expansion(), dtype=BF16))
    n_pool = cache_a_kv.shape[1]
    all_pages = depth * n_pool
    pfx = _page_prefix(cache_c_logf.reshape(all_pages, PKEYS), jnp.asarray(_build_prefix_matrix(), dtype=BF16))
    caches = (cache_a_kv.reshape(all_pages, PAGE_SIZE, N_HEADS, 4 * DK_A),
              cache_b_kv.reshape(all_pages, PAGE_SIZE, N_HEADS, 2 * DH_B),
              jnp.swapaxes(cache_b_kidx, 2, 3).reshape(all_pages, D_IDX, PAGE_SIZE),
              cache_c_kv.reshape(all_pages, PAGE_SIZE, N_HEADS, 2 * DH_C),
              pfx.reshape(all_pages, 1, 2 * PKEYS))

    yp = x_prompt.reshape(bsz * seq, D_MODEL)
    ys = x_sample.reshape(nsamp, D_MODEL)
    news_p, news_s = [], []
    leaf_bufs = "new"
    for l in range(depth):
        lam_init = 0.8 - 0.6 * math.exp(-0.3 * l)
        yp, new_p, leaf_bufs = _prompt_layer(yp, wts, l, depth, bsz, seq, lam_init, leaf_bufs)
        ys, new_s = _sample_layer(ys, wts, l, caches, l * n_pool, state_conv[l], page_table, lam_init)
        news_p.append(new_p)
        news_s.append(new_s)

    def stack(lst, i):
        return jnp.stack([e[i] for e in lst])

    kv_p = [b.reshape(depth, bsz, seq, N_HEADS, b.shape[-1]) for b in leaf_bufs]
    return (yp.reshape(bsz, seq, D_MODEL), ys.reshape(nsamp, 1, D_MODEL),
            kv_p[0], stack(news_s, 0), kv_p[1], stack(news_s, 1),
            stack(news_p, 0), stack(news_s, 2), kv_p[2], stack(news_s, 3),
            stack(news_p, 1), stack(news_s, 4), stack(news_p, 2), stack(news_s, 5))
```

```python
import functools
import math

import numpy as np
import jax
import jax.numpy as jnp
from jax import lax
from jax.experimental import pallas as pl
from jax.experimental.pallas import tpu as pltpu

F32 = jnp.float32
BF16 = jnp.bfloat16
NEG_INF = float("-inf")
M_INIT = -1e30
INT_MIN = -2 ** 31
KEY_OF_NEG_INF = -2139095041

D_MODEL = 1024
DEPTH = 2
PAGE_SIZE = 128
H_A, DK_A = 4, 64
A_W = H_A * 2 * DK_A
H_B, DH_B = 4, 64
B_W = H_B * DH_B
H_IDX, D_IDX = 8, 32
INDEX_TOPK_MAX = 256
H_C, DH_C = 4, 64
C_W = H_C * DH_C
CONV_W, CONV_K = 256, 3
N_BRANCH = 4
D_FF = -(-8 * D_MODEL // (3 * 256)) * 256
ALPHA = (2 * DEPTH) ** 0.25
LN_EPS = 1e-5
NORM_EPS = 1e-6
N_HEADS = 4

_n = H_A + H_B
_S_ALL = 2.0 ** (-8.0 * (np.arange(_n) + 1) / _n)
SLOPES_A = [float(v) for v in _S_ALL[0::2]]
SLOPES_B = [float(v) for v in _S_ALL[1::2]]

LANES = 128
SUBLANES = 8
VMEM_LIMIT_BYTES = 56 * 1024 * 1024
PKEYS = PAGE_SIZE * N_HEADS

_WIDTHS = (A_W, A_W, A_W, B_W, B_W, B_W, H_IDX * D_IDX, D_IDX, H_IDX,
           C_W, C_W, C_W, H_C, CONV_W, CONV_W, CONV_W, N_BRANCH * D_MODEL)
_OFF = np.concatenate([[0], np.cumsum(_WIDTHS)]).astype(np.int64)
(O_AQ, O_AK, O_AV, O_BQ, O_BK, O_BV, O_BIQ, O_BIK, O_BIW, O_CQ, O_CK, O_CV, O_CF,
 O_DB, O_DC, O_DH, O_G, IN_WIDTH) = [int(v) for v in _OFF]

SEG = {}
_pos = 0
for _name, _w in (("qa", 1024), ("na", 1024), ("qb", 512), ("nb", 512), ("iq", 256), ("krep", 256),
                  ("qc", 512), ("nc", 512), ("dconv", 768), ("kidx", 128), ("misc", 128)):
    SEG[_name] = (_pos, _pos + _w)
    _pos += _w
W1_WIDTH = _pos
IDX_COLS = H_IDX * D_IDX + D_IDX + H_IDX
MISC_CF = 0
MISC_WI = 8
QA_AUG = 128
QBC_AUG = 64


def _build_w1_columns():
    src = -np.ones((W1_WIDTH,), np.int64)
    scale = np.ones((W1_WIDTH,), np.float32)
    s = SEG["qa"][0]
    for h in range(H_A):
        src[s + h * 256: s + h * 256 + 128] = O_AQ + h * 128 + np.arange(128)
    scale[SEG["qa"][0]:SEG["qa"][1]] = DK_A ** -0.5
    s = SEG["na"][0]
    for h in range(H_A):
        src[s + h * 256: s + h * 256 + 128] = O_AK + h * 128 + np.arange(128)
        src[s + h * 256 + 128: s + (h + 1) * 256] = O_AV + h * 128 + np.arange(128)
    for nm, oq, ok, ov in (("b", O_BQ, O_BK, O_BV), ("c", O_CQ, O_CK, O_CV)):
        s = SEG["q" + nm][0]
        for h in range(4):
            src[s + h * 128: s + h * 128 + 64] = oq + h * 64 + np.arange(64)
        scale[SEG["q" + nm][0]:SEG["q" + nm][1]] = 64 ** -0.5
        s = SEG["n" + nm][0]
        for h in range(4):
            src[s + h * 128: s + h * 128 + 64] = ok + h * 64 + np.arange(64)
            src[s + h * 128 + 64: s + (h + 1) * 128] = ov + h * 64 + np.arange(64)
    s = SEG["iq"][0]
    src[s:s + 256] = O_BIQ + np.arange(256)
    s = SEG["krep"][0]
    for h in range(H_IDX):
        src[s + h * 32: s + (h + 1) * 32] = O_BIK + np.arange(32)
    s = SEG["dconv"][0]
    src[s:s + 768] = O_DB + np.arange(768)
    s = SEG["kidx"][0]
    src[s:s + 32] = O_BIK + np.arange(32)
    s = SEG["misc"][0]
    src[s + MISC_CF: s + MISC_CF + H_C] = O_CF + np.arange(H_C)
    src[s + MISC_WI: s + MISC_WI + H_IDX] = O_BIW + np.arange(H_IDX)
    scale[s + MISC_WI: s + MISC_WI + H_IDX] = H_IDX ** -0.5 * D_IDX ** -0.5
    return src, scale


_W1_SRC, _W1_SCALE = _build_w1_columns()


def _w1_runs():
    runs = []
    i = 0
    while i < W1_WIDTH:
        j = i + 1
        while (j < W1_WIDTH and _W1_SCALE[j] == _W1_SCALE[i]
               and ((_W1_SRC[i] < 0 and _W1_SRC[j] < 0)
                    or (_W1_SRC[i] >= 0 and _W1_SRC[j] == _W1_SRC[i] + (j - i)))):
            j += 1
        runs.append((int(_W1_SRC[i]), j - i, float(_W1_SCALE[i])))
        i = j
    return runs


def _rearranged_weights(w_in_bf16):
    parts = []
    for start, width, scale in _w1_runs():
        if start < 0:
            parts.append(jnp.zeros(w_in_bf16.shape[:2] + (width,), BF16))
        else:
            piece = w_in_bf16[:, :, start:start + width]
            parts.append(piece if scale == 1.0 else piece * jnp.asarray(scale, BF16))
    return jnp.concatenate(parts, axis=2)


def _build_q_aug_row():
    row = np.zeros((1, W1_WIDTH), np.float32)
    for h in range(4):
        a = SEG["qa"][0] + h * 256 + QA_AUG
        row[0, a], row[0, a + 1] = SLOPES_A[h], SLOPES_A[h] * 256.0
        b = SEG["qb"][0] + h * 128 + QBC_AUG
        row[0, b], row[0, b + 1] = SLOPES_B[h], SLOPES_B[h] * 256.0
        c = SEG["qc"][0] + h * 128 + QBC_AUG
        row[0, c:c + 3] = 1.0
    return row


def _cparams(*sem):
    return pltpu.CompilerParams(dimension_semantics=sem, vmem_limit_bytes=VMEM_LIMIT_BYTES)


def _const_spec(shape):
    nd = len(shape)
    return pl.BlockSpec(shape, lambda *_: (0,) * nd, pipeline_mode=pl.Buffered(1))


def _layer_spec(shape, layer):
    nd = len(shape)
    return pl.BlockSpec((None,) + tuple(shape), lambda *_: (layer,) + (0,) * nd,
                        pipeline_mode=pl.Buffered(1))


def _dot(a, b):
    return jnp.dot(a, b, preferred_element_type=F32)


def _dot_nt(a, b):
    return lax.dot_general(a, b, (((1,), (1,)), ((), ())), preferred_element_type=F32)


def _row_tile(n, pref):
    t = min(n, pref)
    while n % t:
        t //= 2
    return t


def _split3(x):
    hi = x.astype(BF16)
    r = x - hi.astype(F32)
    mid = r.astype(BF16)
    lo = (r - mid.astype(F32)).astype(BF16)
    return hi, mid, lo


def _log_sigmoid(x):
    return jnp.minimum(x, 0.0) - jnp.log1p(jnp.exp(-jnp.abs(x)))


def _layernorm(h, g, b):
    mu = jnp.mean(h, axis=-1, keepdims=True)
    d = h - mu
    var = jnp.mean(d * d, axis=-1, keepdims=True)
    return d * lax.rsqrt(var + LN_EPS) * g + b


def _lambda(lam_ref, lam_init):
    lam_p = lam_ref[...]
    return (jnp.exp(jnp.sum(lam_p[0:1] * lam_p[1:2], axis=1, keepdims=True))
            - jnp.exp(jnp.sum(lam_p[2:3] * lam_p[3:4], axis=1, keepdims=True)) + lam_init)


def _for_causal_extent(qi, tq, seq, body, tiles_per_extent=1):
    n_tiles = seq // tq
    step = min(tiles_per_extent, n_tiles)
    for c in range(n_tiles // step):
        pl.when(qi // step == c)(functools.partial(body, (c + 1) * step * tq))


def _diag_mask(tq):
    return (lax.broadcasted_iota(jnp.int32, (tq, tq), 1) <= lax.broadcasted_iota(jnp.int32, (tq, tq), 0))


def _mask_diag(s, diag):
    tq = diag.shape[0]
    width = s.shape[1]
    tail = jnp.where(diag, s[:, width - tq:], NEG_INF)
    return tail if width == tq else jnp.concatenate([s[:, :width - tq], tail], axis=1)


def _proj_kernel(*refs, n_alias, leaves):
    x_ref, w_ref, aug_ref, bfg_ref = refs[:4]
    (qa_ref, na_ref, qb_ref, nb_ref, iq_ref, kr_ref, qc_ref, nc_ref, dc_ref, misc_ref, kidx_ref,
     logf_ref) = refs[4 + n_alias:16 + n_alias]
    leaf_refs = refs[16 + n_alias:]
    xb = x_ref[...].astype(BF16)

    def seg(name):
        lo, hi = SEG[name]
        return _dot(xb, w_ref[:, lo:hi])

    def qseg(name):
        lo, hi = SEG[name]
        return (seg(name) + aug_ref[:, lo:hi]).astype(BF16)

    news = {"na": seg("na"), "nb": seg("nb"), "nc": seg("nc")}
    qa_ref[...] = qseg("qa")
    na_ref[...] = news["na"]
    qb_ref[...] = qseg("qb")
    nb_ref[...] = news["nb"]
    iq_ref[...] = seg("iq").astype(BF16)
    kr_ref[...] = seg("krep").astype(BF16)
    qc_ref[...] = qseg("qc")
    nc_ref[...] = news["nc"]
    for name, ref in zip(leaves, leaf_refs):
        v = news[name]
        ref[...] = v.reshape(v.shape[0], N_HEADS, v.shape[1] // N_HEADS)
    dc_ref[...] = seg("dconv")
    kidx_ref[...] = seg("kidx")[:, :D_IDX]
    misc = seg("misc")
    misc_ref[...] = misc
    logf_ref[...] = _log_sigmoid(misc + bfg_ref[...])[:, MISC_CF:MISC_CF + H_C]


LEAF_NAMES = ("na", "nb", "nc")


def _proj(x2d, wts, layer, tm, leaf_bufs=None, depth=None):
    n = x2d.shape[0]
    widths = dict((k, v[1] - v[0]) for k, v in SEG.items())
    outs = [("qa", BF16, widths["qa"]), ("na", F32, widths["na"]), ("qb", BF16, widths["qb"]),
            ("nb", F32, widths["nb"]), ("iq", BF16, widths["iq"]), ("krep", BF16, widths["krep"]),
            ("qc", BF16, widths["qc"]), ("nc", F32, widths["nc"]), ("dconv", F32, widths["dconv"]),
            ("misc", F32, widths["misc"]), ("kidx", F32, D_IDX), ("logf", F32, H_C)]
    in_specs = [pl.BlockSpec((tm, D_MODEL), lambda i: (i, 0)),
                _layer_spec((D_MODEL, W1_WIDTH), layer),
                _const_spec((1, W1_WIDTH)),
                _layer_spec((1, LANES), layer)]
    args = [x2d, wts["w1"], wts["aug_row"], wts["bfg_row"]]
    out_specs = [pl.BlockSpec((tm, w), lambda i: (i, 0)) for _, _, w in outs]
    out_shape = [jax.ShapeDtypeStruct((n, w), dt) for _, dt, w in outs]
    leaves, aliases, n_alias = (), {}, 0
    if leaf_bufs is not None:
        leaves = LEAF_NAMES
        for k, name in enumerate(leaves):
            hw = widths[name] // N_HEADS
            out_specs.append(pl.BlockSpec((None, tm, N_HEADS, hw), lambda i: (layer, i, 0, 0)))
            out_shape.append(jax.ShapeDtypeStruct((depth, n, N_HEADS, hw), F32))
            if leaf_bufs != "new":
                in_specs.append(pl.BlockSpec(memory_space=pl.ANY))
                args.append(leaf_bufs[k])
                aliases[len(args) - 1] = len(outs) + k
        n_alias = len(aliases)
    res = pl.pallas_call(
        functools.partial(_proj_kernel, n_alias=n_alias, leaves=leaves),
        grid=(n // tm,),
        in_specs=in_specs, out_specs=out_specs, out_shape=out_shape,
        input_output_aliases=aliases,
        compiler_params=_cparams("parallel"),
        name="proj",
    )(*args)
    out = dict(zip([o[0] for o in outs], res[:len(outs)]))
    out["leaf_bufs"] = tuple(res[len(outs):])
    return out


def _topk_select(score, k):
    rows, width = score.shape
    bits = lax.bitcast_convert_type(score, jnp.int32)
    key = jnp.where(bits < 0, bits ^ jnp.int32(0x7FFFFFFF), bits)
    kf = jnp.float32(k)

    def count(mask):
        return jnp.sum(jnp.where(mask, 1.0, 0.0), axis=1, keepdims=True)

    t0 = jnp.where(count(key >= 0) >= kf, jnp.int32(0), jnp.int32(INT_MIN))

    def vbody(i, t):
        cand = t + jnp.left_shift(jnp.int32(1), jnp.int32(30) - i)
        return jnp.where(count(key >= cand) >= kf, cand, t)

    t = lax.fori_loop(0, 31, vbody, t0)
    gt = key > t
    eq = key == t
    n_gt = count(gt)
    need = kf - n_gt
    idx = lax.broadcasted_iota(jnp.int32, (rows, width), 1)
    nbits = max(1, int(math.ceil(math.log2(width))))

    surplus = jnp.where(t > KEY_OF_NEG_INF, n_gt + count(eq) - kf, 0.0)
    tied = jnp.max(surplus) > 0.0

    def tie_cut():
        def ibody(i, j):
            cand = j + jnp.left_shift(jnp.int32(1), jnp.int32(nbits - 1) - i)
            return jnp.where(count(eq & (idx < cand)) < need, cand, j)

        return lax.fori_loop(0, nbits, ibody, jnp.zeros((rows, 1), jnp.int32))

    j = lax.cond(tied, tie_cut, lambda: jnp.full((rows, 1), width, jnp.int32))
    return gt | (eq & (idx <= j))


def _attn_a_kernel(qa_ref, na_ref, lam_ref, g_ref, o_ref, k1_s, k2_s, v_s, *, tq, seq, lam_init):
    qi = pl.program_id(1)

    @pl.when(qi == 0)
    def _():
        lane = lax.broadcasted_iota(jnp.int32, (seq, LANES), 1)
        kp = lax.broadcasted_iota(jnp.int32, (seq, LANES), 0)
        kaug = jnp.where(lane == 0, (kp & 255).astype(F32),
                         jnp.where(lane == 1, (kp >> 8).astype(F32), 0.0)).astype(BF16)
        for h in range(H_A):
            kk = na_ref[0, :, h * 256:h * 256 + 128]
            k1_s[h, :, 0:LANES] = jnp.where(lane < 64, kk, 0.0).astype(BF16)
            k1_s[h, :, LANES:2 * LANES] = kaug
            k2_s[h, :, 0:LANES] = jnp.where(lane >= 64, kk, 0.0).astype(BF16)
            k2_s[h, :, LANES:2 * LANES] = kaug
            v_s[h] = na_ref[0, :, h * 256 + 128:(h + 1) * 256].astype(BF16)

    lam = _lambda(lam_ref, lam_init)

    def body(width):
        diag = _diag_mask(tq)
        for h in range(H_A):
            qh = qa_ref[0, :, h * 256:(h + 1) * 256]
            s1 = _mask_diag(_dot_nt(qh, k1_s[h, 0:width, :]), diag)
            s2 = _mask_diag(_dot_nt(qh, k2_s[h, 0:width, :]), diag)
            e1 = jnp.exp(s1 - jnp.max(s1, axis=1, keepdims=True))
            e2 = jnp.exp(s2 - jnp.max(s2, axis=1, keepdims=True))
            r1 = 1.0 / jnp.sum(e1, axis=1, keepdims=True)
            r2 = lam / jnp.sum(e2, axis=1, keepdims=True)
            p = (e1 * r1 - e2 * r2).astype(BF16)
            o = _dot(p, v_s[h, 0:width, :])
            o = o * lax.rsqrt(jnp.mean(o * o, axis=1, keepdims=True) + NORM_EPS)
            o = o * g_ref[h:h + 1, :] * (1.0 - lam_init)
            o_ref[0, :, h * 128:(h + 1) * 128] = o.astype(BF16)

    _for_causal_extent(qi, tq, seq, body)


def _attn_a(qa, na, wts, layer, lam_init, tq):
    bsz, seq, _ = qa.shape
    return pl.pallas_call(
        functools.partial(_attn_a_kernel, tq=tq, seq=seq, lam_init=lam_init),
        grid=(bsz, seq // tq),
        in_specs=[pl.BlockSpec((1, tq, 1024), lambda b, i: (b, i, 0)),
                  pl.BlockSpec((1, seq, 1024), lambda b, i: (b, 0, 0)),
                  _layer_spec((4, DK_A), layer),
                  _layer_spec((H_A, 2 * DK_A), layer)],
        out_specs=pl.BlockSpec((1, tq, A_W), lambda b, i: (b, i, 0)),
        out_shape=jax.ShapeDtypeStruct((bsz, seq, A_W), BF16),
        scratch_shapes=[pltpu.VMEM((H_A, seq, 2 * LANES), BF16), pltpu.VMEM((H_A, seq, 2 * LANES), BF16),
                        pltpu.VMEM((H_A, seq, LANES), BF16)],
        compiler_params=_cparams("parallel", "arbitrary"),
        name="attn_a",
    )(qa, na, wts["lam_p"], wts["g"])


def _attn_c_kernel(qc_ref, nc_ref, misc_ref, bfg_ref, o_ref, k_s, v_s, *, tq, seq):
    qi = pl.program_id(1)

    @pl.when(qi == 0)
    def _():
        lane = lax.broadcasted_iota(jnp.int32, (seq, LANES), 1)
        ch = min(256, seq)
        tri = jnp.where(lax.broadcasted_iota(jnp.int32, (ch, ch), 0)
                        >= lax.broadcasted_iota(jnp.int32, (ch, ch), 1), 1.0, 0.0).astype(BF16)
        carry = jnp.zeros((1, LANES), F32)
        chunks = []
        for c in range(seq // ch):
            lf = _log_sigmoid(misc_ref[0, c * ch:(c + 1) * ch, :] + bfg_ref[...])
            hi, mid, lo = _split3(lf)
            fc = (_dot(tri, hi) + _dot(tri, mid)) + _dot(tri, lo) + carry
            carry = fc[ch - 1:ch, :]
            chunks.append(fc)
        fcum = jnp.concatenate(chunks, axis=0) if len(chunks) > 1 else chunks[0]
        for h in range(H_C):
            fh = jnp.broadcast_to(fcum[:, MISC_CF + h:MISC_CF + h + 1], (seq, LANES))
            hi, mid, lo = (v.astype(F32) for v in _split3(-fh))
            kv = nc_ref[0, :, h * 128:(h + 1) * 128]
            aug = jnp.where(lane == QBC_AUG, hi,
                            jnp.where(lane == QBC_AUG + 1, mid, jnp.where(lane == QBC_AUG + 2, lo, 0.0)))
            k_s[h] = jnp.where(lane < 64, kv, aug).astype(BF16)
            v_s[h] = kv.astype(BF16)

    def body(width):
        diag = _diag_mask(tq)
        for h in range(H_C):
            qh = qc_ref[0, :, h * 128:(h + 1) * 128]
            s = _mask_diag(_dot_nt(qh, k_s[h, 0:width, :]), diag)
            e = jnp.exp(s - jnp.max(s, axis=1, keepdims=True))
            r = 1.0 / jnp.sum(e, axis=1, keepdims=True)
            o = _dot(e.astype(BF16), v_s[h, 0:width, :]) * r
            o_ref[0, :, h * 64:(h + 1) * 64] = o[:, 64:128].astype(BF16)

    _for_causal_extent(qi, tq, seq, body)


def _attn_c(qc, nc, misc, wts, layer, tq):
    bsz, seq, _ = qc.shape
    return pl.pallas_call(
        functools.partial(_attn_c_kernel, tq=tq, seq=seq),
        grid=(bsz, seq // tq),
        in_specs=[pl.BlockSpec((1, tq, 512), lambda b, i: (b, i, 0)),
                  pl.BlockSpec((1, seq, 512), lambda b, i: (b, 0, 0)),
                  pl.BlockSpec((1, seq, LANES), lambda b, i: (b, 0, 0)),
                  _layer_spec((1, LANES), layer)],
        out_specs=pl.BlockSpec((1, tq, C_W), lambda b, i: (b, i, 0)),
        out_shape=jax.ShapeDtypeStruct((bsz, seq, C_W), BF16),
        scratch_shapes=[pltpu.VMEM((H_C, seq, LANES), BF16)] * 2,
        compiler_params=_cparams("parallel", "arbitrary"),
        name="attn_c",
    )(qc, nc, misc, wts["bfg_row"])


def _attn_b_kernel(qb_ref, nb_ref, iq_ref, kr_ref, misc_ref, o_ref, k_s, v_s, kr_s, *, tq, seq, topk):
    qi = pl.program_id(1)

    @pl.when(qi == 0)
    def _():
        lane = lax.broadcasted_iota(jnp.int32, (seq, LANES), 1)
        kp = lax.broadcasted_iota(jnp.int32, (seq, LANES), 0)
        aug = jnp.where(lane == QBC_AUG, (kp & 255).astype(F32),
                        jnp.where(lane == QBC_AUG + 1, (kp >> 8).astype(F32), 0.0))
        for h in range(H_B):
            kv = nb_ref[0, :, h * 128:(h + 1) * 128]
            k_s[h] = jnp.where(lane < 64, kv, aug).astype(BF16)
            v_s[h] = kv.astype(BF16)
        kr = kr_ref[0]
        klane = lax.broadcasted_iota(jnp.int32, (seq, 256), 1)
        for h in range(H_IDX):
            kr_s[h] = kr * jnp.where((klane >> 5) == h, 1.0, 0.0).astype(BF16)

    def body(width):
        rows = qi * tq + lax.broadcasted_iota(jnp.int32, (tq, width), 0)
        cols = lax.broadcasted_iota(jnp.int32, (tq, width), 1)
        causal = cols <= rows
        iq = iq_ref[0]
        score = None
        for h in range(H_IDX):
            rel = jnp.maximum(_dot_nt(iq, kr_s[h, 0:width, :]), 0.0) * misc_ref[0, :, MISC_WI + h:MISC_WI + h + 1]
            score = rel if score is None else score + rel
        sel = _topk_select(jnp.where(causal, score, NEG_INF), topk)
        bias = jnp.where(sel & causal, 0.0, NEG_INF)
        for h in range(H_B):
            qh = qb_ref[0, :, h * 128:(h + 1) * 128]
            s = _dot_nt(qh, k_s[h, 0:width, :]) + bias
            e = jnp.exp(s - jnp.max(s, axis=1, keepdims=True))
            r = 1.0 / jnp.sum(e, axis=1, keepdims=True)
            o = _dot(e.astype(BF16), v_s[h, 0:width, :]) * r
            o_ref[0, :, h * 64:(h + 1) * 64] = o[:, 64:128].astype(BF16)

    _for_causal_extent(qi, tq, seq, body, tiles_per_extent=2)


def _attn_b(qb, nb, iq, krep, misc, tq):
    bsz, seq, _ = qb.shape
    topk = min(INDEX_TOPK_MAX, seq // 4)
    return pl.pallas_call(
        functools.partial(_attn_b_kernel, tq=tq, seq=seq, topk=topk),
        grid=(bsz, seq // tq),
        in_specs=[pl.BlockSpec((1, tq, 512), lambda b, i: (b, i, 0)),
                  pl.BlockSpec((1, seq, 512), lambda b, i: (b, 0, 0)),
                  pl.BlockSpec((1, tq, 256), lambda b, i: (b, i, 0)),
                  pl.BlockSpec((1, seq, 256), lambda b, i: (b, 0, 0)),
                  pl.BlockSpec((1, tq, LANES), lambda b, i: (b, i, 0))],
        out_specs=pl.BlockSpec((1, tq, B_W), lambda b, i: (b, i, 0)),
        out_shape=jax.ShapeDtypeStruct((bsz, seq, B_W), BF16),
        scratch_shapes=[pltpu.VMEM((H_B, seq, LANES), BF16), pltpu.VMEM((H_B, seq, LANES), BF16),
                        pltpu.VMEM((H_IDX, seq, H_IDX * D_IDX), BF16)],
        compiler_params=_cparams("parallel", "arbitrary"),
        name="attn_b",
    )(qb, nb, iq, krep, misc)


def _merge_kernel(*refs, tm, sample, tiles_per_seq):
    if sample:
        (x_ref, oa_ref, ob_ref, oc_ref, dc_ref, st_ref, wg_ref, pa_ref, pb_ref, pc_ref, pd_ref,
         wo_ref, g_ref, b_ref, cw_ref, y_ref, conv_ref) = refs
    else:
        (x_ref, oa_ref, ob_ref, oc_ref, dc_ref, halo_ref, wg_ref, pa_ref, pb_ref, pc_ref, pd_ref,
         wo_ref, g_ref, b_ref, cw_ref, y_ref, conv_ref, uu_ref) = refs
    w0 = cw_ref[0:1, :]
    w1 = cw_ref[1:2, :]
    w2 = cw_ref[2:3, :]
    u = dc_ref[:, 256:512] * dc_ref[:, 512:768]
    if sample:
        s0 = st_ref[:, 0:256]
        s1 = st_ref[:, 256:512]
        y = s0 * w0 + s1 * w1 + u * w2
        conv_ref[:, 0:256] = s1
        conv_ref[:, 256:512] = u
    else:
        first = (pl.program_id(0) % tiles_per_seq) == 0
        uh = halo_ref[:, 256:512] * halo_ref[:, 512:768]
        uu_ref[0:SUBLANES, :] = jnp.where(first, 0.0, uh)
        uu_ref[SUBLANES:SUBLANES + tm, :] = u
        y = uu_ref[SUBLANES - 2:SUBLANES - 2 + tm, :] * w0 + uu_ref[SUBLANES - 1:SUBLANES - 1 + tm, :] * w1 + u * w2
        conv_ref[0] = uu_ref[SUBLANES + tm - 2:SUBLANES + tm, :]
    o_d = (dc_ref[:, 0:256] * y).astype(BF16)
    xv = x_ref[...]
    xb = xv.astype(BF16)
    mixed = None
    for i, (o, p_ref) in enumerate(((oa_ref[...], pa_ref), (ob_ref[...], pb_ref),
                                    (oc_ref[...], pc_ref), (o_d, pd_ref))):
        gate = jax.nn.sigmoid(_dot(xb, wg_ref[:, i * D_MODEL:(i + 1) * D_MODEL]))
        term = gate * _dot(o, p_ref[...])
        mixed = term if mixed is None else mixed + term
    h = ALPHA * xv + _dot(mixed.astype(BF16), wo_ref[...])
    y_ref[...] = _layernorm(h, g_ref[...], b_ref[...])


def _merge(x2d, oa, ob, oc, dconv, state2d, wts, layer, tm, seq):
    n = x2d.shape[0]
    sample = state2d is not None
    row = lambda w: pl.BlockSpec((tm, w), lambda i: (i, 0))
    in_specs = [row(D_MODEL), row(A_W), row(B_W), row(C_W), row(768)]
    args = [x2d, oa, ob, oc, dconv]
    if sample:
        in_specs.append(row(512))
        args.append(state2d)
        out_specs = [row(D_MODEL), row(512)]
        out_shape = [jax.ShapeDtypeStruct((n, D_MODEL), F32), jax.ShapeDtypeStruct((n, 512), F32)]
        scratch = []
        tiles_per_seq = 1
    else:
        tiles_per_seq = seq // tm
        r8 = tm // SUBLANES
        in_specs.append(pl.BlockSpec((SUBLANES, 768), lambda i: (jnp.maximum(i * r8 - 1, 0), 0)))
        args.append(dconv)
        out_specs = [row(D_MODEL), pl.BlockSpec((1, CONV_K - 1, CONV_W), lambda i: (i // tiles_per_seq, 0, 0))]
        out_shape = [jax.ShapeDtypeStruct((n, D_MODEL), F32),
                     jax.ShapeDtypeStruct((n // seq, CONV_K - 1, CONV_W), F32)]
        scratch = [pltpu.VMEM((tm + SUBLANES, CONV_W), F32)]
    for name in ("wg", "pa", "pb", "pc", "pd", "wo", "ln1_g", "ln1_b", "conv_w"):
        in_specs.append(_layer_spec(wts[name].shape[1:], layer))
        args.append(wts[name])
    return pl.pallas_call(
        functools.partial(_merge_kernel, tm=tm, sample=sample, tiles_per_seq=tiles_per_seq),
        grid=(n // tm,),
        in_specs=in_specs, out_specs=out_specs, out_shape=out_shape, scratch_shapes=scratch,
        compiler_params=_cparams("arbitrary"),
        name="merge_s" if sample else "merge",
    )(*args)


def _ffn_kernel(x_ref, wi_ref, wo_ref, g_ref, b_ref, y_ref):
    xv = x_ref[...]
    xb = xv.astype(BF16)
    hg = _dot(xb, wi_ref[:, 0:D_FF])
    hu = _dot(xb, wi_ref[:, D_FF:2 * D_FF])
    act = (hg * jax.nn.sigmoid(hg) * hu).astype(BF16)
    h = ALPHA * xv + _dot(act, wo_ref[...])
    y_ref[...] = _layernorm(h, g_ref[...], b_ref[...])


def _ffn(x2d, wts, layer, tm):
    n = x2d.shape[0]
    return pl.pallas_call(
        _ffn_kernel,
        grid=(n // tm,),
        in_specs=[pl.BlockSpec((tm, D_MODEL), lambda i: (i, 0)),
                  _layer_spec((D_MODEL, 2 * D_FF), layer), _layer_spec((D_FF, D_MODEL), layer),
                  _layer_spec((1, D_MODEL), layer), _layer_spec((1, D_MODEL), layer)],
        out_specs=pl.BlockSpec((tm, D_MODEL), lambda i: (i, 0)),
        out_shape=jax.ShapeDtypeStruct((n, D_MODEL), F32),
        compiler_params=_cparams("parallel"),
        name="ffn",
    )(x2d, wts["wf_in"], wts["wf_out"], wts["ln2_g"], wts["ln2_b"])


def _build_prefix_matrix():
    c = np.arange(PKEYS)
    same_head = (c[:, None] % N_HEADS) == (c[None, :] % N_HEADS)
    upto = (c[:, None] // N_HEADS) <= (c[None, :] // N_HEADS)
    return np.concatenate([same_head & upto, same_head], axis=1).astype(np.float32)


def _build_key_expansion():
    return (np.arange(PAGE_SIZE)[:, None] == (np.arange(PKEYS)[None, :] // N_HEADS)).astype(np.float32)


def _pfx_kernel(x_ref, u_ref, o_ref):
    hi, mid, lo = _split3(x_ref[...])
    u = u_ref[...]
    o_ref[...] = (_dot(hi, u) + _dot(mid, u)) + _dot(lo, u)


def _page_prefix(logf_pages, umat):
    n_pool = logf_pages.shape[0]
    tp = _row_tile(n_pool, 512) if n_pool % SUBLANES == 0 else n_pool
    return pl.pallas_call(
        _pfx_kernel,
        grid=(n_pool // tp,),
        in_specs=[pl.BlockSpec((tp, PKEYS), lambda i: (i, 0)),
                  _const_spec(umat.shape)],
        out_specs=pl.BlockSpec((tp, 2 * PKEYS), lambda i: (i, 0)),
        out_shape=jax.ShapeDtypeStruct((n_pool, 2 * PKEYS), F32),
        compiler_params=_cparams("parallel"),
        name="page_prefix",
    )(logf_pages, umat)


def _dot_split(a, b, dot):
    a0, a1, a2 = _split3(a)
    b0, b1, b2 = _split3(b)
    small = (dot(a0, b2) + dot(a2, b0)) + dot(a1, b1)
    return (small + (dot(a0, b1) + dot(a1, b0))) + dot(a0, b0)


def _idx_proj_kernel(x_ref, w_ref, o_ref):
    o_ref[...] = _dot_split(x_ref[...], w_ref[...], _dot)


def _idx_proj(x2d, w_idx, layer):
    return pl.pallas_call(
        _idx_proj_kernel,
        out_shape=jax.ShapeDtypeStruct((x2d.shape[0], w_idx.shape[2]), F32),
        compiler_params=pltpu.CompilerParams(vmem_limit_bytes=VMEM_LIMIT_BYTES),
        name="idx_proj",
    )(x2d, w_idx[layer])


def _idx_kernel(pt_ref, *refs, n_pages):
    pages = refs[:n_pages]
    iq_ref, wi_ref, knew_ref, o_ref = refs[n_pages:]
    iq = iq_ref[0]
    wi = wi_ref[0] * (H_IDX ** -0.5 * D_IDX ** -0.5)

    def weighted(rel):
        return jnp.sum(jnp.maximum(rel, 0.0) * wi, axis=0, keepdims=True)

    for j in range(n_pages):
        o_ref[0, :, j * PAGE_SIZE:(j + 1) * PAGE_SIZE] = weighted(_dot_split(iq, pages[j][0], _dot))
    knew = jnp.broadcast_to(knew_ref[0], (PAGE_SIZE, D_IDX))
    lane = lax.broadcasted_iota(jnp.int32, (1, PAGE_SIZE), 1)
    o_ref[0, :, n_pages * PAGE_SIZE:(n_pages + 1) * PAGE_SIZE] = jnp.where(
        lane == 0, weighted(_dot_split(iq, knew, _dot_nt)), NEG_INF)


def _idx_scores(page_table, kidx_t, page_base, iq8, wi8, knew):
    nsamp, n_pages = page_table.shape
    width = (n_pages + 1) * PAGE_SIZE

    def page_spec(j):
        return pl.BlockSpec((1, D_IDX, PAGE_SIZE), lambda s, pt: (page_base + pt[s, j], 0, 0))

    grid_spec = pltpu.PrefetchScalarGridSpec(
        num_scalar_prefetch=1,
        grid=(nsamp,),
        in_specs=[page_spec(j) for j in range(n_pages)] + [
            pl.BlockSpec((1, H_IDX, D_IDX), lambda s, pt: (s, 0, 0)),
            pl.BlockSpec((1, H_IDX, 1), lambda s, pt: (s, 0, 0)),
            pl.BlockSpec((1, 1, D_IDX), lambda s, pt: (s, 0, 0))],
        out_specs=pl.BlockSpec((1, 1, width), lambda s, pt: (s, 0, 0)),
    )
    out = pl.pallas_call(
        functools.partial(_idx_kernel, n_pages=n_pages),
        grid_spec=grid_spec,
        out_shape=jax.ShapeDtypeStruct((nsamp, 1, width), F32),
        compiler_params=_cparams("arbitrary"),
        name="idx_scores",
    )(page_table, *([kidx_t] * n_pages), iq8, wi8, knew)
    return out.reshape(nsamp, width)


def _sel_kernel(s_ref, e_ref, o_ref, *, topk, n_blocks):
    sel = jnp.where(_topk_select(s_ref[...], topk), 1.0, 0.0).astype(BF16)
    for j in range(n_blocks):
        rep = _dot(sel[:, j * PAGE_SIZE:(j + 1) * PAGE_SIZE], e_ref[...])
        o_ref[:, j * PKEYS:(j + 1) * PKEYS] = jnp.where(rep > 0.5, 0.0, NEG_INF)


def _select_bias(scores, expand, topk):
    nsamp, width = scores.shape
    n_blocks = width // PAGE_SIZE
    return pl.pallas_call(
        functools.partial(_sel_kernel, topk=topk, n_blocks=n_blocks),
        out_shape=jax.ShapeDtypeStruct((nsamp, n_blocks * PKEYS), F32),
        compiler_params=pltpu.CompilerParams(vmem_limit_bytes=VMEM_LIMIT_BYTES),
        name="select_bias",
    )(scores, expand)


def _attn_s_kernel(pt_ref, *refs, group, n_pages, lam_init):
    g = group
    ca = refs[0:g]
    cb = refs[g:2 * g]
    cc = refs[2 * g:3 * g]
    pf = refs[3 * g:4 * g]
    (selp_ref, seln_ref, qa_ref, qb_ref, qc_ref, na_ref, nb_ref, nc_ref, lfn_ref, lam_ref, g_ref,
     oa_ref, ob_ref, oc_ref,
     ma_s, la_s, acca_s, mb_s, lb_s, accb_s, mc_s, lc_s, accc_s, carry_s, xa_s, xb_s, xc_s) = refs[4 * g:]
    p = pl.program_id(1)
    n_steps = n_pages // g

    @pl.when(p == 0)
    def _():
        for m_s, l_s, acc_s in ((ma_s, la_s, acca_s), (mb_s, lb_s, accb_s), (mc_s, lc_s, accc_s)):
            m_s[...] = jnp.full(m_s.shape, M_INIT, F32)
            l_s[...] = jnp.zeros(l_s.shape, F32)
            acc_s[...] = jnp.zeros(acc_s.shape, F32)
        carry_s[...] = jnp.zeros(carry_s.shape, F32)

    sub_a = lax.broadcasted_iota(jnp.int32, (SUBLANES, 256), 0)
    lane_a = lax.broadcasted_iota(jnp.int32, (SUBLANES, 256), 1)
    qa_row = qa_ref[0].astype(F32)
    qa_blk = jnp.zeros((SUBLANES, 256), F32)
    for h in range(N_HEADS):
        qa_blk = jnp.where((sub_a >> 1) == h, jnp.broadcast_to(qa_row[:, h * 256:(h + 1) * 256], (SUBLANES, 256)),
                           qa_blk)
    qa_blk = jnp.where((lane_a < 128) & ((lane_a >> 6) == (sub_a & 1)), qa_blk, 0.0).astype(BF16)
    sub_b = lax.broadcasted_iota(jnp.int32, (SUBLANES, LANES), 0)
    lane_b = lax.broadcasted_iota(jnp.int32, (SUBLANES, LANES), 1)

    def head_block(row):
        blk = jnp.zeros((SUBLANES, LANES), F32)
        for h in range(N_HEADS):
            blk = jnp.where(sub_b == h, jnp.broadcast_to(row[:, h * 128:(h + 1) * 128], (SUBLANES, LANES)), blk)
        return jnp.where(lane_b < 64, blk, 0.0).astype(BF16)

    qb_blk = head_block(qb_ref[0].astype(F32))
    qc_blk = head_block(qc_ref[0].astype(F32))

    sub1 = lax.broadcasted_iota(jnp.int32, (SUBLANES, 1), 0)
    slope_a = jnp.zeros((SUBLANES, 1), F32)
    slope_b = jnp.zeros((SUBLANES, 1), F32)
    for h in range(N_HEADS):
        slope_a = jnp.where((sub1 >> 1) == h, SLOPES_A[h], slope_a)
        slope_b = jnp.where(sub1 == h, SLOPES_B[h], slope_b)

    def head_masks(width):
        sub = lax.broadcasted_iota(jnp.int32, (SUBLANES, width), 0)
        lane = lax.broadcasted_iota(jnp.int32, (SUBLANES, width), 1)
        own_a = jnp.where((lane & 3) == (sub >> 1), 0.0, NEG_INF)
        own_b = jnp.where((lane & 3) == sub, 0.0, NEG_INF)
        return own_a, own_b

    def update(qblk, pages, bias, m_s, l_s, acc_s):
        n = pages[0].shape[0]
        s = jnp.concatenate([_dot_nt(qblk, pg) for pg in pages], axis=1) + bias
        m_old = m_s[...]
        m_new = jnp.maximum(m_old, jnp.max(s, axis=1, keepdims=True))
        alpha = jnp.exp(m_old - m_new)
        e = jnp.exp(s - m_new)
        l_s[...] = alpha * l_s[...] + jnp.sum(e, axis=1, keepdims=True)
        eb = e.astype(BF16)
        pv = None
        for j, pg in enumerate(pages):
            t = _dot(eb[:, j * n:(j + 1) * n], pg)
            pv = t if pv is None else pv + t
        acc_s[...] = alpha * acc_s[...] + pv
        m_s[...] = m_new

    own_a, own_b = head_masks(g * PKEYS)
    lane_g = lax.broadcasted_iota(jnp.int32, (1, g * PKEYS), 1)
    kpos = (p * (g * PAGE_SIZE) + (lane_g >> 2)).astype(F32)
    pages_a = [r[0].reshape(PKEYS, 256).astype(BF16) for r in ca]
    update(qa_blk, pages_a, slope_a * kpos + own_a, ma_s, la_s, acca_s)
    selb = jnp.concatenate([selp_ref[0, j] for j in range(g)], axis=1)
    pages_b = [r[0].reshape(PKEYS, 128).astype(BF16) for r in cb]
    update(qb_blk, pages_b, slope_b * kpos + selb + own_b, mb_s, lb_s, accb_s)
    fparts = []
    carry = carry_s[...]
    for j in range(g):
        fparts.append(pf[j][0, :, 0:PKEYS] + carry)
        carry = carry + pf[j][0, :, PKEYS:2 * PKEYS]
    carry_s[...] = carry
    pages_c = [r[0].reshape(PKEYS, 128).astype(BF16) for r in cc]
    update(qc_blk, pages_c, own_b - jnp.concatenate(fparts, axis=1), mc_s, lc_s, accc_s)

    @pl.when(p == n_steps - 1)
    def _():
        new_pos = float(n_pages * PAGE_SIZE)
        own_a1, own_b1 = head_masks(PAGE_SIZE)
        lane1 = lax.broadcasted_iota(jnp.int32, (1, PAGE_SIZE), 1)
        valid = jnp.where(lane1 < N_HEADS, 0.0, NEG_INF)

        def new_rows(x_s, ref):
            x_s[...] = jnp.zeros(x_s.shape, F32)
            x_s[0:N_HEADS, :] = ref[0]
            return x_s[...].astype(BF16)

        update(qa_blk, [new_rows(xa_s, na_ref)], slope_a * new_pos + own_a1 + valid, ma_s, la_s, acca_s)
        update(qb_blk, [new_rows(xb_s, nb_ref)],
               slope_b * new_pos + seln_ref[0, 0][:, 0:PAGE_SIZE] + own_b1 + valid, mb_s, lb_s, accb_s)
        f_new = carry_s[...][:, 0:PAGE_SIZE] + lfn_ref[0]
        update(qc_blk, [new_rows(xc_s, nc_ref)], own_b1 + valid - f_new, mc_s, lc_s, accc_s)

        lam = _lambda(lam_ref, lam_init)
        inv_a = 1.0 / la_s[...]
        inv_b = 1.0 / lb_s[...]
        inv_c = 1.0 / lc_s[...]
        for h in range(N_HEADS):
            v1 = acca_s[2 * h:2 * h + 1, 128:256] * inv_a[2 * h:2 * h + 1]
            v2 = acca_s[2 * h + 1:2 * h + 2, 128:256] * inv_a[2 * h + 1:2 * h + 2]
            o = v1 - lam * v2
            o = o * lax.rsqrt(jnp.mean(o * o, axis=1, keepdims=True) + NORM_EPS)
            o = o * g_ref[h:h + 1, :] * (1.0 - lam_init)
            oa_ref[0, :, h * 128:(h + 1) * 128] = o.astype(BF16)
            ob_ref[0, :, h * 64:(h + 1) * 64] = (accb_s[h:h + 1, 64:128] * inv_b[h:h + 1]).astype(BF16)
            oc_ref[0, :, h * 64:(h + 1) * 64] = (accc_s[h:h + 1, 64:128] * inv_c[h:h + 1]).astype(BF16)


def _attn_sample(page_table, page_base, ca, cb, cc, pfx, selb4, qa, qb, qc, na, nb, nc, lfn, wts, layer,
                 lam_init, group):
    nsamp, n_pages = page_table.shape
    n_steps = n_pages // group

    def page_spec(width, j):
        return pl.BlockSpec((1, PAGE_SIZE, N_HEADS, width),
                            lambda s, p, pt: (page_base + pt[s, p * group + j], 0, 0, 0))

    def pf_spec(j):
        return pl.BlockSpec((1, 1, 2 * PKEYS), lambda s, p, pt: (page_base + pt[s, p * group + j], 0, 0))

    def samp(shape):
        return pl.BlockSpec((1,) + shape, lambda s, p, pt: (s,) + (0,) * len(shape))

    in_specs = ([page_spec(256, j) for j in range(group)] + [page_spec(128, j) for j in range(group)]
                + [page_spec(128, j) for j in range(group)] + [pf_spec(j) for j in range(group)]
                + [pl.BlockSpec((1, group, 1, PKEYS), lambda s, p, pt: (s, p, 0, 0)),
                   pl.BlockSpec((1, 1, 1, PKEYS), lambda s, p, pt: (s, n_pages, 0, 0)),
                   samp((1, 1024)), samp((1, 512)), samp((1, 512)),
                   samp((N_HEADS, 256)), samp((N_HEADS, 128)), samp((N_HEADS, 128)), samp((1, PAGE_SIZE)),
                   pl.BlockSpec((None, 4, DK_A), lambda s, p, pt: (layer, 0, 0)),
                   pl.BlockSpec((None, H_A, 2 * DK_A), lambda s, p, pt: (layer, 0, 0))])
    out_specs = [samp((1, A_W)), samp((1, B_W)), samp((1, C_W))]
    small = lambda w: [pltpu.VMEM((SUBLANES, 1), F32), pltpu.VMEM((SUBLANES, 1), F32), pltpu.VMEM((SUBLANES, w), F32)]
    grid_spec = pltpu.PrefetchScalarGridSpec(
        num_scalar_prefetch=1, grid=(nsamp, n_steps), in_specs=in_specs, out_specs=out_specs,
        scratch_shapes=small(256) + small(128) + small(128) + [
            pltpu.VMEM((1, PKEYS), F32),
            pltpu.VMEM((PAGE_SIZE, 256), F32), pltpu.VMEM((PAGE_SIZE, 128), F32), pltpu.VMEM((PAGE_SIZE, 128), F32)])
    return pl.pallas_call(
        functools.partial(_attn_s_kernel, group=group, n_pages=n_pages, lam_init=lam_init),
        grid_spec=grid_spec,
        out_shape=[jax.ShapeDtypeStruct((nsamp, 1, A_W), BF16),
                   jax.ShapeDtypeStruct((nsamp, 1, B_W), BF16),
                   jax.ShapeDtypeStruct((nsamp, 1, C_W), BF16)],
        compiler_params=_cparams("arbitrary", "arbitrary"),
        name="attn_sample",
    )(page_table, *([ca] * group), *([cb] * group), *([cc] * group), *([pfx] * group),
      selb4, selb4, qa, qb, qc, na, nb, nc, lfn, wts["lam_p"], wts["g"])


def _prompt_layer(x2d, wts, layer, depth, bsz, seq, lam_init, leaf_bufs):
    n = bsz * seq
    pr = _proj(x2d, wts, layer, _row_tile(n, 512), leaf_bufs=leaf_bufs, depth=depth)
    r3 = lambda a: a.reshape(bsz, seq, a.shape[-1])
    tq = _row_tile(seq, 256)
    oa = _attn_a(r3(pr["qa"]), r3(pr["na"]), wts, layer, lam_init, tq)
    ob = _attn_b(r3(pr["qb"]), r3(pr["nb"]), r3(pr["iq"]), r3(pr["krep"]), r3(pr["misc"]), tq)
    oc = _attn_c(r3(pr["qc"]), r3(pr["nc"]), r3(pr["misc"]), wts, layer, tq)
    x1, new_conv = _merge(x2d, oa.reshape(n, A_W), ob.reshape(n, B_W), oc.reshape(n, C_W), pr["dconv"],
                          None, wts, layer, _row_tile(seq, 256), seq)
    x2 = _ffn(x1, wts, layer, _row_tile(n, 256))
    news = (pr["kidx"].reshape(bsz, seq, D_IDX), pr["logf"].reshape(bsz, seq, H_C), new_conv)
    return x2, news, pr["leaf_bufs"]


def _sample_layer(x2d, wts, layer, caches, page_base, state, page_table, lam_init):
    nsamp = x2d.shape[0]
    n_pages = page_table.shape[1]
    ca, cb, ckt, cc, pfx = caches
    tm = _row_tile(nsamp, 128)
    pr = _proj(x2d, wts, layer, tm)
    zi = _idx_proj(x2d, wts["w_idx"], layer)
    iq8 = zi[:, 0:H_IDX * D_IDX].reshape(nsamp, H_IDX, D_IDX)
    knew = zi[:, H_IDX * D_IDX:H_IDX * D_IDX + D_IDX].reshape(nsamp, 1, D_IDX)
    wi8 = zi[:, H_IDX * D_IDX + D_IDX:IDX_COLS].reshape(nsamp, H_IDX, 1)
    scores = _idx_scores(page_table, ckt, page_base, iq8, wi8, knew)
    topk = min(INDEX_TOPK_MAX, (n_pages * PAGE_SIZE + 1) // 4)
    selb4 = _select_bias(scores, wts["expand"], topk).reshape(nsamp, n_pages + 1, 1, PKEYS)
    lfn = jnp.pad(pr["logf"], ((0, 0), (0, PAGE_SIZE - H_C))).reshape(nsamp, 1, PAGE_SIZE)
    r3 = lambda a: a.reshape(nsamp, 1, a.shape[-1])
    r4 = lambda a: a.reshape(nsamp, N_HEADS, a.shape[-1] // N_HEADS)
    group = n_pages
    oa, ob, oc = _attn_sample(page_table, page_base, ca, cb, cc, pfx, selb4,
                              r3(pr["qa"]), r3(pr["qb"]), r3(pr["qc"]), r4(pr["na"]), r4(pr["nb"]), r4(pr["nc"]),
                              lfn, wts, layer, lam_init, group)
    x1, conv2 = _merge(x2d, oa.reshape(nsamp, A_W), ob.reshape(nsamp, B_W), oc.reshape(nsamp, C_W),
                       pr["dconv"], state.reshape(nsamp, (CONV_K - 1) * CONV_W), wts, layer, tm, 1)
    x2 = _ffn(x1, wts, layer, tm)
    news = (pr["na"].reshape(nsamp, 1, H_A, 4 * DK_A), pr["nb"].reshape(nsamp, 1, H_B, 2 * DH_B),
            pr["kidx"].reshape(nsamp, 1, D_IDX), pr["nc"].reshape(nsamp, 1, H_C, 2 * DH_C),
            pr["logf"].reshape(nsamp, 1, H_C), conv2.reshape(nsamp, CONV_K - 1, CONV_W))
    return x2, news


def kernel(x_prompt, x_sample, cache_a_kv, cache_b_kv, cache_b_kidx, cache_c_kv, cache_c_logf, state_conv, page_table, w_in, b_fgate, lam_q1, lam_k1, lam_q2, lam_k2, g_diffnorm, conv_w, w_br_a, w_br_b, w_br_c, w_br_d, w_o, ln1_g, ln1_b, w_ffn_in, w_ffn_out, ln2_g, ln2_b):
    bsz, seq, _ = x_prompt.shape
    nsamp = x_sample.shape[0]
    depth = w_in.shape[0]
    bf = lambda a: a.astype(BF16)
    w_in_b = bf(w_in)
    wts = dict(
        w1=_rearranged_weights(w_in_b), wg=w_in_b[:, :, O_G:],
        aug_row=jnp.asarray(_build_q_aug_row()),
        bfg_row=jnp.zeros((depth, 1, LANES), F32).at[:, 0, MISC_CF:MISC_CF + H_C].set(b_fgate),
        lam_p=jnp.stack([lam_q1, lam_k1, lam_q2, lam_k2], axis=1),
        g=g_diffnorm, conv_w=conv_w, pa=bf(w_br_a), pb=bf(w_br_b), pc=bf(w_br_c), pd=bf(w_br_d), wo=bf(w_o),
        ln1_g=ln1_g[:, None, :], ln1_b=ln1_b[:, None, :], wf_in=bf(w_ffn_in), wf_out=bf(w_ffn_out),
        ln2_g=ln2_g[:, None, :], ln2_b=ln2_b[:, None, :],
        expand=jnp.asarray(_build_key_expansion(), dtype=BF16),
        w_idx=w_in[:, :, O_BIQ:O_BIQ + IDX_COLS])
    n_pool = cache_a_kv.shape[1]
    all_pages = depth * n_pool
    pfx = _page_prefix(cache_c_logf.reshape(all_pages, PKEYS), jnp.asarray(_build_prefix_matrix(), dtype=BF16))
    caches = (cache_a_kv.reshape(all_pages, PAGE_SIZE, N_HEADS, 4 * DK_A),
              cache_b_kv.reshape(all_pages, PAGE_SIZE, N_HEADS, 2 * DH_B),
              jnp.swapaxes(cache_b_kidx, 2, 3).reshape(all_pages, D_IDX, PAGE_SIZE),
              cache_c_kv.reshape(all_pages, PAGE_SIZE, N_HEADS, 2 * DH_C),
              pfx.reshape(all_pages, 1, 2 * PKEYS))

    yp = x_prompt.reshape(bsz * seq, D_MODEL)
    ys = x_sample.reshape(nsamp, D_MODEL)
    news_p, news_s = [], []
    leaf_bufs = "new"
    for l in range(depth):
        lam_init = 0.8 - 0.6 * math.exp(-0.3 * l)
        yp, new_p, leaf_bufs = _prompt_layer(yp, wts, l, depth, bsz, seq, lam_init, leaf_bufs)
        ys, new_s = _sample_layer(ys, wts, l, caches, l * n_pool, state_conv[l], page_table, lam_init)
        news_p.append(new_p)
        news_s.append(new_s)

    def stack(lst, i):
        return jnp.stack([e[i] for e in lst])

    kv_p = [b.reshape(depth, bsz, seq, N_HEADS, b.shape[-1]) for b in leaf_bufs]
    return (yp.reshape(bsz, seq, D_MODEL), ys.reshape(nsamp, 1, D_MODEL),
            kv_p[0], stack(news_s, 0), kv_p[1], stack(news_s, 1),
            stack(news_p, 0), stack(news_s, 2), kv_p[2], stack(news_s, 3),
            stack(news_p, 1), stack(news_s, 4), stack(news_p, 2), stack(news_s, 5))
```

```python
import functools
import math

import numpy as np
import jax
import jax.numpy as jnp
from jax import lax
from jax.experimental import pallas as pl
from jax.experimental.pallas import tpu as pltpu

F32 = jnp.float32
BF16 = jnp.bfloat16
NEG_INF = float("-inf")

D_MODEL = 1024
DEPTH = 2
PAGE_SIZE = 128
H_A, DK_A = 4, 64
A_W = H_A * 2 * DK_A
H_B, DH_B = 4, 64
B_W = H_B * DH_B
H_IDX, D_IDX = 8, 32
INDEX_TOPK_MAX = 256
H_C, DH_C = 4, 64
C_W = H_C * DH_C
CONV_W, CONV_K = 256, 3
N_BRANCH = 4
D_FF = -(-8 * D_MODEL // (3 * 256)) * 256
ALPHA = (2 * DEPTH) ** 0.25
LN_EPS = 1e-5
NORM_EPS = 1e-6
N_HEADS = 4

_n = H_A + H_B
_S_ALL = 2.0 ** (-8.0 * (np.arange(_n) + 1) / _n)
SLOPES_A = [float(v) for v in _S_ALL[0::2]]
SLOPES_B = [float(v) for v in _S_ALL[1::2]]

LANES = 128
SUBLANES = 8
VMEM_LIMIT_BYTES = 56 * 1024 * 1024
PKEYS = PAGE_SIZE * N_HEADS

_WIDTHS = (A_W, A_W, A_W, B_W, B_W, B_W, H_IDX * D_IDX, D_IDX, H_IDX,
           C_W, C_W, C_W, H_C, CONV_W, CONV_W, CONV_W, N_BRANCH * D_MODEL)
_OFF = np.concatenate([[0], np.cumsum(_WIDTHS)]).astype(np.int64)
(O_AQ, O_AK, O_AV, O_BQ, O_BK, O_BV, O_BIQ, O_BIK, O_BIW, O_CQ, O_CK, O_CV, O_CF,
 O_DB, O_DC, O_DH, O_G, IN_WIDTH) = [int(v) for v in _OFF]

SEG = {}
_pos = 0
for _name, _w in (("qa", 1024), ("na", 1024), ("qb", 512), ("nb", 512), ("iq", 256), ("krep", 256),
                  ("qc", 512), ("nc", 512), ("dconv", 768), ("kidx", 128), ("misc", 128)):
    SEG[_name] = (_pos, _pos + _w)
    _pos += _w
W1_WIDTH = _pos
IDX_COLS = H_IDX * D_IDX + D_IDX + H_IDX
MISC_CF = 0
MISC_WI = 8
QA_AUG = 128
QBC_AUG = 64


def _build_w1_columns():
    src = -np.ones((W1_WIDTH,), np.int64)
    scale = np.ones((W1_WIDTH,), np.float32)
    s = SEG["qa"][0]
    for h in range(H_A):
        src[s + h * 256: s + h * 256 + 128] = O_AQ + h * 128 + np.arange(128)
    scale[SEG["qa"][0]:SEG["qa"][1]] = DK_A ** -0.5
    s = SEG["na"][0]
    for h in range(H_A):
        src[s + h * 256: s + h * 256 + 128] = O_AK + h * 128 + np.arange(128)
        src[s + h * 256 + 128: s + (h + 1) * 256] = O_AV + h * 128 + np.arange(128)
    for nm, oq, ok, ov in (("b", O_BQ, O_BK, O_BV), ("c", O_CQ, O_CK, O_CV)):
        s = SEG["q" + nm][0]
        for h in range(4):
            src[s + h * 128: s + h * 128 + 64] = oq + h * 64 + np.arange(64)
        scale[SEG["q" + nm][0]:SEG["q" + nm][1]] = 64 ** -0.5
        s = SEG["n" + nm][0]
        for h in range(4):
            src[s + h * 128: s + h * 128 + 64] = ok + h * 64 + np.arange(64)
            src[s + h * 128 + 64: s + (h + 1) * 128] = ov + h * 64 + np.arange(64)
    s = SEG["iq"][0]
    src[s:s + 256] = O_BIQ + np.arange(256)
    s = SEG["krep"][0]
    for h in range(H_IDX):
        src[s + h * 32: s + (h + 1) * 32] = O_BIK + np.arange(32)
    s = SEG["dconv"][0]
    src[s:s + 768] = O_DB + np.arange(768)
    s = SEG["kidx"][0]
    src[s:s + 32] = O_BIK + np.arange(32)
    s = SEG["misc"][0]
    src[s + MISC_CF: s + MISC_CF + H_C] = O_CF + np.arange(H_C)
    src[s + MISC_WI: s + MISC_WI + H_IDX] = O_BIW + np.arange(H_IDX)
    scale[s + MISC_WI: s + MISC_WI + H_IDX] = H_IDX ** -0.5 * D_IDX ** -0.5
    return src, scale


_W1_SRC, _W1_SCALE = _build_w1_columns()


def _w1_runs():
    runs = []
    i = 0
    while i < W1_WIDTH:
        j = i + 1
        while (j < W1_WIDTH and _W1_SCALE[j] == _W1_SCALE[i]
               and ((_W1_SRC[i] < 0 and _W1_SRC[j] < 0)
                    or (_W1_SRC[i] >= 0 and _W1_SRC[j] == _W1_SRC[i] + (j - i)))):
            j += 1
        runs.append((int(_W1_SRC[i]), j - i, float(_W1_SCALE[i])))
        i = j
    return runs


def _rearranged_weights(w_in_bf16):
    parts = []
    for start, width, scale in _w1_runs():
        if start < 0:
            parts.append(jnp.zeros(w_in_bf16.shape[:2] + (width,), BF16))
        else:
            piece = w_in_bf16[:, :, start:start + width]
            parts.append(piece if scale == 1.0 else piece * jnp.asarray(scale, BF16))
    return jnp.concatenate(parts, axis=2)


def _build_q_aug_row():
    row = np.zeros((1, W1_WIDTH), np.float32)
    for h in range(4):
        a = SEG["qa"][0] + h * 256 + QA_AUG
        row[0, a], row[0, a + 1] = SLOPES_A[h], SLOPES_A[h] * 256.0
        b = SEG["qb"][0] + h * 128 + QBC_AUG
        row[0, b], row[0, b + 1] = SLOPES_B[h], SLOPES_B[h] * 256.0
        c = SEG["qc"][0] + h * 128 + QBC_AUG
        row[0, c:c + 3] = 1.0
    return row


def _cparams(*sem):
    return pltpu.CompilerParams(dimension_semantics=sem, vmem_limit_bytes=VMEM_LIMIT_BYTES)


def _const_spec(shape):
    nd = len(shape)
    return pl.BlockSpec(shape, lambda *_: (0,) * nd, pipeline_mode=pl.Buffered(1))


def _layer_spec(shape, layer):
    nd = len(shape)
    return pl.BlockSpec((None,) + tuple(shape), lambda *_: (layer,) + (0,) * nd,
                        pipeline_mode=pl.Buffered(1))


def _dot(a, b):
    return jnp.dot(a, b, preferred_element_type=F32)


def _dot_nt(a, b):
    return lax.dot_general(a, b, (((1,), (1,)), ((), ())), preferred_element_type=F32)


def _row_tile(n, pref):
    t = min(n, pref)
    while n % t:
        t //= 2
    return t


def _split3(x):
    hi = x.astype(BF16)
    r = x - hi.astype(F32)
    mid = r.astype(BF16)
    lo = (r - mid.astype(F32)).astype(BF16)
    return hi, mid, lo


def _log_sigmoid(x):
    return jnp.minimum(x, 0.0) - jnp.log1p(jnp.exp(-jnp.abs(x)))


def _layernorm(h, g, b):
    mu = jnp.mean(h, axis=-1, keepdims=True)
    d = h - mu
    var = jnp.mean(d * d, axis=-1, keepdims=True)
    return d * lax.rsqrt(var + LN_EPS) * g + b


def _lambda(lam_ref, lam_init):
    lam_p = lam_ref[...]
    return (jnp.exp(jnp.sum(lam_p[0:1] * lam_p[1:2], axis=1, keepdims=True))
            - jnp.exp(jnp.sum(lam_p[2:3] * lam_p[3:4], axis=1, keepdims=True)) + lam_init)


def _for_causal_extent(qi, tq, seq, body, tiles_per_extent=1):
    n_tiles = seq // tq
    step = min(tiles_per_extent, n_tiles)
    for c in range(n_tiles // step):
        pl.when(qi // step == c)(functools.partial(body, (c + 1) * step * tq))


def _diag_mask(tq):
    return (lax.broadcasted_iota(jnp.int32, (tq, tq), 1) <= lax.broadcasted_iota(jnp.int32, (tq, tq), 0))


def _mask_diag(s, diag):
    tq = diag.shape[0]
    width = s.shape[1]
    tail = jnp.where(diag, s[:, width - tq:], NEG_INF)
    return tail if width == tq else jnp.concatenate([s[:, :width - tq], tail], axis=1)


def _proj_kernel(*refs, n_alias, leaves):
    x_ref, w_ref, aug_ref, bfg_ref = refs[:4]
    (qa_ref, na_ref, qb_ref, nb_ref, iq_ref, kr_ref, qc_ref, nc_ref, dc_ref, misc_ref, kidx_ref,
     logf_ref) = refs[4 + n_alias:16 + n_alias]
    leaf_refs = refs[16 + n_alias:]
    xb = x_ref[...].astype(BF16)

    def seg(name):
        lo, hi = SEG[name]
        return _dot(xb, w_ref[:, lo:hi])

    def qseg(name):
        lo, hi = SEG[name]
        return (seg(name) + aug_ref[:, lo:hi]).astype(BF16)

    news = {"na": seg("na"), "nb": seg("nb"), "nc": seg("nc")}
    qa_ref[...] = qseg("qa")
    na_ref[...] = news["na"]
    qb_ref[...] = qseg("qb")
    nb_ref[...] = news["nb"]
    iq_ref[...] = seg("iq").astype(BF16)
    kr_ref[...] = seg("krep").astype(BF16)
    qc_ref[...] = qseg("qc")
    nc_ref[...] = news["nc"]
    for name, ref in zip(leaves, leaf_refs):
        v = news[name]
        ref[...] = v.reshape(v.shape[0], N_HEADS, v.shape[1] // N_HEADS)
    dc_ref[...] = seg("dconv")
    kidx_ref[...] = seg("kidx")[:, :D_IDX]
    misc = seg("misc")
    misc_ref[...] = misc
    logf_ref[...] = _log_sigmoid(misc + bfg_ref[...])[:, MISC_CF:MISC_CF + H_C]


LEAF_NAMES = ("na", "nb", "nc")


def _proj(x2d, wts, layer, tm, leaf_bufs=None, depth=None):
    n = x2d.shape[0]
    widths = dict((k, v[1] - v[0]) for k, v in SEG.items())
    outs = [("qa", BF16, widths["qa"]), ("na", F32, widths["na"]), ("qb", BF16, widths["qb"]),
            ("nb", F32, widths["nb"]), ("iq", BF16, widths["iq"]), ("krep", BF16, widths["krep"]),
            ("qc", BF16, widths["qc"]), ("nc", F32, widths["nc"]), ("dconv", F32, widths["dconv"]),
            ("misc", F32, widths["misc"]), ("kidx", F32, D_IDX), ("logf", F32, H_C)]
    in_specs = [pl.BlockSpec((tm, D_MODEL), lambda i: (i, 0)),
                _layer_spec((D_MODEL, W1_WIDTH), layer),
                _const_spec((1, W1_WIDTH)),
                _layer_spec((1, LANES), layer)]
    args = [x2d, wts["w1"], wts["aug_row"], wts["bfg_row"]]
    out_specs = [pl.BlockSpec((tm, w), lambda i: (i, 0)) for _, _, w in outs]
    out_shape = [jax.ShapeDtypeStruct((n, w), dt) for _, dt, w in outs]
    leaves, aliases, n_alias = (), {}, 0
    if leaf_bufs is not None:
        leaves = LEAF_NAMES
        for k, name in enumerate(leaves):
            hw = widths[name] // N_HEADS
            out_specs.append(pl.BlockSpec((None, tm, N_HEADS, hw), lambda i: (layer, i, 0, 0)))
            out_shape.append(jax.ShapeDtypeStruct((depth, n, N_HEADS, hw), F32))
            if leaf_bufs != "new":
                in_specs.append(pl.BlockSpec(memory_space=pl.ANY))
                args.append(leaf_bufs[k])
                aliases[len(args) - 1] = len(outs) + k
        n_alias = len(aliases)
    res = pl.pallas_call(
        functools.partial(_proj_kernel, n_alias=n_alias, leaves=leaves),
        grid=(n // tm,),
        in_specs=in_specs, out_specs=out_specs, out_shape=out_shape,
        input_output_aliases=aliases,
        compiler_params=_cparams("parallel"),
        name="proj",
    )(*args)
    out = dict(zip([o[0] for o in outs], res[:len(outs)]))
    out["leaf_bufs"] = tuple(res[len(outs):])
    return out


def _kth_largest_i16(vals, need):
    rows, width = vals.shape

    def count_ge(t):
        m = jnp.where(vals >= t.astype(jnp.int16), jnp.int16(1), jnp.int16(0))
        part = m[:, 0:LANES]
        for j in range(1, width // LANES):
            part = part + m[:, j * LANES:(j + 1) * LANES]
        return jnp.sum(part.astype(F32), axis=1, keepdims=True)

    t0 = jnp.where(count_ge(jnp.zeros((rows, 1), jnp.int32)) >= need, jnp.int32(0), jnp.int32(-32768))

    def body(i, t):
        cand = t + jnp.left_shift(jnp.int32(1), jnp.int32(14) - i)
        return jnp.where(count_ge(cand) >= need, cand, t)

    return lax.fori_loop(0, 15, body, t0, unroll=3), count_ge


def _topk_select(score, k):
    rows, width = score.shape
    kf = jnp.float32(k)
    big = jnp.float32(3.0e38)

    def count(mask):
        return jnp.sum(jnp.where(mask, 1.0, 0.0), axis=1, keepdims=True)

    x = score * 0.5
    finite = x > NEG_INF
    short = count(finite) < kf
    lo0 = jnp.min(jnp.where(finite, x, big), axis=1, keepdims=True)
    hi0 = jnp.max(jnp.where(finite, x, -big), axis=1, keepdims=True)
    need = jnp.full((rows, 1), kf, F32)

    def active(lo, hi):
        return (hi > lo) & jnp.logical_not(short)

    def refine(state):
        lo, hi = state
        act = active(lo, hi)
        w = hi - lo
        up = jnp.where(w < 1e-30, jnp.float32(2.0 ** 64), jnp.float32(1.0))
        scale = 65533.0 / jnp.where(act, w * up, 65533.0)
        v = jnp.maximum(jnp.floor(((x - lo) * up) * scale), -1.0) - 32767.0
        v = jnp.where(x > hi, 32767.0, v)
        b, _ = _kth_largest_i16(v.astype(jnp.int32).astype(jnp.int16), need)
        inb = v == b.astype(F32)
        new_lo = jnp.min(jnp.where(inb, x, big), axis=1, keepdims=True)
        new_hi = jnp.max(jnp.where(inb, x, -big), axis=1, keepdims=True)
        return jnp.where(act, new_lo, lo), jnp.where(act, new_hi, hi)

    def unsettled(state):
        lo, hi = state
        return jnp.max(jnp.where(active(lo, hi), 1.0, 0.0)) > 0.0

    lo, _ = lax.while_loop(unsettled, refine, (lo0, hi0))
    t = jnp.where(short, NEG_INF, lo)
    gt = x > t
    eq = x == t
    n_gt = count(gt)
    need = kf - n_gt
    idx = lax.broadcasted_iota(jnp.int32, (rows, width), 1)
    nbits = max(1, int(math.ceil(math.log2(width))))

    surplus = jnp.where(t > NEG_INF, n_gt + count(eq) - kf, 0.0)
    tied = jnp.max(surplus) > 0.0

    def tie_cut():
        def ibody(i, j):
            cand = j + jnp.left_shift(jnp.int32(1), jnp.int32(nbits - 1) - i)
            return jnp.where(count(eq & (idx < cand)) < need, cand, j)

        return lax.fori_loop(0, nbits, ibody, jnp.zeros((rows, 1), jnp.int32))

    j = lax.cond(tied, tie_cut, lambda: jnp.full((rows, 1), width, jnp.int32))
    return gt | (eq & (idx <= j))


def _attn_a_kernel(qa_ref, na_ref, lam_ref, g_ref, o_ref, k1_s, k2_s, v_s, *, tq, seq, lam_init):
    qi = pl.program_id(1)

    @pl.when(qi == 0)
    def _():
        lane = lax.broadcasted_iota(jnp.int32, (seq, LANES), 1)
        kp = lax.broadcasted_iota(jnp.int32, (seq, LANES), 0)
        kaug = jnp.where(lane == 0, (kp & 255).astype(F32),
                         jnp.where(lane == 1, (kp >> 8).astype(F32), 0.0)).astype(BF16)
        for h in range(H_A):
            kk = na_ref[0, :, h * 256:h * 256 + 128]
            k1_s[h, :, 0:LANES] = jnp.where(lane < 64, kk, 0.0).astype(BF16)
            k1_s[h, :, LANES:2 * LANES] = kaug
            k2_s[h, :, 0:LANES] = jnp.where(lane >= 64, kk, 0.0).astype(BF16)
            k2_s[h, :, LANES:2 * LANES] = kaug
            v_s[h] = na_ref[0, :, h * 256 + 128:(h + 1) * 256].astype(BF16)

    lam = _lambda(lam_ref, lam_init)

    def body(width):
        diag = _diag_mask(tq)
        for h in range(H_A):
            qh = qa_ref[0, :, h * 256:(h + 1) * 256]
            s1 = _mask_diag(_dot_nt(qh, k1_s[h, 0:width, :]), diag)
            s2 = _mask_diag(_dot_nt(qh, k2_s[h, 0:width, :]), diag)
            e1 = jnp.exp(s1 - jnp.max(s1, axis=1, keepdims=True))
            e2 = jnp.exp(s2 - jnp.max(s2, axis=1, keepdims=True))
            r1 = 1.0 / jnp.sum(e1, axis=1, keepdims=True)
            r2 = lam / jnp.sum(e2, axis=1, keepdims=True)
            p = (e1 * r1 - e2 * r2).astype(BF16)
            o = _dot(p, v_s[h, 0:width, :])
            o = o * lax.rsqrt(jnp.mean(o * o, axis=1, keepdims=True) + NORM_EPS)
            o = o * g_ref[h:h + 1, :] * (1.0 - lam_init)
            o_ref[0, :, h * 128:(h + 1) * 128] = o.astype(BF16)

    _for_causal_extent(qi, tq, seq, body)


def _attn_a(qa, na, wts, layer, lam_init, tq):
    bsz, seq, _ = qa.shape
    return pl.pallas_call(
        functools.partial(_attn_a_kernel, tq=tq, seq=seq, lam_init=lam_init),
        grid=(bsz, seq // tq),
        in_specs=[pl.BlockSpec((1, tq, 1024), lambda b, i: (b, i, 0)),
                  pl.BlockSpec((1, seq, 1024), lambda b, i: (b, 0, 0)),
                  _layer_spec((4, DK_A), layer),
                  _layer_spec((H_A, 2 * DK_A), layer)],
        out_specs=pl.BlockSpec((1, tq, A_W), lambda b, i: (b, i, 0)),
        out_shape=jax.ShapeDtypeStruct((bsz, seq, A_W), BF16),
        scratch_shapes=[pltpu.VMEM((H_A, seq, 2 * LANES), BF16), pltpu.VMEM((H_A, seq, 2 * LANES), BF16),
                        pltpu.VMEM((H_A, seq, LANES), BF16)],
        compiler_params=_cparams("parallel", "arbitrary"),
        name="attn_a",
    )(qa, na, wts["lam_p"], wts["g"])


def _attn_c_kernel(qc_ref, nc_ref, misc_ref, bfg_ref, o_ref, k_s, v_s, *, tq, seq):
    qi = pl.program_id(1)

    @pl.when(qi == 0)
    def _():
        lane = lax.broadcasted_iota(jnp.int32, (seq, LANES), 1)
        ch = min(256, seq)
        tri = jnp.where(lax.broadcasted_iota(jnp.int32, (ch, ch), 0)
                        >= lax.broadcasted_iota(jnp.int32, (ch, ch), 1), 1.0, 0.0).astype(BF16)
        carry = jnp.zeros((1, LANES), F32)
        chunks = []
        for c in range(seq // ch):
            lf = _log_sigmoid(misc_ref[0, c * ch:(c + 1) * ch, :] + bfg_ref[...])
            hi, mid, lo = _split3(lf)
            fc = (_dot(tri, hi) + _dot(tri, mid)) + _dot(tri, lo) + carry
            carry = fc[ch - 1:ch, :]
            chunks.append(fc)
        fcum = jnp.concatenate(chunks, axis=0) if len(chunks) > 1 else chunks[0]
        for h in range(H_C):
            fh = jnp.broadcast_to(fcum[:, MISC_CF + h:MISC_CF + h + 1], (seq, LANES))
            hi, mid, lo = (v.astype(F32) for v in _split3(-fh))
            kv = nc_ref[0, :, h * 128:(h + 1) * 128]
            aug = jnp.where(lane == QBC_AUG, hi,
                            jnp.where(lane == QBC_AUG + 1, mid, jnp.where(lane == QBC_AUG + 2, lo, 0.0)))
            k_s[h] = jnp.where(lane < 64, kv, aug).astype(BF16)
            v_s[h] = kv.astype(BF16)

    def body(width):
        diag = _diag_mask(tq)
        for h in range(H_C):
            qh = qc_ref[0, :, h * 128:(h + 1) * 128]
            s = _mask_diag(_dot_nt(qh, k_s[h, 0:width, :]), diag)
            e = jnp.exp(s - jnp.max(s, axis=1, keepdims=True))
            r = 1.0 / jnp.sum(e, axis=1, keepdims=True)
            o = _dot(e.astype(BF16), v_s[h, 0:width, :]) * r
            o_ref[0, :, h * 64:(h + 1) * 64] = o[:, 64:128].astype(BF16)

    _for_causal_extent(qi, tq, seq, body)


def _attn_c(qc, nc, misc, wts, layer, tq):
    bsz, seq, _ = qc.shape
    return pl.pallas_call(
        functools.partial(_attn_c_kernel, tq=tq, seq=seq),
        grid=(bsz, seq // tq),
        in_specs=[pl.BlockSpec((1, tq, 512), lambda b, i: (b, i, 0)),
                  pl.BlockSpec((1, seq, 512), lambda b, i: (b, 0, 0)),
                  pl.BlockSpec((1, seq, LANES), lambda b, i: (b, 0, 0)),
                  _layer_spec((1, LANES), layer)],
        out_specs=pl.BlockSpec((1, tq, C_W), lambda b, i: (b, i, 0)),
        out_shape=jax.ShapeDtypeStruct((bsz, seq, C_W), BF16),
        scratch_shapes=[pltpu.VMEM((H_C, seq, LANES), BF16)] * 2,
        compiler_params=_cparams("parallel", "arbitrary"),
        name="attn_c",
    )(qc, nc, misc, wts["bfg_row"])


def _attn_b_kernel(qb_ref, nb_ref, iq_ref, kr_ref, misc_ref, o_ref, k_s, v_s, kr_s, *, tq, seq, topk):
    qi = pl.program_id(1)

    @pl.when(qi == 0)
    def _():
        lane = lax.broadcasted_iota(jnp.int32, (seq, LANES), 1)
        kp = lax.broadcasted_iota(jnp.int32, (seq, LANES), 0)
        aug = jnp.where(lane == QBC_AUG, (kp & 255).astype(F32),
                        jnp.where(lane == QBC_AUG + 1, (kp >> 8).astype(F32), 0.0))
        for h in range(H_B):
            kv = nb_ref[0, :, h * 128:(h + 1) * 128]
            k_s[h] = jnp.where(lane < 64, kv, aug).astype(BF16)
            v_s[h] = kv.astype(BF16)
        kr = kr_ref[0]
        klane = lax.broadcasted_iota(jnp.int32, (seq, 256), 1)
        for h in range(H_IDX):
            kr_s[h] = kr * jnp.where((klane >> 5) == h, 1.0, 0.0).astype(BF16)

    def body(width):
        rows = qi * tq + lax.broadcasted_iota(jnp.int32, (tq, width), 0)
        cols = lax.broadcasted_iota(jnp.int32, (tq, width), 1)
        causal = cols <= rows
        iq = iq_ref[0]
        score = None
        for h in range(H_IDX):
            rel = jnp.maximum(_dot_nt(iq, kr_s[h, 0:width, :]), 0.0) * misc_ref[0, :, MISC_WI + h:MISC_WI + h + 1]
            score = rel if score is None else score + rel
        sel = _topk_select(jnp.where(causal, score, NEG_INF), topk)
        bias = jnp.where(sel & causal, 0.0, NEG_INF)
        for h in range(H_B):
            qh = qb_ref[0, :, h * 128:(h + 1) * 128]
            s = _dot_nt(qh, k_s[h, 0:width, :]) + bias
            e = jnp.exp(s - jnp.max(s, axis=1, keepdims=True))
            r = 1.0 / jnp.sum(e, axis=1, keepdims=True)
            o = _dot(e.astype(BF16), v_s[h, 0:width, :]) * r
            o_ref[0, :, h * 64:(h + 1) * 64] = o[:, 64:128].astype(BF16)

    _for_causal_extent(qi, tq, seq, body, tiles_per_extent=2)


def _attn_b(qb, nb, iq, krep, misc, tq):
    bsz, seq, _ = qb.shape
    topk = min(INDEX_TOPK_MAX, seq // 4)
    return pl.pallas_call(
        functools.partial(_attn_b_kernel, tq=tq, seq=seq, topk=topk),
        grid=(bsz, seq // tq),
        in_specs=[pl.BlockSpec((1, tq, 512), lambda b, i: (b, i, 0)),
                  pl.BlockSpec((1, seq, 512), lambda b, i: (b, 0, 0)),
                  pl.BlockSpec((1, tq, 256), lambda b, i: (b, i, 0)),
                  pl.BlockSpec((1, seq, 256), lambda b, i: (b, 0, 0)),
                  pl.BlockSpec((1, tq, LANES), lambda b, i: (b, i, 0))],
        out_specs=pl.BlockSpec((1, tq, B_W), lambda b, i: (b, i, 0)),
        out_shape=jax.ShapeDtypeStruct((bsz, seq, B_W), BF16),
        scratch_shapes=[pltpu.VMEM((H_B, seq, LANES), BF16), pltpu.VMEM((H_B, seq, LANES), BF16),
                        pltpu.VMEM((H_IDX, seq, H_IDX * D_IDX), BF16)],
        compiler_params=_cparams("parallel", "arbitrary"),
        name="attn_b",
    )(qb, nb, iq, krep, misc)


def _merge_kernel(*refs, tm, sample, tiles_per_seq):
    if sample:
        (x_ref, oa_ref, ob_ref, oc_ref, dc_ref, st_ref, wg_ref, pa_ref, pb_ref, pc_ref, pd_ref,
         wo_ref, g_ref, b_ref, cw_ref, y_ref, conv_ref) = refs
    else:
        (x_ref, oa_ref, ob_ref, oc_ref, dc_ref, halo_ref, wg_ref, pa_ref, pb_ref, pc_ref, pd_ref,
         wo_ref, g_ref, b_ref, cw_ref, y_ref, conv_ref, uu_ref) = refs
    w0 = cw_ref[0:1, :]
    w1 = cw_ref[1:2, :]
    w2 = cw_ref[2:3, :]
    u = dc_ref[:, 256:512] * dc_ref[:, 512:768]
    if sample:
        s0 = st_ref[:, 0:256]
        s1 = st_ref[:, 256:512]
        y = s0 * w0 + s1 * w1 + u * w2
        conv_ref[:, 0:256] = s1
        conv_ref[:, 256:512] = u
    else:
        first = (pl.program_id(0) % tiles_per_seq) == 0
        uh = halo_ref[:, 256:512] * halo_ref[:, 512:768]
        uu_ref[0:SUBLANES, :] = jnp.where(first, 0.0, uh)
        uu_ref[SUBLANES:SUBLANES + tm, :] = u
        y = uu_ref[SUBLANES - 2:SUBLANES - 2 + tm, :] * w0 + uu_ref[SUBLANES - 1:SUBLANES - 1 + tm, :] * w1 + u * w2
        conv_ref[0] = uu_ref[SUBLANES + tm - 2:SUBLANES + tm, :]
    o_d = (dc_ref[:, 0:256] * y).astype(BF16)
    xv = x_ref[...]
    xb = xv.astype(BF16)
    mixed = None
    for i, (o, p_ref) in enumerate(((oa_ref[...], pa_ref), (ob_ref[...], pb_ref),
                                    (oc_ref[...], pc_ref), (o_d, pd_ref))):
        gate = jax.nn.sigmoid(_dot(xb, wg_ref[:, i * D_MODEL:(i + 1) * D_MODEL]))
        term = gate * _dot(o, p_ref[...])
        mixed = term if mixed is None else mixed + term
    h = ALPHA * xv + _dot(mixed.astype(BF16), wo_ref[...])
    y_ref[...] = _layernorm(h, g_ref[...], b_ref[...])


def _merge(x2d, oa, ob, oc, dconv, state2d, wts, layer, tm, seq):
    n = x2d.shape[0]
    sample = state2d is not None
    row = lambda w: pl.BlockSpec((tm, w), lambda i: (i, 0))
    in_specs = [row(D_MODEL), row(A_W), row(B_W), row(C_W), row(768)]
    args = [x2d, oa, ob, oc, dconv]
    if sample:
        in_specs.append(row(512))
        args.append(state2d)
        out_specs = [row(D_MODEL), row(512)]
        out_shape = [jax.ShapeDtypeStruct((n, D_MODEL), F32), jax.ShapeDtypeStruct((n, 512), F32)]
        scratch = []
        tiles_per_seq = 1
    else:
        tiles_per_seq = seq // tm
        r8 = tm // SUBLANES
        in_specs.append(pl.BlockSpec((SUBLANES, 768), lambda i: (jnp.maximum(i * r8 - 1, 0), 0)))
        args.append(dconv)
        out_specs = [row(D_MODEL), pl.BlockSpec((1, CONV_K - 1, CONV_W), lambda i: (i // tiles_per_seq, 0, 0))]
        out_shape = [jax.ShapeDtypeStruct((n, D_MODEL), F32),
                     jax.ShapeDtypeStruct((n // seq, CONV_K - 1, CONV_W), F32)]
        scratch = [pltpu.VMEM((tm + SUBLANES, CONV_W), F32)]
    for name in ("wg", "pa", "pb", "pc", "pd", "wo", "ln1_g", "ln1_b", "conv_w"):
        in_specs.append(_layer_spec(wts[name].shape[1:], layer))
        args.append(wts[name])
    return pl.pallas_call(
        functools.partial(_merge_kernel, tm=tm, sample=sample, tiles_per_seq=tiles_per_seq),
        grid=(n // tm,),
        in_specs=in_specs, out_specs=out_specs, out_shape=out_shape, scratch_shapes=scratch,
        compiler_params=_cparams("arbitrary"),
        name="merge_s" if sample else "merge",
    )(*args)


def _ffn_kernel(x_ref, wi_ref, wo_ref, g_ref, b_ref, y_ref):
    xv = x_ref[...]
    xb = xv.astype(BF16)
    hg = _dot(xb, wi_ref[:, 0:D_FF])
    hu = _dot(xb, wi_ref[:, D_FF:2 * D_FF])
    act = (hg * jax.nn.sigmoid(hg) * hu).astype(BF16)
    h = ALPHA * xv + _dot(act, wo_ref[...])
    y_ref[...] = _layernorm(h, g_ref[...], b_ref[...])


def _ffn(x2d, wts, layer, tm):
    n = x2d.shape[0]
    return pl.pallas_call(
        _ffn_kernel,
        grid=(n // tm,),
        in_specs=[pl.BlockSpec((tm, D_MODEL), lambda i: (i, 0)),
                  _layer_spec((D_MODEL, 2 * D_FF), layer), _layer_spec((D_FF, D_MODEL), layer),
                  _layer_spec((1, D_MODEL), layer), _layer_spec((1, D_MODEL), layer)],
        out_specs=pl.BlockSpec((tm, D_MODEL), lambda i: (i, 0)),
        out_shape=jax.ShapeDtypeStruct((n, D_MODEL), F32),
        compiler_params=_cparams("parallel"),
        name="ffn",
    )(x2d, wts["wf_in"], wts["wf_out"], wts["ln2_g"], wts["ln2_b"])


def _build_prefix_matrix():
    c = np.arange(PKEYS)
    same_head = (c[:, None] % N_HEADS) == (c[None, :] % N_HEADS)
    upto = (c[:, None] // N_HEADS) <= (c[None, :] // N_HEADS)
    return np.concatenate([same_head & upto, same_head], axis=1).astype(np.float32)


def _build_key_expansion():
    return (np.arange(PAGE_SIZE)[:, None] == (np.arange(PKEYS)[None, :] // N_HEADS)).astype(np.float32)


def _pfx_kernel(x_ref, u_ref, o_ref):
    hi, mid, lo = _split3(x_ref[...])
    u = u_ref[...]
    o_ref[...] = (_dot(hi, u) + _dot(mid, u)) + _dot(lo, u)


def _page_prefix(logf_pages, umat):
    n_pool = logf_pages.shape[0]
    tp = _row_tile(n_pool, 512) if n_pool % SUBLANES == 0 else n_pool
    return pl.pallas_call(
        _pfx_kernel,
        grid=(n_pool // tp,),
        in_specs=[pl.BlockSpec((tp, PKEYS), lambda i: (i, 0)),
                  _const_spec(umat.shape)],
        out_specs=pl.BlockSpec((tp, 2 * PKEYS), lambda i: (i, 0)),
        out_shape=jax.ShapeDtypeStruct((n_pool, 2 * PKEYS), F32),
        compiler_params=_cparams("parallel"),
        name="page_prefix",
    )(logf_pages, umat)


def _dot_split(a, b, dot):
    a0, a1, a2 = _split3(a)
    b0, b1, b2 = _split3(b)
    small = (dot(a0, b2) + dot(a2, b0)) + dot(a1, b1)
    return (small + (dot(a0, b1) + dot(a1, b0))) + dot(a0, b0)


def _idx_proj_kernel(x_ref, w_ref, o_ref):
    o_ref[...] = _dot_split(x_ref[...], w_ref[...], _dot)


def _idx_proj(x2d, w_idx, layer):
    return pl.pallas_call(
        _idx_proj_kernel,
        out_shape=jax.ShapeDtypeStruct((x2d.shape[0], w_idx.shape[2]), F32),
        compiler_params=pltpu.CompilerParams(vmem_limit_bytes=VMEM_LIMIT_BYTES),
        name="idx_proj",
    )(x2d, w_idx[layer])


def _idx_kernel(pt_ref, *refs, n_pages):
    pages = refs[:n_pages]
    iq_ref, wi_ref, knew_ref, o_ref = refs[n_pages:]
    iq = iq_ref[0]
    wi = wi_ref[0] * (H_IDX ** -0.5 * D_IDX ** -0.5)

    def weighted(rel):
        return jnp.sum(jnp.maximum(rel, 0.0) * wi, axis=0, keepdims=True)

    kt = jnp.concatenate([pages[j][0] for j in range(n_pages)], axis=1)
    o_ref[0, :, 0:n_pages * PAGE_SIZE] = weighted(_dot_split(iq, kt, _dot))
    knew = jnp.broadcast_to(knew_ref[0], (PAGE_SIZE, D_IDX))
    lane = lax.broadcasted_iota(jnp.int32, (1, PAGE_SIZE), 1)
    o_ref[0, :, n_pages * PAGE_SIZE:(n_pages + 1) * PAGE_SIZE] = jnp.where(
        lane == 0, weighted(_dot_split(iq, knew, _dot_nt)), NEG_INF)


def _idx_scores(page_table, kidx_t, page_base, iq8, wi8, knew):
    nsamp, n_pages = page_table.shape
    width = (n_pages + 1) * PAGE_SIZE

    def page_spec(j):
        return pl.BlockSpec((1, D_IDX, PAGE_SIZE), lambda s, pt: (page_base + pt[s, j], 0, 0))

    grid_spec = pltpu.PrefetchScalarGridSpec(
        num_scalar_prefetch=1,
        grid=(nsamp,),
        in_specs=[page_spec(j) for j in range(n_pages)] + [
            pl.BlockSpec((1, H_IDX, D_IDX), lambda s, pt: (s, 0, 0)),
            pl.BlockSpec((1, H_IDX, 1), lambda s, pt: (s, 0, 0)),
            pl.BlockSpec((1, 1, D_IDX), lambda s, pt: (s, 0, 0))],
        out_specs=pl.BlockSpec((1, 1, width), lambda s, pt: (s, 0, 0)),
    )
    out = pl.pallas_call(
        functools.partial(_idx_kernel, n_pages=n_pages),
        grid_spec=grid_spec,
        out_shape=jax.ShapeDtypeStruct((nsamp, 1, width), F32),
        compiler_params=_cparams("arbitrary"),
        name="idx_scores",
    )(page_table, *([kidx_t] * n_pages), iq8, wi8, knew)
    return out.reshape(nsamp, width)


def _sel_kernel(s_ref, e_ref, o_ref, *, topk, n_blocks):
    sel = jnp.where(_topk_select(s_ref[...], topk), 1.0, 0.0).astype(BF16)
    for j in range(n_blocks):
        rep = _dot(sel[:, j * PAGE_SIZE:(j + 1) * PAGE_SIZE], e_ref[...])
        o_ref[:, j * PKEYS:(j + 1) * PKEYS] = jnp.where(rep > 0.5, 0.0, NEG_INF)


def _select_bias(scores, expand, topk):
    nsamp, width = scores.shape
    n_blocks = width // PAGE_SIZE
    return pl.pallas_call(
        functools.partial(_sel_kernel, topk=topk, n_blocks=n_blocks),
        out_shape=jax.ShapeDtypeStruct((nsamp, n_blocks * PKEYS), F32),
        compiler_params=pltpu.CompilerParams(vmem_limit_bytes=VMEM_LIMIT_BYTES),
        name="select_bias",
    )(scores, expand)


def _attn_s_kernel(pt_ref, *refs, n_pages, lam_init):
    g = n_pages
    ca = refs[0:g]
    cb = refs[g:2 * g]
    cc = refs[2 * g:3 * g]
    pf = refs[3 * g:4 * g]
    (selp_ref, seln_ref, qa_ref, qb_ref, qc_ref, na_ref, nb_ref, nc_ref, lfn_ref, lam_ref, g_ref,
     oa_ref, ob_ref, oc_ref, xa_s, xb_s, xc_s) = refs[4 * g:]

    sub_a = lax.broadcasted_iota(jnp.int32, (SUBLANES, 256), 0)
    lane_a = lax.broadcasted_iota(jnp.int32, (SUBLANES, 256), 1)
    qa_row = qa_ref[0].astype(F32)
    qa_blk = jnp.zeros((SUBLANES, 256), F32)
    for h in range(N_HEADS):
        qa_blk = jnp.where((sub_a >> 1) == h, jnp.broadcast_to(qa_row[:, h * 256:(h + 1) * 256], (SUBLANES, 256)),
                           qa_blk)
    qa_blk = jnp.where((lane_a < 128) & ((lane_a >> 6) == (sub_a & 1)), qa_blk, 0.0).astype(BF16)
    sub_b = lax.broadcasted_iota(jnp.int32, (SUBLANES, LANES), 0)
    lane_b = lax.broadcasted_iota(jnp.int32, (SUBLANES, LANES), 1)

    def head_block(row):
        blk = jnp.zeros((SUBLANES, LANES), F32)
        for h in range(N_HEADS):
            blk = jnp.where((sub_b & 3) == h, jnp.broadcast_to(row[:, h * 128:(h + 1) * 128], (SUBLANES, LANES)), blk)
        return jnp.where(lane_b < 64, blk, 0.0).astype(BF16)

    qb_blk = head_block(qb_ref[0].astype(F32))
    qc_blk = head_block(qc_ref[0].astype(F32))

    sub1 = lax.broadcasted_iota(jnp.int32, (SUBLANES, 1), 0)
    slope_a = jnp.zeros((SUBLANES, 1), F32)
    slope_b = jnp.zeros((SUBLANES, 1), F32)
    for h in range(N_HEADS):
        slope_a = jnp.where((sub1 >> 1) == h, SLOPES_A[h], slope_a)
        slope_b = jnp.where((sub1 & 3) == h, SLOPES_B[h], slope_b)

    def head_masks(width):
        sub = lax.broadcasted_iota(jnp.int32, (SUBLANES, width), 0)
        lane = lax.broadcasted_iota(jnp.int32, (SUBLANES, width), 1)
        own_a = jnp.where((lane & 3) == (sub >> 1), 0.0, NEG_INF)
        own_b = jnp.where((lane & 3) == (sub & 3), 0.0, NEG_INF)
        return own_a, own_b

    def new_rows(x_s, ref):
        x_s[...] = jnp.zeros(x_s.shape, F32)
        x_s[0:N_HEADS, :] = ref[0]
        return x_s[...].astype(BF16)

    def attend(qblk, blocks, bias):
        s = jnp.concatenate([_dot_nt(qblk, blk) for blk in blocks], axis=1) + bias
        e = jnp.exp(s - jnp.max(s, axis=1, keepdims=True))
        r = 1.0 / jnp.sum(e, axis=1, keepdims=True)
        eb = e.astype(BF16)
        pv, off = None, 0
        for blk in blocks:
            t = _dot(eb[:, off:off + blk.shape[0]], blk)
            pv = t if pv is None else pv + t
            off += blk.shape[0]
        return pv * r

    new_pos = float(n_pages * PAGE_SIZE)
    own_a, own_b = head_masks(g * PKEYS)
    own_a1, own_b1 = head_masks(PAGE_SIZE)
    lane1 = lax.broadcasted_iota(jnp.int32, (1, PAGE_SIZE), 1)
    valid = jnp.where(lane1 < N_HEADS, 0.0, NEG_INF)
    lane_g = lax.broadcasted_iota(jnp.int32, (1, g * PKEYS), 1)
    kpos = (lane_g >> 2).astype(F32)

    blocks_a = [r[0].reshape(PKEYS, 256).astype(BF16) for r in ca] + [new_rows(xa_s, na_ref)]
    bias_a = jnp.concatenate([slope_a * kpos + own_a, slope_a * new_pos + own_a1 + valid], axis=1)
    acc_a = attend(qa_blk, blocks_a, bias_a)

    selb = jnp.concatenate([selp_ref[0, j] for j in range(g)], axis=1)
    blocks_b = [r[0].reshape(PKEYS, 128).astype(BF16) for r in cb] + [new_rows(xb_s, nb_ref)]
    bias_b = jnp.concatenate([slope_b * kpos + selb + own_b,
                              slope_b * new_pos + seln_ref[0, 0][:, 0:PAGE_SIZE] + own_b1 + valid], axis=1)
    acc_b = attend(qb_blk, blocks_b, bias_b)

    fparts = []
    carry = jnp.zeros((1, PKEYS), F32)
    for j in range(g):
        fparts.append(pf[j][0, :, 0:PKEYS] + carry)
        carry = carry + pf[j][0, :, PKEYS:2 * PKEYS]
    f_new = carry[:, 0:PAGE_SIZE] + lfn_ref[0]
    blocks_c = [r[0].reshape(PKEYS, 128).astype(BF16) for r in cc] + [new_rows(xc_s, nc_ref)]
    bias_c = jnp.concatenate([own_b - jnp.concatenate(fparts, axis=1), own_b1 + valid - f_new], axis=1)
    acc_c = attend(qc_blk, blocks_c, bias_c)

    lam = _lambda(lam_ref, lam_init)
    for h in range(N_HEADS):
        o = acc_a[2 * h:2 * h + 1, 128:256] - lam * acc_a[2 * h + 1:2 * h + 2, 128:256]
        o = o * lax.rsqrt(jnp.mean(o * o, axis=1, keepdims=True) + NORM_EPS)
        o = o * g_ref[h:h + 1, :] * (1.0 - lam_init)
        oa_ref[0, :, h * 128:(h + 1) * 128] = o.astype(BF16)
        ob_ref[0, :, h * 64:(h + 1) * 64] = acc_b[h:h + 1, 64:128].astype(BF16)
        oc_ref[0, :, h * 64:(h + 1) * 64] = acc_c[h:h + 1, 64:128].astype(BF16)


def _attn_sample(page_table, page_base, ca, cb, cc, pfx, selb4, qa, qb, qc, na, nb, nc, lfn, wts, layer, lam_init):
    nsamp, n_pages = page_table.shape

    def page_spec(width, j):
        return pl.BlockSpec((1, PAGE_SIZE, N_HEADS, width), lambda s, pt: (page_base + pt[s, j], 0, 0, 0))

    def pf_spec(j):
        return pl.BlockSpec((1, 1, 2 * PKEYS), lambda s, pt: (page_base + pt[s, j], 0, 0))

    def samp(shape):
        return pl.BlockSpec((1,) + shape, lambda s, pt: (s,) + (0,) * len(shape))

    pages = range(n_pages)
    in_specs = ([page_spec(256, j) for j in pages] + [page_spec(128, j) for j in pages]
                + [page_spec(128, j) for j in pages] + [pf_spec(j) for j in pages]
                + [pl.BlockSpec((1, n_pages, 1, PKEYS), lambda s, pt: (s, 0, 0, 0)),
                   pl.BlockSpec((1, 1, 1, PKEYS), lambda s, pt: (s, n_pages, 0, 0)),
                   samp((1, 1024)), samp((1, 512)), samp((1, 512)),
                   samp((N_HEADS, 256)), samp((N_HEADS, 128)), samp((N_HEADS, 128)), samp((1, PAGE_SIZE)),
                   pl.BlockSpec((None, 4, DK_A), lambda s, pt: (layer, 0, 0)),
                   pl.BlockSpec((None, H_A, 2 * DK_A), lambda s, pt: (layer, 0, 0))])
    out_specs = [samp((1, A_W)), samp((1, B_W)), samp((1, C_W))]
    grid_spec = pltpu.PrefetchScalarGridSpec(
        num_scalar_prefetch=1, grid=(nsamp,), in_specs=in_specs, out_specs=out_specs,
        scratch_shapes=[pltpu.VMEM((PAGE_SIZE, 256), F32), pltpu.VMEM((PAGE_SIZE, 128), F32),
                        pltpu.VMEM((PAGE_SIZE, 128), F32)])
    return pl.pallas_call(
        functools.partial(_attn_s_kernel, n_pages=n_pages, lam_init=lam_init),
        grid_spec=grid_spec,
        out_shape=[jax.ShapeDtypeStruct((nsamp, 1, A_W), BF16),
                   jax.ShapeDtypeStruct((nsamp, 1, B_W), BF16),
                   jax.ShapeDtypeStruct((nsamp, 1, C_W), BF16)],
        compiler_params=_cparams("arbitrary"),
        name="attn_sample",
    )(page_table, *([ca] * n_pages), *([cb] * n_pages), *([cc] * n_pages), *([pfx] * n_pages),
      selb4, selb4, qa, qb, qc, na, nb, nc, lfn, wts["lam_p"], wts["g"])


def _prompt_layer(x2d, wts, layer, depth, bsz, seq, lam_init, leaf_bufs):
    n = bsz * seq
    pr = _proj(x2d, wts, layer, _row_tile(n, 512), leaf_bufs=leaf_bufs, depth=depth)
    r3 = lambda a: a.reshape(bsz, seq, a.shape[-1])
    tq = _row_tile(seq, 256)
    oa = _attn_a(r3(pr["qa"]), r3(pr["na"]), wts, layer, lam_init, tq)
    ob = _attn_b(r3(pr["qb"]), r3(pr["nb"]), r3(pr["iq"]), r3(pr["krep"]), r3(pr["misc"]), tq)
    oc = _attn_c(r3(pr["qc"]), r3(pr["nc"]), r3(pr["misc"]), wts, layer, tq)
    x1, new_conv = _merge(x2d, oa.reshape(n, A_W), ob.reshape(n, B_W), oc.reshape(n, C_W), pr["dconv"],
                          None, wts, layer, _row_tile(seq, 256), seq)
    x2 = _ffn(x1, wts, layer, _row_tile(n, 256))
    news = (pr["kidx"].reshape(bsz, seq, D_IDX), pr["logf"].reshape(bsz, seq, H_C), new_conv)
    return x2, news, pr["leaf_bufs"]


def _sample_layer(x2d, wts, layer, caches, page_base, state, page_table, lam_init):
    nsamp = x2d.shape[0]
    n_pages = page_table.shape[1]
    ca, cb, ckt, cc, pfx = caches
    tm = _row_tile(nsamp, 128)
    pr = _proj(x2d, wts, layer, tm)
    zi = _idx_proj(x2d, wts["w_idx"], layer)
    iq8 = zi[:, 0:H_IDX * D_IDX].reshape(nsamp, H_IDX, D_IDX)
    knew = zi[:, H_IDX * D_IDX:H_IDX * D_IDX + D_IDX].reshape(nsamp, 1, D_IDX)
    wi8 = zi[:, H_IDX * D_IDX + D_IDX:IDX_COLS].reshape(nsamp, H_IDX, 1)
    scores = _idx_scores(page_table, ckt, page_base, iq8, wi8, knew)
    topk = min(INDEX_TOPK_MAX, (n_pages * PAGE_SIZE + 1) // 4)
    selb4 = _select_bias(scores, wts["expand"], topk).reshape(nsamp, n_pages + 1, 1, PKEYS)
    lfn = jnp.pad(pr["logf"], ((0, 0), (0, PAGE_SIZE - H_C))).reshape(nsamp, 1, PAGE_SIZE)
    r3 = lambda a: a.reshape(nsamp, 1, a.shape[-1])
    r4 = lambda a: a.reshape(nsamp, N_HEADS, a.shape[-1] // N_HEADS)
    oa, ob, oc = _attn_sample(page_table, page_base, ca, cb, cc, pfx, selb4,
                              r3(pr["qa"]), r3(pr["qb"]), r3(pr["qc"]), r4(pr["na"]), r4(pr["nb"]), r4(pr["nc"]),
                              lfn, wts, layer, lam_init)
    x1, conv2 = _merge(x2d, oa.reshape(nsamp, A_W), ob.reshape(nsamp, B_W), oc.reshape(nsamp, C_W),
                       pr["dconv"], state.reshape(nsamp, (CONV_K - 1) * CONV_W), wts, layer, tm, 1)
    x2 = _ffn(x1, wts, layer, tm)
    news = (pr["na"].reshape(nsamp, 1, H_A, 4 * DK_A), pr["nb"].reshape(nsamp, 1, H_B, 2 * DH_B),
            pr["kidx"].reshape(nsamp, 1, D_IDX), pr["nc"].reshape(nsamp, 1, H_C, 2 * DH_C),
            pr["logf"].reshape(nsamp, 1, H_C), conv2.reshape(nsamp, CONV_K - 1, CONV_W))
    return x2, news


def kernel(x_prompt, x_sample, cache_a_kv, cache_b_kv, cache_b_kidx, cache_c_kv, cache_c_logf, state_conv, page_table, w_in, b_fgate, lam_q1, lam_k1, lam_q2, lam_k2, g_diffnorm, conv_w, w_br_a, w_br_b, w_br_c, w_br_d, w_o, ln1_g, ln1_b, w_ffn_in, w_ffn_out, ln2_g, ln2_b):
    bsz, seq, _ = x_prompt.shape
    nsamp = x_sample.shape[0]
    depth = w_in.shape[0]
    bf = lambda a: a.astype(BF16)
    w_in_b = bf(w_in)
    wts = dict(
        w1=_rearranged_weights(w_in_b), wg=w_in_b[:, :, O_G:],
        aug_row=jnp.asarray(_build_q_aug_row()),
        bfg_row=jnp.zeros((depth, 1, LANES), F32).at[:, 0, MISC_CF:MISC_CF + H_C].set(b_fgate),
        lam_p=jnp.stack([lam_q1, lam_k1, lam_q2, lam_k2], axis=1),
        g=g_diffnorm, conv_w=conv_w, pa=bf(w_br_a), pb=bf(w_br_b), pc=bf(w_br_c), pd=bf(w_br_d), wo=bf(w_o),
        ln1_g=ln1_g[:, None, :], ln1_b=ln1_b[:, None, :], wf_in=bf(w_ffn_in), wf_out=bf(w_ffn_out),
        ln2_g=ln2_g[:, None, :], ln2_b=ln2_b[:, None, :],
        expand=jnp.asarray(_build_key_expansion(), dtype=BF16),
        w_idx=w_in[:, :, O_BIQ:O_BIQ + IDX_COLS])
    n_pool = cache_a_kv.shape[1]
    all_pages = depth * n_pool
    pfx = _page_prefix(cache_c_logf.reshape(all_pages, PKEYS), jnp.asarray(_build_prefix_matrix(), dtype=BF16))
    caches = (cache_a_kv.reshape(all_pages, PAGE_SIZE, N_HEADS, 4 * DK_A),
              cache_b_kv.reshape(all_pages, PAGE_SIZE, N_HEADS, 2 * DH_B),
              jnp.swapaxes(cache_b_kidx, 2, 3).reshape(all_pages, D_IDX, PAGE_SIZE),
              cache_c_kv.reshape(all_pages, PAGE_SIZE, N_HEADS, 2 * DH_C),
              pfx.reshape(all_pages, 1, 2 * PKEYS))

    yp = x_prompt.reshape(bsz * seq, D_MODEL)
    ys = x_sample.reshape(nsamp, D_MODEL)
    news_p, news_s = [], []
    leaf_bufs = "new"
    for l in range(depth):
        lam_init = 0.8 - 0.6 * math.exp(-0.3 * l)
        yp, new_p, leaf_bufs = _prompt_layer(yp, wts, l, depth, bsz, seq, lam_init, leaf_bufs)
        ys, new_s = _sample_layer(ys, wts, l, caches, l * n_pool, state_conv[l], page_table, lam_init)
        news_p.append(new_p)
        news_s.append(new_s)

    def stack(lst, i):
        return jnp.stack([e[i] for e in lst])

    kv_p = [b.reshape(depth, bsz, seq, N_HEADS, b.shape[-1]) for b in leaf_bufs]
    return (yp.reshape(bsz, seq, D_MODEL), ys.reshape(nsamp, 1, D_MODEL),
            kv_p[0], stack(news_s, 0), kv_p[1], stack(news_s, 1),
            stack(news_p, 0), stack(news_s, 2), kv_p[2], stack(news_s, 3),
            stack(news_p, 1), stack(news_s, 4), stack(news_p, 2), stack(news_s, 5))
```

```python
import functools
import math

import numpy as np
import jax
import jax.numpy as jnp
from jax import lax
from jax.experimental import pallas as pl
from jax.experimental.pallas import tpu as pltpu

F32 = jnp.float32
BF16 = jnp.bfloat16
NEG_INF = float("-inf")

D_MODEL = 1024
DEPTH = 2
PAGE_SIZE = 128
H_A, DK_A = 4, 64
A_W = H_A * 2 * DK_A
H_B, DH_B = 4, 64
B_W = H_B * DH_B
H_IDX, D_IDX = 8, 32
INDEX_TOPK_MAX = 256
H_C, DH_C = 4, 64
C_W = H_C * DH_C
CONV_W, CONV_K = 256, 3
N_BRANCH = 4
D_FF = -(-8 * D_MODEL // (3 * 256)) * 256
ALPHA = (2 * DEPTH) ** 0.25
LN_EPS = 1e-5
NORM_EPS = 1e-6
N_HEADS = 4

_n = H_A + H_B
_S_ALL = 2.0 ** (-8.0 * (np.arange(_n) + 1) / _n)
SLOPES_A = [float(v) for v in _S_ALL[0::2]]
SLOPES_B = [float(v) for v in _S_ALL[1::2]]

LANES = 128
SUBLANES = 8
VMEM_LIMIT_BYTES = 56 * 1024 * 1024
PKEYS = PAGE_SIZE * N_HEADS

_WIDTHS = (A_W, A_W, A_W, B_W, B_W, B_W, H_IDX * D_IDX, D_IDX, H_IDX,
           C_W, C_W, C_W, H_C, CONV_W, CONV_W, CONV_W, N_BRANCH * D_MODEL)
_OFF = np.concatenate([[0], np.cumsum(_WIDTHS)]).astype(np.int64)
(O_AQ, O_AK, O_AV, O_BQ, O_BK, O_BV, O_BIQ, O_BIK, O_BIW, O_CQ, O_CK, O_CV, O_CF,
 O_DB, O_DC, O_DH, O_G, IN_WIDTH) = [int(v) for v in _OFF]

SEG = {}
_pos = 0
for _name, _w in (("qa", 1024), ("na", 1024), ("qb", 512), ("nb", 512), ("iq", 256), ("krep", 256),
                  ("qc", 512), ("nc", 512), ("dconv", 768), ("kidx", 128), ("misc", 128)):
    SEG[_name] = (_pos, _pos + _w)
    _pos += _w
W1_WIDTH = _pos
IDX_COLS = H_IDX * D_IDX + D_IDX + H_IDX
MISC_CF = 0
MISC_WI = 8
QA_AUG = 128
QBC_AUG = 64


def _build_w1_columns():
    src = -np.ones((W1_WIDTH,), np.int64)
    scale = np.ones((W1_WIDTH,), np.float32)
    s = SEG["qa"][0]
    for h in range(H_A):
        src[s + h * 256: s + h * 256 + 128] = O_AQ + h * 128 + np.arange(128)
    scale[SEG["qa"][0]:SEG["qa"][1]] = DK_A ** -0.5
    s = SEG["na"][0]
    for h in range(H_A):
        src[s + h * 256: s + h * 256 + 128] = O_AK + h * 128 + np.arange(128)
        src[s + h * 256 + 128: s + (h + 1) * 256] = O_AV + h * 128 + np.arange(128)
    for nm, oq, ok, ov in (("b", O_BQ, O_BK, O_BV), ("c", O_CQ, O_CK, O_CV)):
        s = SEG["q" + nm][0]
        for h in range(4):
            src[s + h * 128: s + h * 128 + 64] = oq + h * 64 + np.arange(64)
        scale[SEG["q" + nm][0]:SEG["q" + nm][1]] = 64 ** -0.5
        s = SEG["n" + nm][0]
        for h in range(4):
            src[s + h * 128: s + h * 128 + 64] = ok + h * 64 + np.arange(64)
            src[s + h * 128 + 64: s + (h + 1) * 128] = ov + h * 64 + np.arange(64)
    s = SEG["iq"][0]
    src[s:s + 256] = O_BIQ + np.arange(256)
    s = SEG["krep"][0]
    for h in range(H_IDX):
        src[s + h * 32: s + (h + 1) * 32] = O_BIK + np.arange(32)
    s = SEG["dconv"][0]
    src[s:s + 768] = O_DB + np.arange(768)
    s = SEG["kidx"][0]
    src[s:s + 32] = O_BIK + np.arange(32)
    s = SEG["misc"][0]
    src[s + MISC_CF: s + MISC_CF + H_C] = O_CF + np.arange(H_C)
    src[s + MISC_WI: s + MISC_WI + H_IDX] = O_BIW + np.arange(H_IDX)
    scale[s + MISC_WI: s + MISC_WI + H_IDX] = H_IDX ** -0.5 * D_IDX ** -0.5
    return src, scale


_W1_SRC, _W1_SCALE = _build_w1_columns()


def _w1_runs():
    runs = []
    i = 0
    while i < W1_WIDTH:
        j = i + 1
        while (j < W1_WIDTH and _W1_SCALE[j] == _W1_SCALE[i]
               and ((_W1_SRC[i] < 0 and _W1_SRC[j] < 0)
                    or (_W1_SRC[i] >= 0 and _W1_SRC[j] == _W1_SRC[i] + (j - i)))):
            j += 1
        runs.append((int(_W1_SRC[i]), j - i, float(_W1_SCALE[i])))
        i = j
    return runs


def _rearranged_weights(w_in_bf16):
    parts = []
    for start, width, scale in _w1_runs():
        if start < 0:
            parts.append(jnp.zeros(w_in_bf16.shape[:2] + (width,), BF16))
        else:
            piece = w_in_bf16[:, :, start:start + width]
            parts.append(piece if scale == 1.0 else piece * jnp.asarray(scale, BF16))
    return jnp.concatenate(parts, axis=2)


def _build_q_aug_row():
    row = np.zeros((1, W1_WIDTH), np.float32)
    for h in range(4):
        a = SEG["qa"][0] + h * 256 + QA_AUG
        row[0, a], row[0, a + 1] = SLOPES_A[h], SLOPES_A[h] * 256.0
        b = SEG["qb"][0] + h * 128 + QBC_AUG
        row[0, b], row[0, b + 1] = SLOPES_B[h], SLOPES_B[h] * 256.0
        c = SEG["qc"][0] + h * 128 + QBC_AUG
        row[0, c:c + 3] = 1.0
    return row


def _cparams(*sem):
    return pltpu.CompilerParams(dimension_semantics=sem, vmem_limit_bytes=VMEM_LIMIT_BYTES)


def _const_spec(shape):
    nd = len(shape)
    return pl.BlockSpec(shape, lambda *_: (0,) * nd, pipeline_mode=pl.Buffered(1))


def _layer_spec(shape, layer):
    nd = len(shape)
    return pl.BlockSpec((None,) + tuple(shape), lambda *_: (layer,) + (0,) * nd,
                        pipeline_mode=pl.Buffered(1))


def _dot(a, b):
    return jnp.dot(a, b, preferred_element_type=F32)


def _dot_nt(a, b):
    return lax.dot_general(a, b, (((1,), (1,)), ((), ())), preferred_element_type=F32)


def _row_tile(n, pref):
    t = min(n, pref)
    while n % t:
        t //= 2
    return t


def _split3(x):
    hi = x.astype(BF16)
    r = x - hi.astype(F32)
    mid = r.astype(BF16)
    lo = (r - mid.astype(F32)).astype(BF16)
    return hi, mid, lo


def _log_sigmoid(x):
    return jnp.minimum(x, 0.0) - jnp.log1p(jnp.exp(-jnp.abs(x)))


def _layernorm(h, g, b):
    mu = jnp.mean(h, axis=-1, keepdims=True)
    d = h - mu
    var = jnp.mean(d * d, axis=-1, keepdims=True)
    return d * lax.rsqrt(var + LN_EPS) * g + b


def _lambda(lam_ref, lam_init):
    lam_p = lam_ref[...]
    return (jnp.exp(jnp.sum(lam_p[0:1] * lam_p[1:2], axis=1, keepdims=True))
            - jnp.exp(jnp.sum(lam_p[2:3] * lam_p[3:4], axis=1, keepdims=True)) + lam_init)


def _for_causal_extent(qi, tq, seq, body, tiles_per_extent=1):
    n_tiles = seq // tq
    step = min(tiles_per_extent, n_tiles)
    for c in range(n_tiles // step):
        pl.when(qi // step == c)(functools.partial(body, (c + 1) * step * tq))


def _diag_mask(tq):
    return (lax.broadcasted_iota(jnp.int32, (tq, tq), 1) <= lax.broadcasted_iota(jnp.int32, (tq, tq), 0))


def _mask_diag(s, diag):
    tq = diag.shape[0]
    width = s.shape[1]
    tail = jnp.where(diag, s[:, width - tq:], NEG_INF)
    return tail if width == tq else jnp.concatenate([s[:, :width - tq], tail], axis=1)


def _proj_kernel(*refs, n_alias, leaves):
    x_ref, w_ref, aug_ref, bfg_ref = refs[:4]
    (qa_ref, na_ref, qb_ref, nb_ref, iq_ref, kr_ref, qc_ref, nc_ref, dc_ref, misc_ref, kidx_ref,
     logf_ref) = refs[4 + n_alias:16 + n_alias]
    leaf_refs = refs[16 + n_alias:]
    xb = x_ref[...].astype(BF16)

    def seg(name):
        lo, hi = SEG[name]
        return _dot(xb, w_ref[:, lo:hi])

    def qseg(name):
        lo, hi = SEG[name]
        return (seg(name) + aug_ref[:, lo:hi]).astype(BF16)

    news = {"na": seg("na"), "nb": seg("nb"), "nc": seg("nc")}
    qa_ref[...] = qseg("qa")
    na_ref[...] = news["na"]
    qb_ref[...] = qseg("qb")
    nb_ref[...] = news["nb"]
    iq_ref[...] = seg("iq").astype(BF16)
    kr_ref[...] = seg("krep").astype(BF16)
    qc_ref[...] = qseg("qc")
    nc_ref[...] = news["nc"]
    for name, ref in zip(leaves, leaf_refs):
        v = news[name]
        ref[...] = v.reshape(v.shape[0], N_HEADS, v.shape[1] // N_HEADS)
    dc_ref[...] = seg("dconv")
    kidx_ref[...] = seg("kidx")[:, :D_IDX]
    misc = seg("misc")
    misc_ref[...] = misc
    logf_ref[...] = _log_sigmoid(misc + bfg_ref[...])[:, MISC_CF:MISC_CF + H_C]


LEAF_NAMES = ("na", "nb", "nc")


def _proj(x2d, wts, layer, tm, leaf_bufs=None, depth=None):
    n = x2d.shape[0]
    widths = dict((k, v[1] - v[0]) for k, v in SEG.items())
    outs = [("qa", BF16, widths["qa"]), ("na", F32, widths["na"]), ("qb", BF16, widths["qb"]),
            ("nb", F32, widths["nb"]), ("iq", BF16, widths["iq"]), ("krep", BF16, widths["krep"]),
            ("qc", BF16, widths["qc"]), ("nc", F32, widths["nc"]), ("dconv", F32, widths["dconv"]),
            ("misc", F32, widths["misc"]), ("kidx", F32, D_IDX), ("logf", F32, H_C)]
    in_specs = [pl.BlockSpec((tm, D_MODEL), lambda i: (i, 0)),
                _layer_spec((D_MODEL, W1_WIDTH), layer),
                _const_spec((1, W1_WIDTH)),
                _layer_spec((1, LANES), layer)]
    args = [x2d, wts["w1"], wts["aug_row"], wts["bfg_row"]]
    out_specs = [pl.BlockSpec((tm, w), lambda i: (i, 0)) for _, _, w in outs]
    out_shape = [jax.ShapeDtypeStruct((n, w), dt) for _, dt, w in outs]
    leaves, aliases, n_alias = (), {}, 0
    if leaf_bufs is not None:
        leaves = LEAF_NAMES
        for k, name in enumerate(leaves):
            hw = widths[name] // N_HEADS
            out_specs.append(pl.BlockSpec((None, tm, N_HEADS, hw), lambda i: (layer, i, 0, 0)))
            out_shape.append(jax.ShapeDtypeStruct((depth, n, N_HEADS, hw), F32))
            if leaf_bufs != "new":
                in_specs.append(pl.BlockSpec(memory_space=pl.ANY))
                args.append(leaf_bufs[k])
                aliases[len(args) - 1] = len(outs) + k
        n_alias = len(aliases)
    res = pl.pallas_call(
        functools.partial(_proj_kernel, n_alias=n_alias, leaves=leaves),
        grid=(n // tm,),
        in_specs=in_specs, out_specs=out_specs, out_shape=out_shape,
        input_output_aliases=aliases,
        compiler_params=_cparams("parallel"),
        name="proj",
    )(*args)
    out = dict(zip([o[0] for o in outs], res[:len(outs)]))
    out["leaf_bufs"] = tuple(res[len(outs):])
    return out


def _kth_largest_i16(vals, need):
    rows, width = vals.shape

    def count_ge(t):
        m = jnp.where(vals >= t.astype(jnp.int16), jnp.int16(1), jnp.int16(0))
        part = m[:, 0:LANES]
        for j in range(1, width // LANES):
            part = part + m[:, j * LANES:(j + 1) * LANES]
        return jnp.sum(part.astype(F32), axis=1, keepdims=True)

    t0 = jnp.where(count_ge(jnp.zeros((rows, 1), jnp.int32)) >= need, jnp.int32(0), jnp.int32(-32768))

    def body(i, t):
        cand = t + jnp.left_shift(jnp.int32(1), jnp.int32(14) - i)
        return jnp.where(count_ge(cand) >= need, cand, t)

    return lax.fori_loop(0, 15, body, t0, unroll=3), count_ge


def _topk_select(score, k):
    rows, width = score.shape
    kf = jnp.float32(k)
    big = jnp.float32(3.0e38)

    def count(mask):
        return jnp.sum(jnp.where(mask, 1.0, 0.0), axis=1, keepdims=True)

    x = score * 0.5
    finite = x > NEG_INF
    short = count(finite) < kf
    lo0 = jnp.min(jnp.where(finite, x, big), axis=1, keepdims=True)
    hi0 = jnp.max(jnp.where(finite, x, -big), axis=1, keepdims=True)
    need = jnp.full((rows, 1), kf, F32)

    def active(lo, hi):
        return (hi > lo) & jnp.logical_not(short)

    def refine(state):
        lo, hi = state
        act = active(lo, hi)
        w = hi - lo
        up = jnp.where(w < 1e-30, jnp.float32(2.0 ** 64), jnp.float32(1.0))
        scale = 65533.0 / jnp.where(act, w * up, 65533.0)
        v = jnp.maximum(jnp.floor(((x - lo) * up) * scale), -1.0) - 32767.0
        v = jnp.where(x > hi, 32767.0, v)
        b, _ = _kth_largest_i16(v.astype(jnp.int32).astype(jnp.int16), need)
        inb = v == b.astype(F32)
        new_lo = jnp.min(jnp.where(inb, x, big), axis=1, keepdims=True)
        new_hi = jnp.max(jnp.where(inb, x, -big), axis=1, keepdims=True)
        return jnp.where(act, new_lo, lo), jnp.where(act, new_hi, hi)

    def unsettled(state):
        lo, hi = state
        return jnp.max(jnp.where(active(lo, hi), 1.0, 0.0)) > 0.0

    lo, _ = lax.while_loop(unsettled, refine, (lo0, hi0))
    t = jnp.where(short, NEG_INF, lo)
    gt = x > t
    eq = x == t
    n_gt = count(gt)
    need = kf - n_gt
    idx = lax.broadcasted_iota(jnp.int32, (rows, width), 1)
    nbits = max(1, int(math.ceil(math.log2(width))))

    surplus = jnp.where(t > NEG_INF, n_gt + count(eq) - kf, 0.0)
    tied = jnp.max(surplus) > 0.0

    def tie_cut():
        def ibody(i, j):
            cand = j + jnp.left_shift(jnp.int32(1), jnp.int32(nbits - 1) - i)
            return jnp.where(count(eq & (idx < cand)) < need, cand, j)

        return lax.fori_loop(0, nbits, ibody, jnp.zeros((rows, 1), jnp.int32))

    j = lax.cond(tied, tie_cut, lambda: jnp.full((rows, 1), width, jnp.int32))
    return gt | (eq & (idx <= j))


def _attn_a_kernel(qa_ref, na_ref, lam_ref, g_ref, o_ref, k1_s, k2_s, v_s, *, tq, seq, lam_init):
    qi = pl.program_id(1)

    @pl.when(qi == 0)
    def _():
        lane = lax.broadcasted_iota(jnp.int32, (seq, LANES), 1)
        kp = lax.broadcasted_iota(jnp.int32, (seq, LANES), 0)
        kaug = jnp.where(lane == 0, (kp & 255).astype(F32),
                         jnp.where(lane == 1, (kp >> 8).astype(F32), 0.0)).astype(BF16)
        for h in range(H_A):
            kk = na_ref[0, :, h * 256:h * 256 + 128]
            k1_s[h, :, 0:LANES] = jnp.where(lane < 64, kk, 0.0).astype(BF16)
            k1_s[h, :, LANES:2 * LANES] = kaug
            k2_s[h, :, 0:LANES] = jnp.where(lane >= 64, kk, 0.0).astype(BF16)
            k2_s[h, :, LANES:2 * LANES] = kaug
            v_s[h] = na_ref[0, :, h * 256 + 128:(h + 1) * 256].astype(BF16)

    lam = _lambda(lam_ref, lam_init)

    def body(width):
        diag = _diag_mask(tq)
        for h in range(H_A):
            qh = qa_ref[0, :, h * 256:(h + 1) * 256]
            s1 = _mask_diag(_dot_nt(qh, k1_s[h, 0:width, :]), diag)
            s2 = _mask_diag(_dot_nt(qh, k2_s[h, 0:width, :]), diag)
            e1 = jnp.exp(s1 - jnp.max(s1, axis=1, keepdims=True))
            e2 = jnp.exp(s2 - jnp.max(s2, axis=1, keepdims=True))
            r1 = 1.0 / jnp.sum(e1, axis=1, keepdims=True)
            r2 = lam / jnp.sum(e2, axis=1, keepdims=True)
            p = (e1 * r1 - e2 * r2).astype(BF16)
            o = _dot(p, v_s[h, 0:width, :])
            o = o * lax.rsqrt(jnp.mean(o * o, axis=1, keepdims=True) + NORM_EPS)
            o = o * g_ref[h:h + 1, :] * (1.0 - lam_init)
            o_ref[0, :, h * 128:(h + 1) * 128] = o.astype(BF16)

    _for_causal_extent(qi, tq, seq, body)


def _attn_a(qa, na, wts, layer, lam_init, tq):
    bsz, seq, _ = qa.shape
    return pl.pallas_call(
        functools.partial(_attn_a_kernel, tq=tq, seq=seq, lam_init=lam_init),
        grid=(bsz, seq // tq),
        in_specs=[pl.BlockSpec((1, tq, 1024), lambda b, i: (b, i, 0)),
                  pl.BlockSpec((1, seq, 1024), lambda b, i: (b, 0, 0)),
                  _layer_spec((4, DK_A), layer),
                  _layer_spec((H_A, 2 * DK_A), layer)],
        out_specs=pl.BlockSpec((1, tq, A_W), lambda b, i: (b, i, 0)),
        out_shape=jax.ShapeDtypeStruct((bsz, seq, A_W), BF16),
        scratch_shapes=[pltpu.VMEM((H_A, seq, 2 * LANES), BF16), pltpu.VMEM((H_A, seq, 2 * LANES), BF16),
                        pltpu.VMEM((H_A, seq, LANES), BF16)],
        compiler_params=_cparams("parallel", "arbitrary"),
        name="attn_a",
    )(qa, na, wts["lam_p"], wts["g"])


def _attn_c_kernel(qc_ref, nc_ref, misc_ref, bfg_ref, o_ref, k_s, v_s, *, tq, seq):
    qi = pl.program_id(1)

    @pl.when(qi == 0)
    def _():
        lane = lax.broadcasted_iota(jnp.int32, (seq, LANES), 1)
        ch = min(256, seq)
        tri = jnp.where(lax.broadcasted_iota(jnp.int32, (ch, ch), 0)
                        >= lax.broadcasted_iota(jnp.int32, (ch, ch), 1), 1.0, 0.0).astype(BF16)
        carry = jnp.zeros((1, LANES), F32)
        chunks = []
        for c in range(seq // ch):
            lf = _log_sigmoid(misc_ref[0, c * ch:(c + 1) * ch, :] + bfg_ref[...])
            hi, mid, lo = _split3(lf)
            fc = (_dot(tri, hi) + _dot(tri, mid)) + _dot(tri, lo) + carry
            carry = fc[ch - 1:ch, :]
            chunks.append(fc)
        fcum = jnp.concatenate(chunks, axis=0) if len(chunks) > 1 else chunks[0]
        for h in range(H_C):
            fh = jnp.broadcast_to(fcum[:, MISC_CF + h:MISC_CF + h + 1], (seq, LANES))
            hi, mid, lo = (v.astype(F32) for v in _split3(-fh))
            kv = nc_ref[0, :, h * 128:(h + 1) * 128]
            aug = jnp.where(lane == QBC_AUG, hi,
                            jnp.where(lane == QBC_AUG + 1, mid, jnp.where(lane == QBC_AUG + 2, lo, 0.0)))
            k_s[h] = jnp.where(lane < 64, kv, aug).astype(BF16)
            v_s[h] = kv.astype(BF16)

    def body(width):
        diag = _diag_mask(tq)
        for h in range(H_C):
            qh = qc_ref[0, :, h * 128:(h + 1) * 128]
            s = _mask_diag(_dot_nt(qh, k_s[h, 0:width, :]), diag)
            e = jnp.exp(s - jnp.max(s, axis=1, keepdims=True))
            r = 1.0 / jnp.sum(e, axis=1, keepdims=True)
            o = _dot(e.astype(BF16), v_s[h, 0:width, :]) * r
            o_ref[0, :, h * 64:(h + 1) * 64] = o[:, 64:128].astype(BF16)

    _for_causal_extent(qi, tq, seq, body)


def _attn_c(qc, nc, misc, wts, layer, tq):
    bsz, seq, _ = qc.shape
    return pl.pallas_call(
        functools.partial(_attn_c_kernel, tq=tq, seq=seq),
        grid=(bsz, seq // tq),
        in_specs=[pl.BlockSpec((1, tq, 512), lambda b, i: (b, i, 0)),
                  pl.BlockSpec((1, seq, 512), lambda b, i: (b, 0, 0)),
                  pl.BlockSpec((1, seq, LANES), lambda b, i: (b, 0, 0)),
                  _layer_spec((1, LANES), layer)],
        out_specs=pl.BlockSpec((1, tq, C_W), lambda b, i: (b, i, 0)),
        out_shape=jax.ShapeDtypeStruct((bsz, seq, C_W), BF16),
        scratch_shapes=[pltpu.VMEM((H_C, seq, LANES), BF16)] * 2,
        compiler_params=_cparams("parallel", "arbitrary"),
        name="attn_c",
    )(qc, nc, misc, wts["bfg_row"])


def _attn_b_kernel(*refs, tq, seq, topk, has_tail):
    qb_ref, nb_ref, iq_ref, kr_ref, misc_ref = refs[:5]
    tail_ref = refs[5] if has_tail else None
    o_ref, k_s, v_s, kr_s = refs[5 + int(has_tail):]
    qi = pl.program_id(1)

    @pl.when(qi == 0)
    def _():
        lane = lax.broadcasted_iota(jnp.int32, (seq, LANES), 1)
        kp = lax.broadcasted_iota(jnp.int32, (seq, LANES), 0)
        aug = jnp.where(lane == QBC_AUG, (kp & 255).astype(F32),
                        jnp.where(lane == QBC_AUG + 1, (kp >> 8).astype(F32), 0.0))
        for h in range(H_B):
            kv = nb_ref[0, :, h * 128:(h + 1) * 128]
            k_s[h] = jnp.where(lane < 64, kv, aug).astype(BF16)
            v_s[h] = kv.astype(BF16)
        kr = kr_ref[0]
        klane = lax.broadcasted_iota(jnp.int32, (seq, 256), 1)
        for h in range(H_IDX):
            kr_s[h] = kr * jnp.where((klane >> 5) == h, 1.0, 0.0).astype(BF16)

    def body(width):
        rows = qi * tq + lax.broadcasted_iota(jnp.int32, (tq, width), 0)
        cols = lax.broadcasted_iota(jnp.int32, (tq, width), 1)
        causal = cols <= rows
        iq = iq_ref[0]
        score = None
        for h in range(H_IDX):
            rel = jnp.maximum(_dot_nt(iq, kr_s[h, 0:width, :]), 0.0) * misc_ref[0, :, MISC_WI + h:MISC_WI + h + 1]
            score = rel if score is None else score + rel
        if has_tail and width == seq:
            last = jnp.where(qi == seq // tq - 1, tail_ref[0], score[tq - SUBLANES:, :])
            score = jnp.concatenate([score[:tq - SUBLANES, :], last], axis=0)
        sel = _topk_select(jnp.where(causal, score, NEG_INF), topk)
        bias = jnp.where(sel & causal, 0.0, NEG_INF)
        for h in range(H_B):
            qh = qb_ref[0, :, h * 128:(h + 1) * 128]
            s = _dot_nt(qh, k_s[h, 0:width, :]) + bias
            e = jnp.exp(s - jnp.max(s, axis=1, keepdims=True))
            r = 1.0 / jnp.sum(e, axis=1, keepdims=True)
            o = _dot(e.astype(BF16), v_s[h, 0:width, :]) * r
            o_ref[0, :, h * 64:(h + 1) * 64] = o[:, 64:128].astype(BF16)

    _for_causal_extent(qi, tq, seq, body, tiles_per_extent=2)


def _attn_b(qb, nb, iq, krep, misc, tail, tq):
    bsz, seq, _ = qb.shape
    topk = min(INDEX_TOPK_MAX, seq // 4)
    has_tail = tail is not None
    tail_specs = [pl.BlockSpec((1, SUBLANES, seq), lambda b, i: (b, 0, 0))] if has_tail else []
    return pl.pallas_call(
        functools.partial(_attn_b_kernel, tq=tq, seq=seq, topk=topk, has_tail=has_tail),
        grid=(bsz, seq // tq),
        in_specs=[pl.BlockSpec((1, tq, 512), lambda b, i: (b, i, 0)),
                  pl.BlockSpec((1, seq, 512), lambda b, i: (b, 0, 0)),
                  pl.BlockSpec((1, tq, 256), lambda b, i: (b, i, 0)),
                  pl.BlockSpec((1, seq, 256), lambda b, i: (b, 0, 0)),
                  pl.BlockSpec((1, tq, LANES), lambda b, i: (b, i, 0))] + tail_specs,
        out_specs=pl.BlockSpec((1, tq, B_W), lambda b, i: (b, i, 0)),
        out_shape=jax.ShapeDtypeStruct((bsz, seq, B_W), BF16),
        scratch_shapes=[pltpu.VMEM((H_B, seq, LANES), BF16), pltpu.VMEM((H_B, seq, LANES), BF16),
                        pltpu.VMEM((H_IDX, seq, H_IDX * D_IDX), BF16)],
        compiler_params=_cparams("parallel", "arbitrary"),
        name="attn_b",
    )(qb, nb, iq, krep, misc, *([tail] if has_tail else []))


def _merge_kernel(*refs, tm, sample, tiles_per_seq):
    if sample:
        (x_ref, oa_ref, ob_ref, oc_ref, dc_ref, st_ref, wg_ref, pa_ref, pb_ref, pc_ref, pd_ref,
         wo_ref, g_ref, b_ref, cw_ref, y_ref, conv_ref) = refs
    else:
        (x_ref, oa_ref, ob_ref, oc_ref, dc_ref, halo_ref, wg_ref, pa_ref, pb_ref, pc_ref, pd_ref,
         wo_ref, g_ref, b_ref, cw_ref, y_ref, conv_ref, uu_ref) = refs
    w0 = cw_ref[0:1, :]
    w1 = cw_ref[1:2, :]
    w2 = cw_ref[2:3, :]
    u = dc_ref[:, 256:512] * dc_ref[:, 512:768]
    if sample:
        s0 = st_ref[:, 0:256]
        s1 = st_ref[:, 256:512]
        y = s0 * w0 + s1 * w1 + u * w2
        conv_ref[:, 0:256] = s1
        conv_ref[:, 256:512] = u
    else:
        first = (pl.program_id(0) % tiles_per_seq) == 0
        uh = halo_ref[:, 256:512] * halo_ref[:, 512:768]
        uu_ref[0:SUBLANES, :] = jnp.where(first, 0.0, uh)
        uu_ref[SUBLANES:SUBLANES + tm, :] = u
        y = uu_ref[SUBLANES - 2:SUBLANES - 2 + tm, :] * w0 + uu_ref[SUBLANES - 1:SUBLANES - 1 + tm, :] * w1 + u * w2
        conv_ref[0] = uu_ref[SUBLANES + tm - 2:SUBLANES + tm, :]
    o_d = (dc_ref[:, 0:256] * y).astype(BF16)
    xv = x_ref[...]
    xb = xv.astype(BF16)
    mixed = None
    for i, (o, p_ref) in enumerate(((oa_ref[...], pa_ref), (ob_ref[...], pb_ref),
                                    (oc_ref[...], pc_ref), (o_d, pd_ref))):
        gate = jax.nn.sigmoid(_dot(xb, wg_ref[:, i * D_MODEL:(i + 1) * D_MODEL]))
        term = gate * _dot(o, p_ref[...])
        mixed = term if mixed is None else mixed + term
    h = ALPHA * xv + _dot(mixed.astype(BF16), wo_ref[...])
    y_ref[...] = _layernorm(h, g_ref[...], b_ref[...])


def _merge(x2d, oa, ob, oc, dconv, state2d, wts, layer, tm, seq):
    n = x2d.shape[0]
    sample = state2d is not None
    row = lambda w: pl.BlockSpec((tm, w), lambda i: (i, 0))
    in_specs = [row(D_MODEL), row(A_W), row(B_W), row(C_W), row(768)]
    args = [x2d, oa, ob, oc, dconv]
    if sample:
        in_specs.append(row(512))
        args.append(state2d)
        out_specs = [row(D_MODEL), row(512)]
        out_shape = [jax.ShapeDtypeStruct((n, D_MODEL), F32), jax.ShapeDtypeStruct((n, 512), F32)]
        scratch = []
        tiles_per_seq = 1
    else:
        tiles_per_seq = seq // tm
        r8 = tm // SUBLANES
        in_specs.append(pl.BlockSpec((SUBLANES, 768), lambda i: (jnp.maximum(i * r8 - 1, 0), 0)))
        args.append(dconv)
        out_specs = [row(D_MODEL), pl.BlockSpec((1, CONV_K - 1, CONV_W), lambda i: (i // tiles_per_seq, 0, 0))]
        out_shape = [jax.ShapeDtypeStruct((n, D_MODEL), F32),
                     jax.ShapeDtypeStruct((n // seq, CONV_K - 1, CONV_W), F32)]
        scratch = [pltpu.VMEM((tm + SUBLANES, CONV_W), F32)]
    for name in ("wg", "pa", "pb", "pc", "pd", "wo", "ln1_g", "ln1_b", "conv_w"):
        in_specs.append(_layer_spec(wts[name].shape[1:], layer))
        args.append(wts[name])
    return pl.pallas_call(
        functools.partial(_merge_kernel, tm=tm, sample=sample, tiles_per_seq=tiles_per_seq),
        grid=(n // tm,),
        in_specs=in_specs, out_specs=out_specs, out_shape=out_shape, scratch_shapes=scratch,
        compiler_params=_cparams("arbitrary"),
        name="merge_s" if sample else "merge",
    )(*args)


def _ffn_kernel(x_ref, wi_ref, wo_ref, g_ref, b_ref, y_ref):
    xv = x_ref[...]
    xb = xv.astype(BF16)
    hg = _dot(xb, wi_ref[:, 0:D_FF])
    hu = _dot(xb, wi_ref[:, D_FF:2 * D_FF])
    act = (hg * jax.nn.sigmoid(hg) * hu).astype(BF16)
    h = ALPHA * xv + _dot(act, wo_ref[...])
    y_ref[...] = _layernorm(h, g_ref[...], b_ref[...])


def _ffn(x2d, wts, layer, tm):
    n = x2d.shape[0]
    return pl.pallas_call(
        _ffn_kernel,
        grid=(n // tm,),
        in_specs=[pl.BlockSpec((tm, D_MODEL), lambda i: (i, 0)),
                  _layer_spec((D_MODEL, 2 * D_FF), layer), _layer_spec((D_FF, D_MODEL), layer),
                  _layer_spec((1, D_MODEL), layer), _layer_spec((1, D_MODEL), layer)],
        out_specs=pl.BlockSpec((tm, D_MODEL), lambda i: (i, 0)),
        out_shape=jax.ShapeDtypeStruct((n, D_MODEL), F32),
        compiler_params=_cparams("parallel"),
        name="ffn",
    )(x2d, wts["wf_in"], wts["wf_out"], wts["ln2_g"], wts["ln2_b"])


def _build_prefix_matrix():
    c = np.arange(PKEYS)
    same_head = (c[:, None] % N_HEADS) == (c[None, :] % N_HEADS)
    upto = (c[:, None] // N_HEADS) <= (c[None, :] // N_HEADS)
    return np.concatenate([same_head & upto, same_head], axis=1).astype(np.float32)


def _build_key_expansion():
    return (np.arange(PAGE_SIZE)[:, None] == (np.arange(PKEYS)[None, :] // N_HEADS)).astype(np.float32)


def _pfx_kernel(x_ref, u_ref, o_ref):
    hi, mid, lo = _split3(x_ref[...])
    u = u_ref[...]
    o_ref[...] = (_dot(hi, u) + _dot(mid, u)) + _dot(lo, u)


def _page_prefix(logf_pages, umat):
    n_pool = logf_pages.shape[0]
    tp = _row_tile(n_pool, 512) if n_pool % SUBLANES == 0 else n_pool
    return pl.pallas_call(
        _pfx_kernel,
        grid=(n_pool // tp,),
        in_specs=[pl.BlockSpec((tp, PKEYS), lambda i: (i, 0)),
                  _const_spec(umat.shape)],
        out_specs=pl.BlockSpec((tp, 2 * PKEYS), lambda i: (i, 0)),
        out_shape=jax.ShapeDtypeStruct((n_pool, 2 * PKEYS), F32),
        compiler_params=_cparams("parallel"),
        name="page_prefix",
    )(logf_pages, umat)


def _dot_split(a, b, dot):
    a0, a1, a2 = _split3(a)
    b0, b1, b2 = _split3(b)
    small = (dot(a0, b2) + dot(a2, b0)) + dot(a1, b1)
    return (small + (dot(a0, b1) + dot(a1, b0))) + dot(a0, b0)


def _idx_proj_kernel(x_ref, w_ref, o_ref):
    o_ref[...] = _dot_split(x_ref[...], w_ref[...], _dot)


def _idx_proj(x2d, w_idx, layer):
    return pl.pallas_call(
        _idx_proj_kernel,
        out_shape=jax.ShapeDtypeStruct((x2d.shape[0], w_idx.shape[2]), F32),
        compiler_params=pltpu.CompilerParams(vmem_limit_bytes=VMEM_LIMIT_BYTES),
        name="idx_proj",
    )(x2d, w_idx[layer])


def _idx_keys(x2d, w_key, tm):
    n = x2d.shape[0]
    return pl.pallas_call(
        _idx_proj_kernel,
        grid=(n // tm,),
        in_specs=[pl.BlockSpec((tm, D_MODEL), lambda i: (i, 0)), _const_spec(w_key.shape)],
        out_specs=pl.BlockSpec((tm, LANES), lambda i: (i, 0)),
        out_shape=jax.ShapeDtypeStruct((n, LANES), F32),
        compiler_params=_cparams("parallel"),
        name="idx_keys",
    )(x2d, w_key)


def _tail_score_kernel(z_ref, k_ref, o_ref):
    k0, k1, k2 = _split3(k_ref[0][:, 0:D_IDX])
    score = None
    for h in range(H_IDX):
        q0, q1, q2 = _split3(z_ref[0][:, h * D_IDX:(h + 1) * D_IDX])
        small = (_dot_nt(q0, k2) + _dot_nt(q2, k0)) + _dot_nt(q1, k1)
        rel = jnp.maximum((small + (_dot_nt(q0, k1) + _dot_nt(q1, k0))) + _dot_nt(q0, k0), 0.0)
        w = z_ref[0][:, H_IDX * D_IDX + D_IDX + h:H_IDX * D_IDX + D_IDX + h + 1] * (H_IDX ** -0.5 * D_IDX ** -0.5)
        score = rel * w if score is None else score + rel * w
    o_ref[0] = score


def _tail_scores(zt, keys):
    bsz, seq, _ = keys.shape
    return pl.pallas_call(
        _tail_score_kernel,
        grid=(bsz,),
        in_specs=[pl.BlockSpec((1, SUBLANES, IDX_COLS), lambda b: (b, 0, 0)),
                  pl.BlockSpec((1, seq, LANES), lambda b: (b, 0, 0))],
        out_specs=pl.BlockSpec((1, SUBLANES, seq), lambda b: (b, 0, 0)),
        out_shape=jax.ShapeDtypeStruct((bsz, SUBLANES, seq), F32),
        compiler_params=_cparams("parallel"),
        name="tail_scores",
    )(zt, keys)


def _idx_kernel(pt_ref, *refs, n_pages):
    pages = refs[:n_pages]
    iq_ref, wi_ref, knew_ref, o_ref = refs[n_pages:]
    iq = iq_ref[0]
    wi = wi_ref[0] * (H_IDX ** -0.5 * D_IDX ** -0.5)

    def weighted(rel):
        return jnp.sum(jnp.maximum(rel, 0.0) * wi, axis=0, keepdims=True)

    kt = jnp.concatenate([pages[j][0] for j in range(n_pages)], axis=1)
    o_ref[0, :, 0:n_pages * PAGE_SIZE] = weighted(_dot_split(iq, kt, _dot))
    knew = jnp.broadcast_to(knew_ref[0], (PAGE_SIZE, D_IDX))
    lane = lax.broadcasted_iota(jnp.int32, (1, PAGE_SIZE), 1)
    o_ref[0, :, n_pages * PAGE_SIZE:(n_pages + 1) * PAGE_SIZE] = jnp.where(
        lane == 0, weighted(_dot_split(iq, knew, _dot_nt)), NEG_INF)


def _idx_scores(page_table, kidx_t, page_base, iq8, wi8, knew):
    nsamp, n_pages = page_table.shape
    width = (n_pages + 1) * PAGE_SIZE

    def page_spec(j):
        return pl.BlockSpec((1, D_IDX, PAGE_SIZE), lambda s, pt: (page_base + pt[s, j], 0, 0))

    grid_spec = pltpu.PrefetchScalarGridSpec(
        num_scalar_prefetch=1,
        grid=(nsamp,),
        in_specs=[page_spec(j) for j in range(n_pages)] + [
            pl.BlockSpec((1, H_IDX, D_IDX), lambda s, pt: (s, 0, 0)),
            pl.BlockSpec((1, H_IDX, 1), lambda s, pt: (s, 0, 0)),
            pl.BlockSpec((1, 1, D_IDX), lambda s, pt: (s, 0, 0))],
        out_specs=pl.BlockSpec((1, 1, width), lambda s, pt: (s, 0, 0)),
    )
    out = pl.pallas_call(
        functools.partial(_idx_kernel, n_pages=n_pages),
        grid_spec=grid_spec,
        out_shape=jax.ShapeDtypeStruct((nsamp, 1, width), F32),
        compiler_params=_cparams("arbitrary"),
        name="idx_scores",
    )(page_table, *([kidx_t] * n_pages), iq8, wi8, knew)
    return out.reshape(nsamp, width)


def _sel_kernel(s_ref, e_ref, o_ref, *, topk, n_blocks):
    sel = jnp.where(_topk_select(s_ref[...], topk), 1.0, 0.0).astype(BF16)
    for j in range(n_blocks):
        rep = _dot(sel[:, j * PAGE_SIZE:(j + 1) * PAGE_SIZE], e_ref[...])
        o_ref[:, j * PKEYS:(j + 1) * PKEYS] = jnp.where(rep > 0.5, 0.0, NEG_INF)


def _select_bias(scores, expand, topk):
    nsamp, width = scores.shape
    n_blocks = width // PAGE_SIZE
    return pl.pallas_call(
        functools.partial(_sel_kernel, topk=topk, n_blocks=n_blocks),
        out_shape=jax.ShapeDtypeStruct((nsamp, n_blocks * PKEYS), F32),
        compiler_params=pltpu.CompilerParams(vmem_limit_bytes=VMEM_LIMIT_BYTES),
        name="select_bias",
    )(scores, expand)


def _attn_s_kernel(pt_ref, *refs, n_pages, lam_init):
    g = n_pages
    ca = refs[0:g]
    cb = refs[g:2 * g]
    cc = refs[2 * g:3 * g]
    pf = refs[3 * g:4 * g]
    (selp_ref, seln_ref, qa_ref, qb_ref, qc_ref, na_ref, nb_ref, nc_ref, lfn_ref, lam_ref, g_ref,
     oa_ref, ob_ref, oc_ref, xa_s, xb_s, xc_s) = refs[4 * g:]

    sub_a = lax.broadcasted_iota(jnp.int32, (SUBLANES, 256), 0)
    lane_a = lax.broadcasted_iota(jnp.int32, (SUBLANES, 256), 1)
    qa_row = qa_ref[0].astype(F32)
    qa_blk = jnp.zeros((SUBLANES, 256), F32)
    for h in range(N_HEADS):
        qa_blk = jnp.where((sub_a >> 1) == h, jnp.broadcast_to(qa_row[:, h * 256:(h + 1) * 256], (SUBLANES, 256)),
                           qa_blk)
    qa_blk = jnp.where((lane_a < 128) & ((lane_a >> 6) == (sub_a & 1)), qa_blk, 0.0).astype(BF16)
    sub_b = lax.broadcasted_iota(jnp.int32, (SUBLANES, LANES), 0)
    lane_b = lax.broadcasted_iota(jnp.int32, (SUBLANES, LANES), 1)

    def head_block(row):
        blk = jnp.zeros((SUBLANES, LANES), F32)
        for h in range(N_HEADS):
            blk = jnp.where((sub_b & 3) == h, jnp.broadcast_to(row[:, h * 128:(h + 1) * 128], (SUBLANES, LANES)), blk)
        return jnp.where(lane_b < 64, blk, 0.0).astype(BF16)

    qb_blk = head_block(qb_ref[0].astype(F32))
    qc_blk = head_block(qc_ref[0].astype(F32))

    sub1 = lax.broadcasted_iota(jnp.int32, (SUBLANES, 1), 0)
    slope_a = jnp.zeros((SUBLANES, 1), F32)
    slope_b = jnp.zeros((SUBLANES, 1), F32)
    for h in range(N_HEADS):
        slope_a = jnp.where((sub1 >> 1) == h, SLOPES_A[h], slope_a)
        slope_b = jnp.where((sub1 & 3) == h, SLOPES_B[h], slope_b)

    def head_masks(width):
        sub = lax.broadcasted_iota(jnp.int32, (SUBLANES, width), 0)
        lane = lax.broadcasted_iota(jnp.int32, (SUBLANES, width), 1)
        own_a = jnp.where((lane & 3) == (sub >> 1), 0.0, NEG_INF)
        own_b = jnp.where((lane & 3) == (sub & 3), 0.0, NEG_INF)
        return own_a, own_b

    def new_rows(x_s, ref):
        x_s[...] = jnp.zeros(x_s.shape, F32)
        x_s[0:N_HEADS, :] = ref[0]
        return x_s[...].astype(BF16)

    def attend(qblk, blocks, bias):
        s = jnp.concatenate([_dot_nt(qblk, blk) for blk in blocks], axis=1) + bias
        e = jnp.exp(s - jnp.max(s, axis=1, keepdims=True))
        r = 1.0 / jnp.sum(e, axis=1, keepdims=True)
        eb = e.astype(BF16)
        pv, off = None, 0
        for blk in blocks:
            t = _dot(eb[:, off:off + blk.shape[0]], blk)
            pv = t if pv is None else pv + t
            off += blk.shape[0]
        return pv * r

    new_pos = float(n_pages * PAGE_SIZE)
    own_a, own_b = head_masks(g * PKEYS)
    own_a1, own_b1 = head_masks(PAGE_SIZE)
    lane1 = lax.broadcasted_iota(jnp.int32, (1, PAGE_SIZE), 1)
    valid = jnp.where(lane1 < N_HEADS, 0.0, NEG_INF)
    lane_g = lax.broadcasted_iota(jnp.int32, (1, g * PKEYS), 1)
    kpos = (lane_g >> 2).astype(F32)

    blocks_a = [r[0].reshape(PKEYS, 256).astype(BF16) for r in ca] + [new_rows(xa_s, na_ref)]
    bias_a = jnp.concatenate([slope_a * kpos + own_a, slope_a * new_pos + own_a1 + valid], axis=1)
    acc_a = attend(qa_blk, blocks_a, bias_a)

    selb = jnp.concatenate([selp_ref[0, j] for j in range(g)], axis=1)
    blocks_b = [r[0].reshape(PKEYS, 128).astype(BF16) for r in cb] + [new_rows(xb_s, nb_ref)]
    bias_b = jnp.concatenate([slope_b * kpos + selb + own_b,
                              slope_b * new_pos + seln_ref[0, 0][:, 0:PAGE_SIZE] + own_b1 + valid], axis=1)
    acc_b = attend(qb_blk, blocks_b, bias_b)

    fparts = []
    carry = jnp.zeros((1, PKEYS), F32)
    for j in range(g):
        fparts.append(pf[j][0, :, 0:PKEYS] + carry)
        carry = carry + pf[j][0, :, PKEYS:2 * PKEYS]
    f_new = carry[:, 0:PAGE_SIZE] + lfn_ref[0]
    blocks_c = [r[0].reshape(PKEYS, 128).astype(BF16) for r in cc] + [new_rows(xc_s, nc_ref)]
    bias_c = jnp.concatenate([own_b - jnp.concatenate(fparts, axis=1), own_b1 + valid - f_new], axis=1)
    acc_c = attend(qc_blk, blocks_c, bias_c)

    lam = _lambda(lam_ref, lam_init)
    for h in range(N_HEADS):
        o = acc_a[2 * h:2 * h + 1, 128:256] - lam * acc_a[2 * h + 1:2 * h + 2, 128:256]
        o = o * lax.rsqrt(jnp.mean(o * o, axis=1, keepdims=True) + NORM_EPS)
        o = o * g_ref[h:h + 1, :] * (1.0 - lam_init)
        oa_ref[0, :, h * 128:(h + 1) * 128] = o.astype(BF16)
        ob_ref[0, :, h * 64:(h + 1) * 64] = acc_b[h:h + 1, 64:128].astype(BF16)
        oc_ref[0, :, h * 64:(h + 1) * 64] = acc_c[h:h + 1, 64:128].astype(BF16)


def _attn_sample(page_table, page_base, ca, cb, cc, pfx, selb4, qa, qb, qc, na, nb, nc, lfn, wts, layer, lam_init):
    nsamp, n_pages = page_table.shape

    def page_spec(width, j):
        return pl.BlockSpec((1, PAGE_SIZE, N_HEADS, width), lambda s, pt: (page_base + pt[s, j], 0, 0, 0))

    def pf_spec(j):
        return pl.BlockSpec((1, 1, 2 * PKEYS), lambda s, pt: (page_base + pt[s, j], 0, 0))

    def samp(shape):
        return pl.BlockSpec((1,) + shape, lambda s, pt: (s,) + (0,) * len(shape))

    pages = range(n_pages)
    in_specs = ([page_spec(256, j) for j in pages] + [page_spec(128, j) for j in pages]
                + [page_spec(128, j) for j in pages] + [pf_spec(j) for j in pages]
                + [pl.BlockSpec((1, n_pages, 1, PKEYS), lambda s, pt: (s, 0, 0, 0)),
                   pl.BlockSpec((1, 1, 1, PKEYS), lambda s, pt: (s, n_pages, 0, 0)),
                   samp((1, 1024)), samp((1, 512)), samp((1, 512)),
                   samp((N_HEADS, 256)), samp((N_HEADS, 128)), samp((N_HEADS, 128)), samp((1, PAGE_SIZE)),
                   pl.BlockSpec((None, 4, DK_A), lambda s, pt: (layer, 0, 0)),
                   pl.BlockSpec((None, H_A, 2 * DK_A), lambda s, pt: (layer, 0, 0))])
    out_specs = [samp((1, A_W)), samp((1, B_W)), samp((1, C_W))]
    grid_spec = pltpu.PrefetchScalarGridSpec(
        num_scalar_prefetch=1, grid=(nsamp,), in_specs=in_specs, out_specs=out_specs,
        scratch_shapes=[pltpu.VMEM((PAGE_SIZE, 256), F32), pltpu.VMEM((PAGE_SIZE, 128), F32),
                        pltpu.VMEM((PAGE_SIZE, 128), F32)])
    return pl.pallas_call(
        functools.partial(_attn_s_kernel, n_pages=n_pages, lam_init=lam_init),
        grid_spec=grid_spec,
        out_shape=[jax.ShapeDtypeStruct((nsamp, 1, A_W), BF16),
                   jax.ShapeDtypeStruct((nsamp, 1, B_W), BF16),
                   jax.ShapeDtypeStruct((nsamp, 1, C_W), BF16)],
        compiler_params=_cparams("arbitrary"),
        name="attn_sample",
    )(page_table, *([ca] * n_pages), *([cb] * n_pages), *([cc] * n_pages), *([pfx] * n_pages),
      selb4, selb4, qa, qb, qc, na, nb, nc, lfn, wts["lam_p"], wts["g"])


def _prompt_layer(x2d, wts, layer, depth, bsz, seq, lam_init, leaf_bufs):
    n = bsz * seq
    pr = _proj(x2d, wts, layer, _row_tile(n, 512), leaf_bufs=leaf_bufs, depth=depth)
    r3 = lambda a: a.reshape(bsz, seq, a.shape[-1])
    tq = _row_tile(seq, 256)
    oa = _attn_a(r3(pr["qa"]), r3(pr["na"]), wts, layer, lam_init, tq)
    tail = None
    if layer < depth - 1:
        keys = _idx_keys(x2d, wts["w_key"][layer], _row_tile(n, 512)).reshape(bsz, seq, LANES)
        xt = x2d.reshape(bsz, seq, D_MODEL)[:, seq - SUBLANES:, :].reshape(bsz * SUBLANES, D_MODEL)
        zt = _idx_proj(xt, wts["w_idx"], layer).reshape(bsz, SUBLANES, IDX_COLS)
        tail = _tail_scores(zt, keys)
    ob = _attn_b(r3(pr["qb"]), r3(pr["nb"]), r3(pr["iq"]), r3(pr["krep"]), r3(pr["misc"]), tail, tq)
    oc = _attn_c(r3(pr["qc"]), r3(pr["nc"]), r3(pr["misc"]), wts, layer, tq)
    x1, new_conv = _merge(x2d, oa.reshape(n, A_W), ob.reshape(n, B_W), oc.reshape(n, C_W), pr["dconv"],
                          None, wts, layer, _row_tile(seq, 256), seq)
    x2 = _ffn(x1, wts, layer, _row_tile(n, 256))
    news = (pr["kidx"].reshape(bsz, seq, D_IDX), pr["logf"].reshape(bsz, seq, H_C), new_conv)
    return x2, news, pr["leaf_bufs"]


def _sample_layer(x2d, wts, layer, caches, page_base, state, page_table, lam_init):
    nsamp = x2d.shape[0]
    n_pages = page_table.shape[1]
    ca, cb, ckt, cc, pfx = caches
    tm = _row_tile(nsamp, 128)
    pr = _proj(x2d, wts, layer, tm)
    zi = _idx_proj(x2d, wts["w_idx"], layer)
    iq8 = zi[:, 0:H_IDX * D_IDX].reshape(nsamp, H_IDX, D_IDX)
    knew = zi[:, H_IDX * D_IDX:H_IDX * D_IDX + D_IDX].reshape(nsamp, 1, D_IDX)
    wi8 = zi[:, H_IDX * D_IDX + D_IDX:IDX_COLS].reshape(nsamp, H_IDX, 1)
    scores = _idx_scores(page_table, ckt, page_base, iq8, wi8, knew)
    topk = min(INDEX_TOPK_MAX, (n_pages * PAGE_SIZE + 1) // 4)
    selb4 = _select_bias(scores, wts["expand"], topk).reshape(nsamp, n_pages + 1, 1, PKEYS)
    lfn = jnp.pad(pr["logf"], ((0, 0), (0, PAGE_SIZE - H_C))).reshape(nsamp, 1, PAGE_SIZE)
    r3 = lambda a: a.reshape(nsamp, 1, a.shape[-1])
    r4 = lambda a: a.reshape(nsamp, N_HEADS, a.shape[-1] // N_HEADS)
    oa, ob, oc = _attn_sample(page_table, page_base, ca, cb, cc, pfx, selb4,
                              r3(pr["qa"]), r3(pr["qb"]), r3(pr["qc"]), r4(pr["na"]), r4(pr["nb"]), r4(pr["nc"]),
                              lfn, wts, layer, lam_init)
    x1, conv2 = _merge(x2d, oa.reshape(nsamp, A_W), ob.reshape(nsamp, B_W), oc.reshape(nsamp, C_W),
                       pr["dconv"], state.reshape(nsamp, (CONV_K - 1) * CONV_W), wts, layer, tm, 1)
    x2 = _ffn(x1, wts, layer, tm)
    news = (pr["na"].reshape(nsamp, 1, H_A, 4 * DK_A), pr["nb"].reshape(nsamp, 1, H_B, 2 * DH_B),
            pr["kidx"].reshape(nsamp, 1, D_IDX), pr["nc"].reshape(nsamp, 1, H_C, 2 * DH_C),
            pr["logf"].reshape(nsamp, 1, H_C), conv2.reshape(nsamp, CONV_K - 1, CONV_W))
    return x2, news


def kernel(x_prompt, x_sample, cache_a_kv, cache_b_kv, cache_b_kidx, cache_c_kv, cache_c_logf, state_conv, page_table, w_in, b_fgate, lam_q1, lam_k1, lam_q2, lam_k2, g_diffnorm, conv_w, w_br_a, w_br_b, w_br_c, w_br_d, w_o, ln1_g, ln1_b, w_ffn_in, w_ffn_out, ln2_g, ln2_b):
    bsz, seq, _ = x_prompt.shape
    nsamp = x_sample.shape[0]
    depth = w_in.shape[0]
    bf = lambda a: a.astype(BF16)
    w_in_b = bf(w_in)
    wts = dict(
        w1=_rearranged_weights(w_in_b), wg=w_in_b[:, :, O_G:],
        aug_row=jnp.asarray(_build_q_aug_row()),
        bfg_row=jnp.zeros((depth, 1, LANES), F32).at[:, 0, MISC_CF:MISC_CF + H_C].set(b_fgate),
        lam_p=jnp.stack([lam_q1, lam_k1, lam_q2, lam_k2], axis=1),
        g=g_diffnorm, conv_w=conv_w, pa=bf(w_br_a), pb=bf(w_br_b), pc=bf(w_br_c), pd=bf(w_br_d), wo=bf(w_o),
        ln1_g=ln1_g[:, None, :], ln1_b=ln1_b[:, None, :], wf_in=bf(w_ffn_in), wf_out=bf(w_ffn_out),
        ln2_g=ln2_g[:, None, :], ln2_b=ln2_b[:, None, :],
        expand=jnp.asarray(_build_key_expansion(), dtype=BF16),
        w_idx=w_in[:, :, O_BIQ:O_BIQ + IDX_COLS],
        w_key=jnp.pad(w_in[:, :, O_BIK:O_BIK + D_IDX], ((0, 0), (0, 0), (0, LANES - D_IDX))))
    n_pool = cache_a_kv.shape[1]
    all_pages = depth * n_pool
    pfx = _page_prefix(cache_c_logf.reshape(all_pages, PKEYS), jnp.asarray(_build_prefix_matrix(), dtype=BF16))
    caches = (cache_a_kv.reshape(all_pages, PAGE_SIZE, N_HEADS, 4 * DK_A),
              cache_b_kv.reshape(all_pages, PAGE_SIZE, N_HEADS, 2 * DH_B),
              jnp.swapaxes(cache_b_kidx, 2, 3).reshape(all_pages, D_IDX, PAGE_SIZE),
              cache_c_kv.reshape(all_pages, PAGE_SIZE, N_HEADS, 2 * DH_C),
              pfx.reshape(all_pages, 1, 2 * PKEYS))

    yp = x_prompt.reshape(bsz * seq, D_MODEL)
    ys = x_sample.reshape(nsamp, D_MODEL)
    news_p, news_s = [], []
    leaf_bufs = "new"
    for l in range(depth):
        lam_init = 0.8 - 0.6 * math.exp(-0.3 * l)
        yp, new_p, leaf_bufs = _prompt_layer(yp, wts, l, depth, bsz, seq, lam_init, leaf_bufs)
        ys, new_s = _sample_layer(ys, wts, l, caches, l * n_pool, state_conv[l], page_table, lam_init)
        news_p.append(new_p)
        news_s.append(new_s)

    def stack(lst, i):
        return jnp.stack([e[i] for e in lst])

    kv_p = [b.reshape(depth, bsz, seq, N_HEADS, b.shape[-1]) for b in leaf_bufs]
    return (yp.reshape(bsz, seq, D_MODEL), ys.reshape(nsamp, 1, D_MODEL),
            kv_p[0], stack(news_s, 0), kv_p[1], stack(news_s, 1),
            stack(news_p, 0), stack(news_s, 2), kv_p[2], stack(news_s, 3),
            stack(news_p, 1), stack(news_s, 4), stack(news_p, 2), stack(news_s, 5))
```

```python
import functools
import math

import numpy as np
import jax
import jax.numpy as jnp
from jax import lax
from jax.experimental import pallas as pl
from jax.experimental.pallas import tpu as pltpu

F32 = jnp.float32
BF16 = jnp.bfloat16
NEG_INF = float("-inf")

D_MODEL = 1024
DEPTH = 2
PAGE_SIZE = 128
H_A, DK_A = 4, 64
A_W = H_A * 2 * DK_A
H_B, DH_B = 4, 64
B_W = H_B * DH_B
H_IDX, D_IDX = 8, 32
INDEX_TOPK_MAX = 256
H_C, DH_C = 4, 64
C_W = H_C * DH_C
CONV_W, CONV_K = 256, 3
N_BRANCH = 4
D_FF = -(-8 * D_MODEL // (3 * 256)) * 256
ALPHA = (2 * DEPTH) ** 0.25
LN_EPS = 1e-5
NORM_EPS = 1e-6
N_HEADS = 4

_n = H_A + H_B
_S_ALL = 2.0 ** (-8.0 * (np.arange(_n) + 1) / _n)
SLOPES_A = [float(v) for v in _S_ALL[0::2]]
SLOPES_B = [float(v) for v in _S_ALL[1::2]]

LANES = 128
SUBLANES = 8
VMEM_LIMIT_BYTES = 56 * 1024 * 1024
PKEYS = PAGE_SIZE * N_HEADS

_WIDTHS = (A_W, A_W, A_W, B_W, B_W, B_W, H_IDX * D_IDX, D_IDX, H_IDX,
           C_W, C_W, C_W, H_C, CONV_W, CONV_W, CONV_W, N_BRANCH * D_MODEL)
_OFF = np.concatenate([[0], np.cumsum(_WIDTHS)]).astype(np.int64)
(O_AQ, O_AK, O_AV, O_BQ, O_BK, O_BV, O_BIQ, O_BIK, O_BIW, O_CQ, O_CK, O_CV, O_CF,
 O_DB, O_DC, O_DH, O_G, IN_WIDTH) = [int(v) for v in _OFF]

SEG = {}
_pos = 0
for _name, _w in (("qa", 1024), ("na", 1024), ("qb", 512), ("nb", 512), ("iq", 256), ("krep", 256),
                  ("qc", 512), ("nc", 512), ("dconv", 768), ("kidx", 128), ("misc", 128)):
    SEG[_name] = (_pos, _pos + _w)
    _pos += _w
W1_WIDTH = _pos
IDX_COLS = H_IDX * D_IDX + D_IDX + H_IDX
MISC_CF = 0
MISC_WI = 8
QA_AUG = 128
QBC_AUG = 64


def _build_w1_columns():
    src = -np.ones((W1_WIDTH,), np.int64)
    scale = np.ones((W1_WIDTH,), np.float32)
    s = SEG["qa"][0]
    for h in range(H_A):
        src[s + h * 256: s + h * 256 + 128] = O_AQ + h * 128 + np.arange(128)
    scale[SEG["qa"][0]:SEG["qa"][1]] = DK_A ** -0.5
    s = SEG["na"][0]
    for h in range(H_A):
        src[s + h * 256: s + h * 256 + 128] = O_AK + h * 128 + np.arange(128)
        src[s + h * 256 + 128: s + (h + 1) * 256] = O_AV + h * 128 + np.arange(128)
    for nm, oq, ok, ov in (("b", O_BQ, O_BK, O_BV), ("c", O_CQ, O_CK, O_CV)):
        s = SEG["q" + nm][0]
        for h in range(4):
            src[s + h * 128: s + h * 128 + 64] = oq + h * 64 + np.arange(64)
        scale[SEG["q" + nm][0]:SEG["q" + nm][1]] = 64 ** -0.5
        s = SEG["n" + nm][0]
        for h in range(4):
            src[s + h * 128: s + h * 128 + 64] = ok + h * 64 + np.arange(64)
            src[s + h * 128 + 64: s + (h + 1) * 128] = ov + h * 64 + np.arange(64)
    s = SEG["iq"][0]
    src[s:s + 256] = O_BIQ + np.arange(256)
    s = SEG["krep"][0]
    for h in range(H_IDX):
        src[s + h * 32: s + (h + 1) * 32] = O_BIK + np.arange(32)
    s = SEG["dconv"][0]
    src[s:s + 768] = O_DB + np.arange(768)
    s = SEG["kidx"][0]
    src[s:s + 32] = O_BIK + np.arange(32)
    s = SEG["misc"][0]
    src[s + MISC_CF: s + MISC_CF + H_C] = O_CF + np.arange(H_C)
    src[s + MISC_WI: s + MISC_WI + H_IDX] = O_BIW + np.arange(H_IDX)
    scale[s + MISC_WI: s + MISC_WI + H_IDX] = H_IDX ** -0.5 * D_IDX ** -0.5
    return src, scale


_W1_SRC, _W1_SCALE = _build_w1_columns()


def _w1_runs():
    runs = []
    i = 0
    while i < W1_WIDTH:
        j = i + 1
        while (j < W1_WIDTH and _W1_SCALE[j] == _W1_SCALE[i]
               and ((_W1_SRC[i] < 0 and _W1_SRC[j] < 0)
                    or (_W1_SRC[i] >= 0 and _W1_SRC[j] == _W1_SRC[i] + (j - i)))):
            j += 1
        runs.append((int(_W1_SRC[i]), j - i, float(_W1_SCALE[i])))
        i = j
    return runs


def _rearranged_weights(w_in_bf16):
    parts = []
    for start, width, scale in _w1_runs():
        if start < 0:
            parts.append(jnp.zeros(w_in_bf16.shape[:2] + (width,), BF16))
        else:
            piece = w_in_bf16[:, :, start:start + width]
            parts.append(piece if scale == 1.0 else piece * jnp.asarray(scale, BF16))
    return jnp.concatenate(parts, axis=2)


def _build_q_aug_row():
    row = np.zeros((1, W1_WIDTH), np.float32)
    for h in range(4):
        a = SEG["qa"][0] + h * 256 + QA_AUG
        row[0, a], row[0, a + 1] = SLOPES_A[h], SLOPES_A[h] * 256.0
        b = SEG["qb"][0] + h * 128 + QBC_AUG
        row[0, b], row[0, b + 1] = SLOPES_B[h], SLOPES_B[h] * 256.0
        c = SEG["qc"][0] + h * 128 + QBC_AUG
        row[0, c:c + 3] = 1.0
    return row


def _cparams(*sem):
    return pltpu.CompilerParams(dimension_semantics=sem, vmem_limit_bytes=VMEM_LIMIT_BYTES)


def _const_spec(shape):
    nd = len(shape)
    return pl.BlockSpec(shape, lambda *_: (0,) * nd, pipeline_mode=pl.Buffered(1))


def _layer_spec(shape, layer):
    nd = len(shape)
    return pl.BlockSpec((None,) + tuple(shape), lambda *_: (layer,) + (0,) * nd,
                        pipeline_mode=pl.Buffered(1))


def _dot(a, b):
    return jnp.dot(a, b, preferred_element_type=F32)


def _dot_nt(a, b):
    return lax.dot_general(a, b, (((1,), (1,)), ((), ())), preferred_element_type=F32)


def _row_tile(n, pref):
    t = min(n, pref)
    while n % t:
        t //= 2
    return t


def _split3(x):
    hi = x.astype(BF16)
    r = x - hi.astype(F32)
    mid = r.astype(BF16)
    lo = (r - mid.astype(F32)).astype(BF16)
    return hi, mid, lo


def _log_sigmoid(x):
    return jnp.minimum(x, 0.0) - jnp.log1p(jnp.exp(-jnp.abs(x)))


def _layernorm(h, g, b):
    mu = jnp.mean(h, axis=-1, keepdims=True)
    d = h - mu
    var = jnp.mean(d * d, axis=-1, keepdims=True)
    return d * lax.rsqrt(var + LN_EPS) * g + b


def _lambda(lam_ref, lam_init):
    lam_p = lam_ref[...]
    return (jnp.exp(jnp.sum(lam_p[0:1] * lam_p[1:2], axis=1, keepdims=True))
            - jnp.exp(jnp.sum(lam_p[2:3] * lam_p[3:4], axis=1, keepdims=True)) + lam_init)


def _for_causal_extent(qi, tq, seq, body, tiles_per_extent=1):
    n_tiles = seq // tq
    step = min(tiles_per_extent, n_tiles)
    for c in range(n_tiles // step):
        pl.when(qi // step == c)(functools.partial(body, (c + 1) * step * tq))


def _diag_mask(tq):
    return (lax.broadcasted_iota(jnp.int32, (tq, tq), 1) <= lax.broadcasted_iota(jnp.int32, (tq, tq), 0))


def _mask_diag(s, diag):
    tq = diag.shape[0]
    width = s.shape[1]
    tail = jnp.where(diag, s[:, width - tq:], NEG_INF)
    return tail if width == tq else jnp.concatenate([s[:, :width - tq], tail], axis=1)


def _proj_kernel(*refs, n_alias, leaves):
    x_ref, w_ref, aug_ref, bfg_ref = refs[:4]
    (qa_ref, na_ref, qb_ref, nb_ref, iq_ref, kr_ref, qc_ref, nc_ref, dc_ref, misc_ref, kidx_ref,
     logf_ref) = refs[4 + n_alias:16 + n_alias]
    leaf_refs = refs[16 + n_alias:]
    xb = x_ref[...].astype(BF16)

    def seg(name):
        lo, hi = SEG[name]
        return _dot(xb, w_ref[:, lo:hi])

    def qseg(name):
        lo, hi = SEG[name]
        return (seg(name) + aug_ref[:, lo:hi]).astype(BF16)

    news = {"na": seg("na"), "nb": seg("nb"), "nc": seg("nc")}
    qa_ref[...] = qseg("qa")
    na_ref[...] = news["na"]
    qb_ref[...] = qseg("qb")
    nb_ref[...] = news["nb"]
    iq_ref[...] = seg("iq").astype(BF16)
    kr_ref[...] = seg("krep").astype(BF16)
    qc_ref[...] = qseg("qc")
    nc_ref[...] = news["nc"]
    for name, ref in zip(leaves, leaf_refs):
        v = news[name]
        ref[...] = v.reshape(v.shape[0], N_HEADS, v.shape[1] // N_HEADS)
    dc_ref[...] = seg("dconv")
    kidx_ref[...] = seg("kidx")[:, :D_IDX]
    misc = seg("misc")
    misc_ref[...] = misc
    logf_ref[...] = _log_sigmoid(misc + bfg_ref[...])[:, MISC_CF:MISC_CF + H_C]


LEAF_NAMES = ("na", "nb", "nc")


def _proj(x2d, wts, layer, tm, leaf_bufs=None, depth=None):
    n = x2d.shape[0]
    widths = dict((k, v[1] - v[0]) for k, v in SEG.items())
    outs = [("qa", BF16, widths["qa"]), ("na", F32, widths["na"]), ("qb", BF16, widths["qb"]),
            ("nb", F32, widths["nb"]), ("iq", BF16, widths["iq"]), ("krep", BF16, widths["krep"]),
            ("qc", BF16, widths["qc"]), ("nc", F32, widths["nc"]), ("dconv", F32, widths["dconv"]),
            ("misc", F32, widths["misc"]), ("kidx", F32, D_IDX), ("logf", F32, H_C)]
    in_specs = [pl.BlockSpec((tm, D_MODEL), lambda i: (i, 0)),
                _layer_spec((D_MODEL, W1_WIDTH), layer),
                _const_spec((1, W1_WIDTH)),
                _layer_spec((1, LANES), layer)]
    args = [x2d, wts["w1"], wts["aug_row"], wts["bfg_row"]]
    out_specs = [pl.BlockSpec((tm, w), lambda i: (i, 0)) for _, _, w in outs]
    out_shape = [jax.ShapeDtypeStruct((n, w), dt) for _, dt, w in outs]
    leaves, aliases, n_alias = (), {}, 0
    if leaf_bufs is not None:
        leaves = LEAF_NAMES
        for k, name in enumerate(leaves):
            hw = widths[name] // N_HEADS
            out_specs.append(pl.BlockSpec((None, tm, N_HEADS, hw), lambda i: (layer, i, 0, 0)))
            out_shape.append(jax.ShapeDtypeStruct((depth, n, N_HEADS, hw), F32))
            if leaf_bufs != "new":
                in_specs.append(pl.BlockSpec(memory_space=pl.ANY))
                args.append(leaf_bufs[k])
                aliases[len(args) - 1] = len(outs) + k
        n_alias = len(aliases)
    res = pl.pallas_call(
        functools.partial(_proj_kernel, n_alias=n_alias, leaves=leaves),
        grid=(n // tm,),
        in_specs=in_specs, out_specs=out_specs, out_shape=out_shape,
        input_output_aliases=aliases,
        compiler_params=_cparams("parallel"),
        name="proj",
    )(*args)
    out = dict(zip([o[0] for o in outs], res[:len(outs)]))
    out["leaf_bufs"] = tuple(res[len(outs):])
    return out


def _kth_largest_i16(vals, need):
    rows, width = vals.shape

    def count_ge(t):
        m = jnp.where(vals >= t.astype(jnp.int16), jnp.int16(1), jnp.int16(0))
        part = m[:, 0:LANES]
        for j in range(1, width // LANES):
            part = part + m[:, j * LANES:(j + 1) * LANES]
        return jnp.sum(part.astype(F32), axis=1, keepdims=True)

    t0 = jnp.where(count_ge(jnp.zeros((rows, 1), jnp.int32)) >= need, jnp.int32(0), jnp.int32(-32768))

    def body(i, t):
        cand = t + jnp.left_shift(jnp.int32(1), jnp.int32(14) - i)
        return jnp.where(count_ge(cand) >= need, cand, t)

    return lax.fori_loop(0, 15, body, t0, unroll=3), count_ge


def _topk_select(score, k):
    rows, width = score.shape
    kf = jnp.float32(k)
    big = jnp.float32(3.0e38)

    def count(mask):
        return jnp.sum(jnp.where(mask, 1.0, 0.0), axis=1, keepdims=True)

    x = score * 0.5
    finite = x > NEG_INF
    short = count(finite) < kf
    lo0 = jnp.min(jnp.where(finite, x, big), axis=1, keepdims=True)
    hi0 = jnp.max(jnp.where(finite, x, -big), axis=1, keepdims=True)
    need = jnp.full((rows, 1), kf, F32)

    def active(lo, hi):
        return (hi > lo) & jnp.logical_not(short)

    def refine(state):
        lo, hi = state
        act = active(lo, hi)
        w = hi - lo
        up = jnp.where(w < 1e-30, jnp.float32(2.0 ** 64), jnp.float32(1.0))
        scale = 65533.0 / jnp.where(act, w * up, 65533.0)
        v = jnp.maximum(jnp.floor(((x - lo) * up) * scale), -1.0) - 32767.0
        v = jnp.where(x > hi, 32767.0, v)
        b, _ = _kth_largest_i16(v.astype(jnp.int32).astype(jnp.int16), need)
        inb = v == b.astype(F32)
        new_lo = jnp.min(jnp.where(inb, x, big), axis=1, keepdims=True)
        new_hi = jnp.max(jnp.where(inb, x, -big), axis=1, keepdims=True)
        return jnp.where(act, new_lo, lo), jnp.where(act, new_hi, hi)

    def unsettled(state):
        lo, hi = state
        return jnp.max(jnp.where(active(lo, hi), 1.0, 0.0)) > 0.0

    lo, _ = lax.while_loop(unsettled, refine, (lo0, hi0))
    t = jnp.where(short, NEG_INF, lo)
    gt = x > t
    eq = x == t
    n_gt = count(gt)
    need = kf - n_gt
    idx = lax.broadcasted_iota(jnp.int32, (rows, width), 1)
    nbits = max(1, int(math.ceil(math.log2(width))))

    surplus = jnp.where(t > NEG_INF, n_gt + count(eq) - kf, 0.0)
    tied = jnp.max(surplus) > 0.0

    def tie_cut():
        def ibody(i, j):
            cand = j + jnp.left_shift(jnp.int32(1), jnp.int32(nbits - 1) - i)
            return jnp.where(count(eq & (idx < cand)) < need, cand, j)

        return lax.fori_loop(0, nbits, ibody, jnp.zeros((rows, 1), jnp.int32))

    j = lax.cond(tied, tie_cut, lambda: jnp.full((rows, 1), width, jnp.int32))
    return gt | (eq & (idx <= j))


def _attn_a_kernel(qa_ref, na_ref, lam_ref, g_ref, o_ref, k1_s, k2_s, v_s, *, tq, seq, lam_init):
    qi = pl.program_id(1)

    @pl.when(qi == 0)
    def _():
        lane = lax.broadcasted_iota(jnp.int32, (seq, LANES), 1)
        kp = lax.broadcasted_iota(jnp.int32, (seq, LANES), 0)
        kaug = jnp.where(lane == 0, (kp & 255).astype(F32),
                         jnp.where(lane == 1, (kp >> 8).astype(F32), 0.0)).astype(BF16)
        for h in range(H_A):
            kk = na_ref[0, :, h * 256:h * 256 + 128]
            k1_s[h, :, 0:LANES] = jnp.where(lane < 64, kk, 0.0).astype(BF16)
            k1_s[h, :, LANES:2 * LANES] = kaug
            k2_s[h, :, 0:LANES] = jnp.where(lane >= 64, kk, 0.0).astype(BF16)
            k2_s[h, :, LANES:2 * LANES] = kaug
            v_s[h] = na_ref[0, :, h * 256 + 128:(h + 1) * 256].astype(BF16)

    lam = _lambda(lam_ref, lam_init)

    def body(width):
        diag = _diag_mask(tq)
        for h in range(H_A):
            qh = qa_ref[0, :, h * 256:(h + 1) * 256]
            s1 = _mask_diag(_dot_nt(qh, k1_s[h, 0:width, :]), diag)
            s2 = _mask_diag(_dot_nt(qh, k2_s[h, 0:width, :]), diag)
            e1 = jnp.exp(s1 - jnp.max(s1, axis=1, keepdims=True))
            e2 = jnp.exp(s2 - jnp.max(s2, axis=1, keepdims=True))
            r1 = 1.0 / jnp.sum(e1, axis=1, keepdims=True)
            r2 = lam / jnp.sum(e2, axis=1, keepdims=True)
            p = (e1 * r1 - e2 * r2).astype(BF16)
            o = _dot(p, v_s[h, 0:width, :])
            o = o * lax.rsqrt(jnp.mean(o * o, axis=1, keepdims=True) + NORM_EPS)
            o = o * g_ref[h:h + 1, :] * (1.0 - lam_init)
            o_ref[0, :, h * 128:(h + 1) * 128] = o.astype(BF16)

    _for_causal_extent(qi, tq, seq, body)


def _attn_a(qa, na, wts, layer, lam_init, tq):
    bsz, seq, _ = qa.shape
    return pl.pallas_call(
        functools.partial(_attn_a_kernel, tq=tq, seq=seq, lam_init=lam_init),
        grid=(bsz, seq // tq),
        in_specs=[pl.BlockSpec((1, tq, 1024), lambda b, i: (b, i, 0)),
                  pl.BlockSpec((1, seq, 1024), lambda b, i: (b, 0, 0)),
                  _layer_spec((4, DK_A), layer),
                  _layer_spec((H_A, 2 * DK_A), layer)],
        out_specs=pl.BlockSpec((1, tq, A_W), lambda b, i: (b, i, 0)),
        out_shape=jax.ShapeDtypeStruct((bsz, seq, A_W), BF16),
        scratch_shapes=[pltpu.VMEM((H_A, seq, 2 * LANES), BF16), pltpu.VMEM((H_A, seq, 2 * LANES), BF16),
                        pltpu.VMEM((H_A, seq, LANES), BF16)],
        compiler_params=_cparams("parallel", "arbitrary"),
        name="attn_a",
    )(qa, na, wts["lam_p"], wts["g"])


def _attn_c_kernel(qc_ref, nc_ref, misc_ref, bfg_ref, o_ref, k_s, v_s, *, tq, seq):
    qi = pl.program_id(1)

    @pl.when(qi == 0)
    def _():
        lane = lax.broadcasted_iota(jnp.int32, (seq, LANES), 1)
        ch = min(256, seq)
        tri = jnp.where(lax.broadcasted_iota(jnp.int32, (ch, ch), 0)
                        >= lax.broadcasted_iota(jnp.int32, (ch, ch), 1), 1.0, 0.0).astype(BF16)
        carry = jnp.zeros((1, LANES), F32)
        chunks = []
        for c in range(seq // ch):
            lf = _log_sigmoid(misc_ref[0, c * ch:(c + 1) * ch, :] + bfg_ref[...])
            hi, mid, lo = _split3(lf)
            fc = (_dot(tri, hi) + _dot(tri, mid)) + _dot(tri, lo) + carry
            carry = fc[ch - 1:ch, :]
            chunks.append(fc)
        fcum = jnp.concatenate(chunks, axis=0) if len(chunks) > 1 else chunks[0]
        for h in range(H_C):
            fh = jnp.broadcast_to(fcum[:, MISC_CF + h:MISC_CF + h + 1], (seq, LANES))
            hi, mid, lo = (v.astype(F32) for v in _split3(-fh))
            kv = nc_ref[0, :, h * 128:(h + 1) * 128]
            aug = jnp.where(lane == QBC_AUG, hi,
                            jnp.where(lane == QBC_AUG + 1, mid, jnp.where(lane == QBC_AUG + 2, lo, 0.0)))
            k_s[h] = jnp.where(lane < 64, kv, aug).astype(BF16)
            v_s[h] = kv.astype(BF16)

    def body(width):
        diag = _diag_mask(tq)
        for h in range(H_C):
            qh = qc_ref[0, :, h * 128:(h + 1) * 128]
            s = _mask_diag(_dot_nt(qh, k_s[h, 0:width, :]), diag)
            e = jnp.exp(s - jnp.max(s, axis=1, keepdims=True))
            r = 1.0 / jnp.sum(e, axis=1, keepdims=True)
            o = _dot(e.astype(BF16), v_s[h, 0:width, :]) * r
            o_ref[0, :, h * 64:(h + 1) * 64] = o[:, 64:128].astype(BF16)

    _for_causal_extent(qi, tq, seq, body)


def _attn_c(qc, nc, misc, wts, layer, tq):
    bsz, seq, _ = qc.shape
    return pl.pallas_call(
        functools.partial(_attn_c_kernel, tq=tq, seq=seq),
        grid=(bsz, seq // tq),
        in_specs=[pl.BlockSpec((1, tq, 512), lambda b, i: (b, i, 0)),
                  pl.BlockSpec((1, seq, 512), lambda b, i: (b, 0, 0)),
                  pl.BlockSpec((1, seq, LANES), lambda b, i: (b, 0, 0)),
                  _layer_spec((1, LANES), layer)],
        out_specs=pl.BlockSpec((1, tq, C_W), lambda b, i: (b, i, 0)),
        out_shape=jax.ShapeDtypeStruct((bsz, seq, C_W), BF16),
        scratch_shapes=[pltpu.VMEM((H_C, seq, LANES), BF16)] * 2,
        compiler_params=_cparams("parallel", "arbitrary"),
        name="attn_c",
    )(qc, nc, misc, wts["bfg_row"])


def _attn_b_kernel(*refs, tq, seq, topk, has_tail):
    qb_ref, nb_ref, iq_ref, kr_ref, misc_ref = refs[:5]
    tail_ref = refs[5] if has_tail else None
    o_ref, k_s, v_s, kr_s = refs[5 + int(has_tail):]
    qi = pl.program_id(1)

    @pl.when(qi == 0)
    def _():
        lane = lax.broadcasted_iota(jnp.int32, (seq, LANES), 1)
        kp = lax.broadcasted_iota(jnp.int32, (seq, LANES), 0)
        aug = jnp.where(lane == QBC_AUG, (kp & 255).astype(F32),
                        jnp.where(lane == QBC_AUG + 1, (kp >> 8).astype(F32), 0.0))
        for h in range(H_B):
            kv = nb_ref[0, :, h * 128:(h + 1) * 128]
            k_s[h] = jnp.where(lane < 64, kv, aug).astype(BF16)
            v_s[h] = kv.astype(BF16)
        kr = kr_ref[0]
        klane = lax.broadcasted_iota(jnp.int32, (seq, 256), 1)
        for h in range(H_IDX):
            kr_s[h] = kr * jnp.where((klane >> 5) == h, 1.0, 0.0).astype(BF16)

    def body(width):
        rows = qi * tq + lax.broadcasted_iota(jnp.int32, (tq, width), 0)
        cols = lax.broadcasted_iota(jnp.int32, (tq, width), 1)
        causal = cols <= rows
        iq = iq_ref[0]
        score = None
        for h in range(H_IDX):
            rel = jnp.maximum(_dot_nt(iq, kr_s[h, 0:width, :]), 0.0) * misc_ref[0, :, MISC_WI + h:MISC_WI + h + 1]
            score = rel if score is None else score + rel
        if has_tail and width == seq:
            last = jnp.where(qi == seq // tq - 1, tail_ref[0], score[tq - SUBLANES:, :])
            score = jnp.concatenate([score[:tq - SUBLANES, :], last], axis=0)
        sel = _topk_select(jnp.where(causal, score, NEG_INF), topk)
        bias = jnp.where(sel & causal, 0.0, NEG_INF)
        for h in range(H_B):
            qh = qb_ref[0, :, h * 128:(h + 1) * 128]
            s = _dot_nt(qh, k_s[h, 0:width, :]) + bias
            e = jnp.exp(s - jnp.max(s, axis=1, keepdims=True))
            r = 1.0 / jnp.sum(e, axis=1, keepdims=True)
            o = _dot(e.astype(BF16), v_s[h, 0:width, :]) * r
            o_ref[0, :, h * 64:(h + 1) * 64] = o[:, 64:128].astype(BF16)

    _for_causal_extent(qi, tq, seq, body, tiles_per_extent=2)


def _attn_b(qb, nb, iq, krep, misc, tail, tq):
    bsz, seq, _ = qb.shape
    topk = min(INDEX_TOPK_MAX, seq // 4)
    has_tail = tail is not None
    tail_specs = [pl.BlockSpec((1, SUBLANES, seq), lambda b, i: (b, 0, 0))] if has_tail else []
    return pl.pallas_call(
        functools.partial(_attn_b_kernel, tq=tq, seq=seq, topk=topk, has_tail=has_tail),
        grid=(bsz, seq // tq),
        in_specs=[pl.BlockSpec((1, tq, 512), lambda b, i: (b, i, 0)),
                  pl.BlockSpec((1, seq, 512), lambda b, i: (b, 0, 0)),
                  pl.BlockSpec((1, tq, 256), lambda b, i: (b, i, 0)),
                  pl.BlockSpec((1, seq, 256), lambda b, i: (b, 0, 0)),
                  pl.BlockSpec((1, tq, LANES), lambda b, i: (b, i, 0))] + tail_specs,
        out_specs=pl.BlockSpec((1, tq, B_W), lambda b, i: (b, i, 0)),
        out_shape=jax.ShapeDtypeStruct((bsz, seq, B_W), BF16),
        scratch_shapes=[pltpu.VMEM((H_B, seq, LANES), BF16), pltpu.VMEM((H_B, seq, LANES), BF16),
                        pltpu.VMEM((H_IDX, seq, H_IDX * D_IDX), BF16)],
        compiler_params=_cparams("parallel", "arbitrary"),
        name="attn_b",
    )(qb, nb, iq, krep, misc, *([tail] if has_tail else []))


def _merge_kernel(*refs, tm, sample, tiles_per_seq):
    if sample:
        (x_ref, oa_ref, ob_ref, oc_ref, dc_ref, st_ref, wg_ref, pa_ref, pb_ref, pc_ref, pd_ref,
         wo_ref, g_ref, b_ref, cw_ref, wfi_ref, wfo_ref, g2_ref, b2_ref, y_ref, conv_ref) = refs
    else:
        (x_ref, oa_ref, ob_ref, oc_ref, dc_ref, halo_ref, wg_ref, pa_ref, pb_ref, pc_ref, pd_ref,
         wo_ref, g_ref, b_ref, cw_ref, wfi_ref, wfo_ref, g2_ref, b2_ref, y_ref, conv_ref, uu_ref) = refs
    w0 = cw_ref[0:1, :]
    w1 = cw_ref[1:2, :]
    w2 = cw_ref[2:3, :]
    u = dc_ref[:, 256:512] * dc_ref[:, 512:768]
    if sample:
        s0 = st_ref[:, 0:256]
        s1 = st_ref[:, 256:512]
        y = s0 * w0 + s1 * w1 + u * w2
        conv_ref[:, 0:256] = s1
        conv_ref[:, 256:512] = u
    else:
        first = (pl.program_id(0) % tiles_per_seq) == 0
        uh = halo_ref[:, 256:512] * halo_ref[:, 512:768]
        uu_ref[0:SUBLANES, :] = jnp.where(first, 0.0, uh)
        uu_ref[SUBLANES:SUBLANES + tm, :] = u
        y = uu_ref[SUBLANES - 2:SUBLANES - 2 + tm, :] * w0 + uu_ref[SUBLANES - 1:SUBLANES - 1 + tm, :] * w1 + u * w2
        conv_ref[0] = uu_ref[SUBLANES + tm - 2:SUBLANES + tm, :]
    o_d = (dc_ref[:, 0:256] * y).astype(BF16)
    xv = x_ref[...]
    xb = xv.astype(BF16)
    mixed = None
    for i, (o, p_ref) in enumerate(((oa_ref[...], pa_ref), (ob_ref[...], pb_ref),
                                    (oc_ref[...], pc_ref), (o_d, pd_ref))):
        gate = jax.nn.sigmoid(_dot(xb, wg_ref[:, i * D_MODEL:(i + 1) * D_MODEL]))
        term = gate * _dot(o, p_ref[...])
        mixed = term if mixed is None else mixed + term
    h = ALPHA * xv + _dot(mixed.astype(BF16), wo_ref[...])
    x1 = _layernorm(h, g_ref[...], b_ref[...])
    x1b = x1.astype(BF16)
    hg = _dot(x1b, wfi_ref[:, 0:D_FF])
    hu = _dot(x1b, wfi_ref[:, D_FF:2 * D_FF])
    act = (hg * jax.nn.sigmoid(hg) * hu).astype(BF16)
    y_ref[...] = _layernorm(ALPHA * x1 + _dot(act, wfo_ref[...]), g2_ref[...], b2_ref[...])


def _merge(x2d, oa, ob, oc, dconv, state2d, wts, layer, tm, seq):
    n = x2d.shape[0]
    sample = state2d is not None
    row = lambda w: pl.BlockSpec((tm, w), lambda i: (i, 0))
    in_specs = [row(D_MODEL), row(A_W), row(B_W), row(C_W), row(768)]
    args = [x2d, oa, ob, oc, dconv]
    if sample:
        in_specs.append(row(512))
        args.append(state2d)
        out_specs = [row(D_MODEL), row(512)]
        out_shape = [jax.ShapeDtypeStruct((n, D_MODEL), F32), jax.ShapeDtypeStruct((n, 512), F32)]
        scratch = []
        tiles_per_seq = 1
    else:
        tiles_per_seq = seq // tm
        r8 = tm // SUBLANES
        in_specs.append(pl.BlockSpec((SUBLANES, 768), lambda i: (jnp.maximum(i * r8 - 1, 0), 0)))
        args.append(dconv)
        out_specs = [row(D_MODEL), pl.BlockSpec((1, CONV_K - 1, CONV_W), lambda i: (i // tiles_per_seq, 0, 0))]
        out_shape = [jax.ShapeDtypeStruct((n, D_MODEL), F32),
                     jax.ShapeDtypeStruct((n // seq, CONV_K - 1, CONV_W), F32)]
        scratch = [pltpu.VMEM((tm + SUBLANES, CONV_W), F32)]
    for name in ("wg", "pa", "pb", "pc", "pd", "wo", "ln1_g", "ln1_b", "conv_w", "wf_in", "wf_out", "ln2_g", "ln2_b"):
        in_specs.append(_layer_spec(wts[name].shape[1:], layer))
        args.append(wts[name])
    return pl.pallas_call(
        functools.partial(_merge_kernel, tm=tm, sample=sample, tiles_per_seq=tiles_per_seq),
        grid=(n // tm,),
        in_specs=in_specs, out_specs=out_specs, out_shape=out_shape, scratch_shapes=scratch,
        compiler_params=_cparams("arbitrary"),
        name="merge_s" if sample else "merge",
    )(*args)


def _ffn_kernel(x_ref, wi_ref, wo_ref, g_ref, b_ref, y_ref):
    xv = x_ref[...]
    xb = xv.astype(BF16)
    hg = _dot(xb, wi_ref[:, 0:D_FF])
    hu = _dot(xb, wi_ref[:, D_FF:2 * D_FF])
    act = (hg * jax.nn.sigmoid(hg) * hu).astype(BF16)
    h = ALPHA * xv + _dot(act, wo_ref[...])
    y_ref[...] = _layernorm(h, g_ref[...], b_ref[...])


def _ffn(x2d, wts, layer, tm):
    n = x2d.shape[0]
    return pl.pallas_call(
        _ffn_kernel,
        grid=(n // tm,),
        in_specs=[pl.BlockSpec((tm, D_MODEL), lambda i: (i, 0)),
                  _layer_spec((D_MODEL, 2 * D_FF), layer), _layer_spec((D_FF, D_MODEL), layer),
                  _layer_spec((1, D_MODEL), layer), _layer_spec((1, D_MODEL), layer)],
        out_specs=pl.BlockSpec((tm, D_MODEL), lambda i: (i, 0)),
        out_shape=jax.ShapeDtypeStruct((n, D_MODEL), F32),
        compiler_params=_cparams("parallel"),
        name="ffn",
    )(x2d, wts["wf_in"], wts["wf_out"], wts["ln2_g"], wts["ln2_b"])


def _build_prefix_matrix():
    c = np.arange(PKEYS)
    same_head = (c[:, None] % N_HEADS) == (c[None, :] % N_HEADS)
    upto = (c[:, None] // N_HEADS) <= (c[None, :] // N_HEADS)
    return np.concatenate([same_head & upto, same_head], axis=1).astype(np.float32)


def _build_key_expansion():
    return (np.arange(PAGE_SIZE)[:, None] == (np.arange(PKEYS)[None, :] // N_HEADS)).astype(np.float32)


def _pfx_kernel(x_ref, u_ref, o_ref):
    hi, mid, lo = _split3(x_ref[...])
    u = u_ref[...]
    o_ref[...] = (_dot(hi, u) + _dot(mid, u)) + _dot(lo, u)


def _page_prefix(logf_pages, umat):
    n_pool = logf_pages.shape[0]
    tp = _row_tile(n_pool, 512) if n_pool % SUBLANES == 0 else n_pool
    return pl.pallas_call(
        _pfx_kernel,
        grid=(n_pool // tp,),
        in_specs=[pl.BlockSpec((tp, PKEYS), lambda i: (i, 0)),
                  _const_spec(umat.shape)],
        out_specs=pl.BlockSpec((tp, 2 * PKEYS), lambda i: (i, 0)),
        out_shape=jax.ShapeDtypeStruct((n_pool, 2 * PKEYS), F32),
        compiler_params=_cparams("parallel"),
        name="page_prefix",
    )(logf_pages, umat)


def _dot_split(a, b, dot):
    a0, a1, a2 = _split3(a)
    b0, b1, b2 = _split3(b)
    small = (dot(a0, b2) + dot(a2, b0)) + dot(a1, b1)
    return (small + (dot(a0, b1) + dot(a1, b0))) + dot(a0, b0)


def _idx_proj_kernel(x_ref, w_ref, o_ref):
    o_ref[...] = _dot_split(x_ref[...], w_ref[...], _dot)


def _idx_proj(x2d, w_idx, layer):
    return pl.pallas_call(
        _idx_proj_kernel,
        out_shape=jax.ShapeDtypeStruct((x2d.shape[0], w_idx.shape[2]), F32),
        compiler_params=pltpu.CompilerParams(vmem_limit_bytes=VMEM_LIMIT_BYTES),
        name="idx_proj",
    )(x2d, w_idx[layer])


def _idx_keys(x2d, w_key, tm):
    n = x2d.shape[0]
    return pl.pallas_call(
        _idx_proj_kernel,
        grid=(n // tm,),
        in_specs=[pl.BlockSpec((tm, D_MODEL), lambda i: (i, 0)), _const_spec(w_key.shape)],
        out_specs=pl.BlockSpec((tm, LANES), lambda i: (i, 0)),
        out_shape=jax.ShapeDtypeStruct((n, LANES), F32),
        compiler_params=_cparams("parallel"),
        name="idx_keys",
    )(x2d, w_key)


def _tail_score_kernel(z_ref, k_ref, o_ref):
    k0, k1, k2 = _split3(k_ref[0][:, 0:D_IDX])
    score = None
    for h in range(H_IDX):
        q0, q1, q2 = _split3(z_ref[0][:, h * D_IDX:(h + 1) * D_IDX])
        small = (_dot_nt(q0, k2) + _dot_nt(q2, k0)) + _dot_nt(q1, k1)
        rel = jnp.maximum((small + (_dot_nt(q0, k1) + _dot_nt(q1, k0))) + _dot_nt(q0, k0), 0.0)
        w = z_ref[0][:, H_IDX * D_IDX + D_IDX + h:H_IDX * D_IDX + D_IDX + h + 1] * (H_IDX ** -0.5 * D_IDX ** -0.5)
        score = rel * w if score is None else score + rel * w
    o_ref[0] = score


def _tail_scores(zt, keys):
    bsz, seq, _ = keys.shape
    return pl.pallas_call(
        _tail_score_kernel,
        grid=(bsz,),
        in_specs=[pl.BlockSpec((1, SUBLANES, IDX_COLS), lambda b: (b, 0, 0)),
                  pl.BlockSpec((1, seq, LANES), lambda b: (b, 0, 0))],
        out_specs=pl.BlockSpec((1, SUBLANES, seq), lambda b: (b, 0, 0)),
        out_shape=jax.ShapeDtypeStruct((bsz, SUBLANES, seq), F32),
        compiler_params=_cparams("parallel"),
        name="tail_scores",
    )(zt, keys)


def _idx_kernel(pt_ref, *refs, n_pages):
    pages = refs[:n_pages]
    iq_ref, wi_ref, knew_ref, o_ref = refs[n_pages:]
    iq = iq_ref[0]
    wi = wi_ref[0] * (H_IDX ** -0.5 * D_IDX ** -0.5)

    def weighted(rel):
        return jnp.sum(jnp.maximum(rel, 0.0) * wi, axis=0, keepdims=True)

    kt = jnp.concatenate([pages[j][0] for j in range(n_pages)], axis=1)
    o_ref[0, :, 0:n_pages * PAGE_SIZE] = weighted(_dot_split(iq, kt, _dot))
    knew = jnp.broadcast_to(knew_ref[0], (PAGE_SIZE, D_IDX))
    lane = lax.broadcasted_iota(jnp.int32, (1, PAGE_SIZE), 1)
    o_ref[0, :, n_pages * PAGE_SIZE:(n_pages + 1) * PAGE_SIZE] = jnp.where(
        lane == 0, weighted(_dot_split(iq, knew, _dot_nt)), NEG_INF)


def _idx_scores(page_table, kidx_t, page_base, iq8, wi8, knew):
    nsamp, n_pages = page_table.shape
    width = (n_pages + 1) * PAGE_SIZE

    def page_spec(j):
        return pl.BlockSpec((1, D_IDX, PAGE_SIZE), lambda s, pt: (page_base + pt[s, j], 0, 0))

    grid_spec = pltpu.PrefetchScalarGridSpec(
        num_scalar_prefetch=1,
        grid=(nsamp,),
        in_specs=[page_spec(j) for j in range(n_pages)] + [
            pl.BlockSpec((1, H_IDX, D_IDX), lambda s, pt: (s, 0, 0)),
            pl.BlockSpec((1, H_IDX, 1), lambda s, pt: (s, 0, 0)),
            pl.BlockSpec((1, 1, D_IDX), lambda s, pt: (s, 0, 0))],
        out_specs=pl.BlockSpec((1, 1, width), lambda s, pt: (s, 0, 0)),
    )
    out = pl.pallas_call(
        functools.partial(_idx_kernel, n_pages=n_pages),
        grid_spec=grid_spec,
        out_shape=jax.ShapeDtypeStruct((nsamp, 1, width), F32),
        compiler_params=_cparams("arbitrary"),
        name="idx_scores",
    )(page_table, *([kidx_t] * n_pages), iq8, wi8, knew)
    return out.reshape(nsamp, width)


def _sel_kernel(s_ref, e_ref, o_ref, *, topk, n_blocks):
    sel = jnp.where(_topk_select(s_ref[...], topk), 1.0, 0.0).astype(BF16)
    for j in range(n_blocks):
        rep = _dot(sel[:, j * PAGE_SIZE:(j + 1) * PAGE_SIZE], e_ref[...])
        o_ref[:, j * PKEYS:(j + 1) * PKEYS] = jnp.where(rep > 0.5, 0.0, NEG_INF)


def _select_bias(scores, expand, topk):
    nsamp, width = scores.shape
    n_blocks = width // PAGE_SIZE
    return pl.pallas_call(
        functools.partial(_sel_kernel, topk=topk, n_blocks=n_blocks),
        out_shape=jax.ShapeDtypeStruct((nsamp, n_blocks * PKEYS), F32),
        compiler_params=pltpu.CompilerParams(vmem_limit_bytes=VMEM_LIMIT_BYTES),
        name="select_bias",
    )(scores, expand)


def _attn_s_kernel(pt_ref, *refs, n_pages, lam_init):
    g = n_pages
    ca = refs[0:g]
    cb = refs[g:2 * g]
    cc = refs[2 * g:3 * g]
    pf = refs[3 * g:4 * g]
    (selp_ref, seln_ref, qa_ref, qb_ref, qc_ref, na_ref, nb_ref, nc_ref, lfn_ref, lam_ref, g_ref,
     oa_ref, ob_ref, oc_ref, xa_s, xb_s, xc_s) = refs[4 * g:]

    sub_a = lax.broadcasted_iota(jnp.int32, (SUBLANES, 256), 0)
    lane_a = lax.broadcasted_iota(jnp.int32, (SUBLANES, 256), 1)
    qa_row = qa_ref[0].astype(F32)
    qa_blk = jnp.zeros((SUBLANES, 256), F32)
    for h in range(N_HEADS):
        qa_blk = jnp.where((sub_a >> 1) == h, jnp.broadcast_to(qa_row[:, h * 256:(h + 1) * 256], (SUBLANES, 256)),
                           qa_blk)
    qa_blk = jnp.where((lane_a < 128) & ((lane_a >> 6) == (sub_a & 1)), qa_blk, 0.0).astype(BF16)
    sub_b = lax.broadcasted_iota(jnp.int32, (SUBLANES, LANES), 0)
    lane_b = lax.broadcasted_iota(jnp.int32, (SUBLANES, LANES), 1)

    def head_block(row):
        blk = jnp.zeros((SUBLANES, LANES), F32)
        for h in range(N_HEADS):
            blk = jnp.where((sub_b & 3) == h, jnp.broadcast_to(row[:, h * 128:(h + 1) * 128], (SUBLANES, LANES)), blk)
        return jnp.where(lane_b < 64, blk, 0.0).astype(BF16)

    qb_blk = head_block(qb_ref[0].astype(F32))
    qc_blk = head_block(qc_ref[0].astype(F32))

    sub1 = lax.broadcasted_iota(jnp.int32, (SUBLANES, 1), 0)
    slope_a = jnp.zeros((SUBLANES, 1), F32)
    slope_b = jnp.zeros((SUBLANES, 1), F32)
    for h in range(N_HEADS):
        slope_a = jnp.where((sub1 >> 1) == h, SLOPES_A[h], slope_a)
        slope_b = jnp.where((sub1 & 3) == h, SLOPES_B[h], slope_b)

    def head_masks(width):
        sub = lax.broadcasted_iota(jnp.int32, (SUBLANES, width), 0)
        lane = lax.broadcasted_iota(jnp.int32, (SUBLANES, width), 1)
        own_a = jnp.where((lane & 3) == (sub >> 1), 0.0, NEG_INF)
        own_b = jnp.where((lane & 3) == (sub & 3), 0.0, NEG_INF)
        return own_a, own_b

    def new_rows(x_s, ref):
        x_s[...] = jnp.zeros(x_s.shape, F32)
        x_s[0:N_HEADS, :] = ref[0]
        return x_s[...].astype(BF16)

    def attend(qblk, blocks, bias):
        s = jnp.concatenate([_dot_nt(qblk, blk) for blk in blocks], axis=1) + bias
        e = jnp.exp(s - jnp.max(s, axis=1, keepdims=True))
        r = 1.0 / jnp.sum(e, axis=1, keepdims=True)
        eb = e.astype(BF16)
        pv, off = None, 0
        for blk in blocks:
            t = _dot(eb[:, off:off + blk.shape[0]], blk)
            pv = t if pv is None else pv + t
            off += blk.shape[0]
        return pv * r

    new_pos = float(n_pages * PAGE_SIZE)
    own_a, own_b = head_masks(g * PKEYS)
    own_a1, own_b1 = head_masks(PAGE_SIZE)
    lane1 = lax.broadcasted_iota(jnp.int32, (1, PAGE_SIZE), 1)
    valid = jnp.where(lane1 < N_HEADS, 0.0, NEG_INF)
    lane_g = lax.broadcasted_iota(jnp.int32, (1, g * PKEYS), 1)
    kpos = (lane_g >> 2).astype(F32)

    blocks_a = [r[0].reshape(PKEYS, 256).astype(BF16) for r in ca] + [new_rows(xa_s, na_ref)]
    bias_a = jnp.concatenate([slope_a * kpos + own_a, slope_a * new_pos + own_a1 + valid], axis=1)
    acc_a = attend(qa_blk, blocks_a, bias_a)

    selb = jnp.concatenate([selp_ref[0, j] for j in range(g)], axis=1)
    blocks_b = [r[0].reshape(PKEYS, 128).astype(BF16) for r in cb] + [new_rows(xb_s, nb_ref)]
    bias_b = jnp.concatenate([slope_b * kpos + selb + own_b,
                              slope_b * new_pos + seln_ref[0, 0][:, 0:PAGE_SIZE] + own_b1 + valid], axis=1)
    acc_b = attend(qb_blk, blocks_b, bias_b)

    fparts = []
    carry = jnp.zeros((1, PKEYS), F32)
    for j in range(g):
        fparts.append(pf[j][0, :, 0:PKEYS] + carry)
        carry = carry + pf[j][0, :, PKEYS:2 * PKEYS]
    f_new = carry[:, 0:PAGE_SIZE] + lfn_ref[0]
    blocks_c = [r[0].reshape(PKEYS, 128).astype(BF16) for r in cc] + [new_rows(xc_s, nc_ref)]
    bias_c = jnp.concatenate([own_b - jnp.concatenate(fparts, axis=1), own_b1 + valid - f_new], axis=1)
    acc_c = attend(qc_blk, blocks_c, bias_c)

    lam = _lambda(lam_ref, lam_init)
    for h in range(N_HEADS):
        o = acc_a[2 * h:2 * h + 1, 128:256] - lam * acc_a[2 * h + 1:2 * h + 2, 128:256]
        o = o * lax.rsqrt(jnp.mean(o * o, axis=1, keepdims=True) + NORM_EPS)
        o = o * g_ref[h:h + 1, :] * (1.0 - lam_init)
        oa_ref[0, :, h * 128:(h + 1) * 128] = o.astype(BF16)
        ob_ref[0, :, h * 64:(h + 1) * 64] = acc_b[h:h + 1, 64:128].astype(BF16)
        oc_ref[0, :, h * 64:(h + 1) * 64] = acc_c[h:h + 1, 64:128].astype(BF16)


def _attn_sample(page_table, page_base, ca, cb, cc, pfx, selb4, qa, qb, qc, na, nb, nc, lfn, wts, layer, lam_init):
    nsamp, n_pages = page_table.shape

    def page_spec(width, j):
        return pl.BlockSpec((1, PAGE_SIZE, N_HEADS, width), lambda s, pt: (page_base + pt[s, j], 0, 0, 0))

    def pf_spec(j):
        return pl.BlockSpec((1, 1, 2 * PKEYS), lambda s, pt: (page_base + pt[s, j], 0, 0))

    def samp(shape):
        return pl.BlockSpec((1,) + shape, lambda s, pt: (s,) + (0,) * len(shape))

    pages = range(n_pages)
    in_specs = ([page_spec(256, j) for j in pages] + [page_spec(128, j) for j in pages]
                + [page_spec(128, j) for j in pages] + [pf_spec(j) for j in pages]
                + [pl.BlockSpec((1, n_pages, 1, PKEYS), lambda s, pt: (s, 0, 0, 0)),
                   pl.BlockSpec((1, 1, 1, PKEYS), lambda s, pt: (s, n_pages, 0, 0)),
                   samp((1, 1024)), samp((1, 512)), samp((1, 512)),
                   samp((N_HEADS, 256)), samp((N_HEADS, 128)), samp((N_HEADS, 128)), samp((1, PAGE_SIZE)),
                   pl.BlockSpec((None, 4, DK_A), lambda s, pt: (layer, 0, 0)),
                   pl.BlockSpec((None, H_A, 2 * DK_A), lambda s, pt: (layer, 0, 0))])
    out_specs = [samp((1, A_W)), samp((1, B_W)), samp((1, C_W))]
    grid_spec = pltpu.PrefetchScalarGridSpec(
        num_scalar_prefetch=1, grid=(nsamp,), in_specs=in_specs, out_specs=out_specs,
        scratch_shapes=[pltpu.VMEM((PAGE_SIZE, 256), F32), pltpu.VMEM((PAGE_SIZE, 128), F32),
                        pltpu.VMEM((PAGE_SIZE, 128), F32)])
    return pl.pallas_call(
        functools.partial(_attn_s_kernel, n_pages=n_pages, lam_init=lam_init),
        grid_spec=grid_spec,
        out_shape=[jax.ShapeDtypeStruct((nsamp, 1, A_W), BF16),
                   jax.ShapeDtypeStruct((nsamp, 1, B_W), BF16),
                   jax.ShapeDtypeStruct((nsamp, 1, C_W), BF16)],
        compiler_params=_cparams("arbitrary"),
        name="attn_sample",
    )(page_table, *([ca] * n_pages), *([cb] * n_pages), *([cc] * n_pages), *([pfx] * n_pages),
      selb4, selb4, qa, qb, qc, na, nb, nc, lfn, wts["lam_p"], wts["g"])


def _prompt_layer(x2d, wts, layer, depth, bsz, seq, lam_init, leaf_bufs):
    n = bsz * seq
    pr = _proj(x2d, wts, layer, _row_tile(n, 512), leaf_bufs=leaf_bufs, depth=depth)
    r3 = lambda a: a.reshape(bsz, seq, a.shape[-1])
    tq = _row_tile(seq, 256)
    oa = _attn_a(r3(pr["qa"]), r3(pr["na"]), wts, layer, lam_init, tq)
    tail = None
    if layer < depth - 1:
        keys = _idx_keys(x2d, wts["w_key"][layer], _row_tile(n, 512)).reshape(bsz, seq, LANES)
        xt = x2d.reshape(bsz, seq, D_MODEL)[:, seq - SUBLANES:, :].reshape(bsz * SUBLANES, D_MODEL)
        zt = _idx_proj(xt, wts["w_idx"], layer).reshape(bsz, SUBLANES, IDX_COLS)
        tail = _tail_scores(zt, keys)
    ob = _attn_b(r3(pr["qb"]), r3(pr["nb"]), r3(pr["iq"]), r3(pr["krep"]), r3(pr["misc"]), tail, tq)
    oc = _attn_c(r3(pr["qc"]), r3(pr["nc"]), r3(pr["misc"]), wts, layer, tq)
    x1, new_conv = _merge(x2d, oa.reshape(n, A_W), ob.reshape(n, B_W), oc.reshape(n, C_W), pr["dconv"],
                          None, wts, layer, _row_tile(seq, 256), seq)
    x2 = x1
    news = (pr["kidx"].reshape(bsz, seq, D_IDX), pr["logf"].reshape(bsz, seq, H_C), new_conv)
    return x2, news, pr["leaf_bufs"]


def _sample_layer(x2d, wts, layer, caches, page_base, state, page_table, lam_init):
    nsamp = x2d.shape[0]
    n_pages = page_table.shape[1]
    ca, cb, ckt, cc, pfx = caches
    tm = _row_tile(nsamp, 128)
    pr = _proj(x2d, wts, layer, tm)
    zi = _idx_proj(x2d, wts["w_idx"], layer)
    iq8 = zi[:, 0:H_IDX * D_IDX].reshape(nsamp, H_IDX, D_IDX)
    knew = zi[:, H_IDX * D_IDX:H_IDX * D_IDX + D_IDX].reshape(nsamp, 1, D_IDX)
    wi8 = zi[:, H_IDX * D_IDX + D_IDX:IDX_COLS].reshape(nsamp, H_IDX, 1)
    scores = _idx_scores(page_table, ckt, page_base, iq8, wi8, knew)
    topk = min(INDEX_TOPK_MAX, (n_pages * PAGE_SIZE + 1) // 4)
    selb4 = _select_bias(scores, wts["expand"], topk).reshape(nsamp, n_pages + 1, 1, PKEYS)
    lfn = jnp.pad(pr["logf"], ((0, 0), (0, PAGE_SIZE - H_C))).reshape(nsamp, 1, PAGE_SIZE)
    r3 = lambda a: a.reshape(nsamp, 1, a.shape[-1])
    r4 = lambda a: a.reshape(nsamp, N_HEADS, a.shape[-1] // N_HEADS)
    oa, ob, oc = _attn_sample(page_table, page_base, ca, cb, cc, pfx, selb4,
                              r3(pr["qa"]), r3(pr["qb"]), r3(pr["qc"]), r4(pr["na"]), r4(pr["nb"]), r4(pr["nc"]),
                              lfn, wts, layer, lam_init)
    x1, conv2 = _merge(x2d, oa.reshape(nsamp, A_W), ob.reshape(nsamp, B_W), oc.reshape(nsamp, C_W),
                       pr["dconv"], state.reshape(nsamp, (CONV_K - 1) * CONV_W), wts, layer, tm, 1)
    x2 = x1
    news = (pr["na"].reshape(nsamp, 1, H_A, 4 * DK_A), pr["nb"].reshape(nsamp, 1, H_B, 2 * DH_B),
            pr["kidx"].reshape(nsamp, 1, D_IDX), pr["nc"].reshape(nsamp, 1, H_C, 2 * DH_C),
            pr["logf"].reshape(nsamp, 1, H_C), conv2.reshape(nsamp, CONV_K - 1, CONV_W))
    return x2, news


def kernel(x_prompt, x_sample, cache_a_kv, cache_b_kv, cache_b_kidx, cache_c_kv, cache_c_logf, state_conv, page_table, w_in, b_fgate, lam_q1, lam_k1, lam_q2, lam_k2, g_diffnorm, conv_w, w_br_a, w_br_b, w_br_c, w_br_d, w_o, ln1_g, ln1_b, w_ffn_in, w_ffn_out, ln2_g, ln2_b):
    bsz, seq, _ = x_prompt.shape
    nsamp = x_sample.shape[0]
    depth = w_in.shape[0]
    bf = lambda a: a.astype(BF16)
    w_in_b = bf(w_in)
    wts = dict(
        w1=_rearranged_weights(w_in_b), wg=w_in_b[:, :, O_G:],
        aug_row=jnp.asarray(_build_q_aug_row()),
        bfg_row=jnp.zeros((depth, 1, LANES), F32).at[:, 0, MISC_CF:MISC_CF + H_C].set(b_fgate),
        lam_p=jnp.stack([lam_q1, lam_k1, lam_q2, lam_k2], axis=1),
        g=g_diffnorm, conv_w=conv_w, pa=bf(w_br_a), pb=bf(w_br_b), pc=bf(w_br_c), pd=bf(w_br_d), wo=bf(w_o),
        ln1_g=ln1_g[:, None, :], ln1_b=ln1_b[:, None, :], wf_in=bf(w_ffn_in), wf_out=bf(w_ffn_out),
        ln2_g=ln2_g[:, None, :], ln2_b=ln2_b[:, None, :],
        expand=jnp.asarray(_build_key_expansion(), dtype=BF16),
        w_idx=w_in[:, :, O_BIQ:O_BIQ + IDX_COLS],
        w_key=jnp.pad(w_in[:, :, O_BIK:O_BIK + D_IDX], ((0, 0), (0, 0), (0, LANES - D_IDX))))
    n_pool = cache_a_kv.shape[1]
    all_pages = depth * n_pool
    pfx = _page_prefix(cache_c_logf.reshape(all_pages, PKEYS), jnp.asarray(_build_prefix_matrix(), dtype=BF16))
    caches = (cache_a_kv.reshape(all_pages, PAGE_SIZE, N_HEADS, 4 * DK_A),
              cache_b_kv.reshape(all_pages, PAGE_SIZE, N_HEADS, 2 * DH_B),
              jnp.swapaxes(cache_b_kidx, 2, 3).reshape(all_pages, D_IDX, PAGE_SIZE),
              cache_c_kv.reshape(all_pages, PAGE_SIZE, N_HEADS, 2 * DH_C),
              pfx.reshape(all_pages, 1, 2 * PKEYS))

    yp = x_prompt.reshape(bsz * seq, D_MODEL)
    ys = x_sample.reshape(nsamp, D_MODEL)
    news_p, news_s = [], []
    leaf_bufs = "new"
    for l in range(depth):
        lam_init = 0.8 - 0.6 * math.exp(-0.3 * l)
        yp, new_p, leaf_bufs = _prompt_layer(yp, wts, l, depth, bsz, seq, lam_init, leaf_bufs)
        ys, new_s = _sample_layer(ys, wts, l, caches, l * n_pool, state_conv[l], page_table, lam_init)
        news_p.append(new_p)
        news_s.append(new_s)

    def stack(lst, i):
        return jnp.stack([e[i] for e in lst])

    kv_p = [b.reshape(depth, bsz, seq, N_HEADS, b.shape[-1]) for b in leaf_bufs]
    return (yp.reshape(bsz, seq, D_MODEL), ys.reshape(nsamp, 1, D_MODEL),
            kv_p[0], stack(news_s, 0), kv_p[1], stack(news_s, 1),
            stack(news_p, 0), stack(news_s, 2), kv_p[2], stack(news_s, 3),
            stack(news_p, 1), stack(news_s, 4), stack(news_p, 2), stack(news_s, 5))
```
